```python
import jax, jax.numpy as jnp
from jax import lax
import numpy as np

D_MODEL = 1024
BATCH = 4
SEQ = 4096
DEPTH = 4

CTX_LEN = 256
GRID_W = 64
HEAD_DIM = 64
A_HEADS = 8
A_KV_HEADS = 2
A_WINDOW = 128
A_BLOCK = 128
B_HEADS = 8
NA_ROWS = 8
NA_COLS = 16
C_CH = 512
C_KSIZE = 31
M_HEADS = 4
M_DIM = 128
M_CHUNK = 128
N_BRANCH = 4
BRANCH_W = 512
N_EXPERTS = 16
EXPERT_FF = 1024
CAPACITY_FACTOR = 2
ROPE_BASE = 10000.0
LN_EPS = 1e-6
NEG_INF = -1e30

PROJ_SIZES = (A_HEADS * HEAD_DIM, A_KV_HEADS * HEAD_DIM, A_KV_HEADS * HEAD_DIM,
              B_HEADS * HEAD_DIM, B_HEADS * HEAD_DIM, B_HEADS * HEAD_DIM,
              2 * C_CH,
              M_HEADS * M_DIM, M_HEADS * M_DIM, M_HEADS * M_DIM, M_HEADS * M_DIM, 4 * M_HEADS)
PROJ_W = sum(PROJ_SIZES)

kernel_name = 'hybrid_dit_parallel_mixers_ec_moe'


def _layer_norm(x, g=None, b=None):
    xf = x.astype(jnp.float32)
    mu = xf.mean(-1, keepdims=True)
    var = jnp.square(xf - mu).mean(-1, keepdims=True)
    y = (xf - mu) * lax.rsqrt(var + LN_EPS)
    if g is not None:
        y = y * g.astype(jnp.float32) + b.astype(jnp.float32)
    return y.astype(x.dtype)


def _split_proj(p):
    out, off = [], 0
    for n in PROJ_SIZES:
        out.append(p[..., off:off + n])
        off += n
    return out


def _heads(t, h):
    return t.reshape(t.shape[:2] + (h, -1))


def _axial_rope(x):
    T = x.shape[1]
    t = jnp.arange(T)
    half = HEAD_DIM // 2
    nf = half // 2
    inv = ROPE_BASE ** (-jnp.arange(nf, dtype=jnp.float32) / nf)

    def rot(xa, pos):
        ang = pos.astype(jnp.float32)[:, None] * inv[None, :]
        cos = jnp.cos(ang)[:, None, :].astype(x.dtype)
        sin = jnp.sin(ang)[:, None, :].astype(x.dtype)
        x1, x2 = xa[..., :nf], xa[..., nf:]
        return jnp.concatenate([x1 * cos - x2 * sin, x1 * sin + x2 * cos], axis=-1)

    return jnp.concatenate([rot(x[..., :half], t // GRID_W), rot(x[..., half:], t % GRID_W)], axis=-1)


def _windowed_gqa(q, k, v, kc, vc, sink):
    B, S = q.shape[:2]
    L = kc.shape[1]
    G = A_HEADS // A_KV_HEADS
    nb = S // A_BLOCK
    nk = 3 * A_BLOCK
    qb = q.reshape(B, nb, A_BLOCK, A_KV_HEADS, G, HEAD_DIM) * HEAD_DIM ** -0.5

    def band(t):
        tp = jnp.pad(t, ((0, 0), (A_BLOCK, A_BLOCK), (0, 0), (0, 0)))
        tp = tp.reshape(B, nb + 2, A_BLOCK, A_KV_HEADS, HEAD_DIM)
        return jnp.concatenate([tp[:, :-2], tp[:, 1:-1], tp[:, 2:]], axis=2)

    kb, vb = band(k), band(v)
    rel = jnp.arange(nk)[None, :] - jnp.arange(A_BLOCK)[:, None] - A_BLOCK
    in_band = jnp.abs(rel) <= A_WINDOW
    kpos = jnp.arange(nb)[:, None] * A_BLOCK - A_BLOCK + jnp.arange(nk)[None, :]
    in_seq = (kpos >= 0) & (kpos < S)
    mask = in_band[None] & in_seq[:, None, :]
    s_loc = jnp.einsum('bnqgrd,bnkgd->bngrqk', qb, kb).astype(jnp.float32)
    s_loc = jnp.where(mask[None, :, None, None], s_loc, NEG_INF)
    s_ctx = jnp.einsum('bnqgrd,bkgd->bngrqk', qb, kc).astype(jnp.float32)
    s_sink = jnp.broadcast_to(sink.astype(jnp.float32).reshape(1, 1, A_KV_HEADS, G, 1, 1), s_ctx.shape[:-1] + (1,))
    p = jax.nn.softmax(jnp.concatenate([s_loc, s_ctx, s_sink], axis=-1), axis=-1).astype(v.dtype)
    o = (jnp.einsum('bngrqk,bnkgd->bnqgrd', p[..., :nk], vb)
         + jnp.einsum('bngrqk,bkgd->bnqgrd', p[..., nk:nk + L], vc))
    return o.reshape(B, S, A_HEADS * HEAD_DIM)


def _ctx_attention(q, k, v, sink=None):
    B, L, H, hd = q.shape
    KV = k.shape[2]
    G = H // KV
    qg = q.reshape(B, L, KV, G, hd) * hd ** -0.5
    s = jnp.einsum('bqgrd,bkgd->bgrqk', qg, k).astype(jnp.float32)
    if sink is not None:
        s = jnp.concatenate([s, jnp.broadcast_to(sink.astype(jnp.float32).reshape(1, KV, G, 1, 1), s.shape[:-1] + (1,))], axis=-1)
    p = jax.nn.softmax(s, axis=-1)[..., :L].astype(v.dtype)
    o = jnp.einsum('bgrqk,bkgd->bqgrd', p, v)
    return o.reshape(B, L, H * hd)


def _neighborhood_attn(q, k, v, kc, vc, rpb):
    B, T, H, hd = q.shape
    rows = T // GRID_W
    wh = min(NA_ROWS, rows)
    grid = lambda t: t.reshape(B, rows, GRID_W, H, hd)
    qg = grid(q) * hd ** -0.5
    r = jnp.arange(rows)
    row_idx = jnp.clip(r - wh // 2, 0, rows - wh)[:, None] + jnp.arange(wh)[None, :]
    kg = grid(k)[:, row_idx]
    vg = grid(v)[:, row_idx]
    col = jnp.arange(GRID_W)
    c0 = jnp.clip(col - NA_COLS // 2, 0, GRID_W - NA_COLS)
    col_ok = (col[None, :] >= c0[:, None]) & (col[None, :] < c0[:, None] + NA_COLS)
    dr = row_idx - r[:, None] + NA_ROWS - 1
    dc = jnp.clip(col[None, :] - col[:, None] + NA_COLS - 1, 0, 2 * NA_COLS - 2)
    bias = rpb[:, dr[:, :, None, None], dc[None, None]].transpose(1, 0, 3, 2, 4)
    s_nb = jnp.einsum('brqhd,brwkhd->brhqwk', qg, kg).astype(jnp.float32) + bias.astype(jnp.float32)
    s_nb = jnp.where(col_ok[:, None, :], s_nb, NEG_INF).reshape(B, rows, H, GRID_W, wh * GRID_W)
    s_ctx = jnp.einsum('brqhd,bkhd->brhqk', qg, kc).astype(jnp.float32)
    p = jax.nn.softmax(jnp.concatenate([s_nb, s_ctx], axis=-1), axis=-1).astype(v.dtype)
    nk = wh * GRID_W
    o = (jnp.einsum('brhqwk,brwkhd->brqhd', p[..., :nk].reshape(B, rows, H, GRID_W, wh, GRID_W), vg)
         + jnp.einsum('brhqk,bkhd->brqhd', p[..., nk:], vc))
    return o.reshape(B, T, H * hd)


def _conformer_conv(u, w_dw, b_dw, g, b):
    a, gate = jnp.split(u, 2, axis=-1)
    y = a * jax.nn.sigmoid(gate)
    y = lax.conv_general_dilated(y, w_dw[:, None, :].astype(y.dtype), (1,), [(C_KSIZE // 2, C_KSIZE // 2)],
                                 dimension_numbers=('NWC', 'WIO', 'NWC'), feature_group_count=C_CH) + b_dw
    return jax.nn.silu(_layer_norm(y, g, b))


def _mlstm_chunk(carry, inp):
    C, n, m = carry
    q, k, v, ig, lf = inp
    Lc = q.shape[-2]
    F = jnp.cumsum(lf, axis=-1)
    a = F + m[..., None]
    causal = jnp.tril(jnp.ones((Lc, Lc), dtype=bool))
    logw = jnp.where(causal, F[..., :, None] - F[..., None, :] + ig[..., None, :], -jnp.inf)
    mt = jnp.maximum(a, logw.max(-1))
    w_inter = jnp.exp(a - mt)
    s = jnp.einsum('bhtd,bhsd->bhts', q, k) * jnp.exp(logw - mt[..., None])
    num = w_inter[..., None] * jnp.einsum('bhtd,bhde->bhte', q, C) + jnp.einsum('bhts,bhse->bhte', s, v)
    den = w_inter * jnp.einsum('bhtd,bhd->bht', q, n) + s.sum(-1)
    h = num / jnp.maximum(jnp.abs(den), jnp.exp(-mt))[..., None]
    FL = F[..., -1]
    g = FL[..., None] - F + ig
    m_new = jnp.maximum(FL + m, g.max(-1))
    decay = jnp.exp(FL + m - m_new)
    w = jnp.exp(g - m_new[..., None])
    C_new = decay[..., None, None] * C + jnp.einsum('bhs,bhsd,bhse->bhde', w, k, v)
    n_new = decay[..., None] * n + jnp.einsum('bhs,bhsd->bhd', w, k)
    return (C_new, n_new, m_new), h


def _mlstm_scan(q, k, v, ig, lf, state):
    B, H, T, d = q.shape
    nc = T // M_CHUNK
    chunks = lambda t: jnp.moveaxis(t.reshape(t.shape[:2] + (nc, M_CHUNK) + t.shape[3:]), 2, 0)
    state, h = lax.scan(_mlstm_chunk, state, (chunks(q), chunks(k), chunks(v), chunks(ig), chunks(lf)))
    return jnp.moveaxis(h, 0, 2).reshape(B, H, T, d), state


def _mlstm_inputs(q, k, v, gates, gate_b):
    B, T, _ = q.shape
    heads = lambda t: t.astype(jnp.float32).reshape(B, T, M_HEADS, M_DIM).transpose(0, 2, 1, 3)
    g = (gates.astype(jnp.float32) + gate_b.astype(jnp.float32)).reshape(B, T, 4, M_HEADS).transpose(2, 0, 3, 1)
    fwd = (g[0], jax.nn.log_sigmoid(g[1]))
    bwd = (g[2], jax.nn.log_sigmoid(g[3]))
    return heads(q), heads(k) * M_DIM ** -0.5, heads(v), fwd, bwd


def _bi_mlstm(lat, ctx, gate_b, need_ctx_out):
    ql, kl, vl, fl, bl = _mlstm_inputs(lat[0], lat[1], lat[2], lat[4], gate_b)
    qc, kc, vc, fc, bc = _mlstm_inputs(ctx[0], ctx[1], ctx[2], ctx[4], gate_b)
    B = ql.shape[0]
    zero = (jnp.zeros((B, M_HEADS, M_DIM, M_DIM), jnp.float32), jnp.zeros((B, M_HEADS, M_DIM), jnp.float32),
            jnp.zeros((B, M_HEADS), jnp.float32))
    rev = lambda t: jnp.flip(t, axis=2)
    hcf, st_f = _mlstm_scan(qc, kc, vc, fc[0], fc[1], zero)
    hcb, st_b = _mlstm_scan(rev(qc), rev(kc), rev(vc), rev(bc[0]), rev(bc[1]), zero)
    hlf, _ = _mlstm_scan(ql, kl, vl, fl[0], fl[1], st_f)
    hlb, _ = _mlstm_scan(rev(ql), rev(kl), rev(vl), rev(bl[0]), rev(bl[1]), st_b)

    def out(hf, hb, o_pre):
        Bb, H, T, d = hf.shape
        h = (hf + rev(hb)).transpose(0, 2, 1, 3).reshape(Bb, T, H * d)
        return (jax.nn.sigmoid(o_pre.astype(jnp.float32)) * h).astype(o_pre.dtype)

    y_lat = out(hlf, hlb, lat[3])
    y_ctx = out(hcf, hcb, ctx[3]) if need_ctx_out else None
    return y_lat, y_ctx


def _merge(h, ys, w_branch, w_gate, b_gate, w_out):
    g = jax.nn.sigmoid((h @ w_gate + b_gate).astype(jnp.float32)).astype(h.dtype)
    z = g[..., :D_MODEL] * (ys[0] @ w_branch[0])
    for i in range(1, N_BRANCH):
        z = z + g[..., i * D_MODEL:(i + 1) * D_MODEL] * (ys[i] @ w_branch[i])
    return z @ w_out


def _token_mixing(hl, hc, need_ctx, w_in, sink, rpb, conv_w, conv_b, conv_g, conv_bb, gate_b,
                  w_branch, w_gate, b_gate, w_out):
    pl = _split_proj(hl @ w_in)
    pc = _split_proj(hc @ w_in)
    kac, vac = _heads(pc[1], A_KV_HEADS), _heads(pc[2], A_KV_HEADS)
    ya = _windowed_gqa(_axial_rope(_heads(pl[0], A_HEADS)), _axial_rope(_heads(pl[1], A_KV_HEADS)),
                       _heads(pl[2], A_KV_HEADS), kac, vac, sink)
    kbc, vbc = _heads(pc[4], B_HEADS), _heads(pc[5], B_HEADS)
    yb = _neighborhood_attn(_heads(pl[3], B_HEADS), _heads(pl[4], B_HEADS), _heads(pl[5], B_HEADS), kbc, vbc, rpb)
    yc = _conformer_conv(pl[6], conv_w, conv_b, conv_g, conv_bb)
    yd, yd_c = _bi_mlstm(pl[7:12], pc[7:12], gate_b, need_ctx)
    out_lat = _merge(hl, (ya, yb, yc, yd), w_branch, w_gate, b_gate, w_out)
    out_ctx = None
    if need_ctx:
        ya_c = _ctx_attention(_heads(pc[0], A_HEADS), kac, vac, sink)
        yb_c = _ctx_attention(_heads(pc[3], B_HEADS), kbc, vbc)
        yc_c = _conformer_conv(pc[6], conv_w, conv_b, conv_g, conv_bb)
        out_ctx = _merge(hc, (ya_c, yb_c, yc_c, yd_c), w_branch, w_gate, b_gate, w_out)
    return out_lat, out_ctx


def _expert_choice_ffn(h, w_router, w1, w3, w2):
    B, T, D = h.shape
    cap = CAPACITY_FACTOR * T // N_EXPERTS
    aff = jax.nn.softmax((h @ w_router).astype(jnp.float32), axis=-1)
    vals, idx = lax.top_k(aff.transpose(0, 2, 1), cap)
    xg = jax.vmap(lambda hb, ib: hb[ib])(h, idx)
    hid = jax.nn.silu(jnp.einsum('becd,edf->becf', xg, w1)) * jnp.einsum('becd,edf->becf', xg, w3)
    y = jnp.einsum('becf,efd->becd', hid, w2) * vals[..., None].astype(h.dtype)
    return jax.vmap(lambda yb, ib: jnp.zeros((T, D), h.dtype).at[ib.reshape(-1)].add(yb.reshape(-1, D)))(y, idx)


def setup_inputs(seed: int = 0) -> dict:
    key = jax.random.key(seed)
    ks = jax.random.split(key, 40)
    f32 = jnp.float32
    D = D_MODEL
    beta = (8.0 * DEPTH) ** -0.25
    nrm = lambda k, shape, s: jax.random.normal(k, shape, f32) * s
    f_bias = jnp.linspace(3.0, 6.0, M_HEADS, dtype=f32)[None, :]
    mlstm_gate_b = jnp.concatenate([nrm(ks[10], (DEPTH, M_HEADS), 0.1), f_bias + nrm(ks[11], (DEPTH, M_HEADS), 0.1),
                                    nrm(ks[12], (DEPTH, M_HEADS), 0.1), f_bias + nrm(ks[13], (DEPTH, M_HEADS), 0.1)], axis=-1)
    return {
        'x': nrm(ks[0], (BATCH, SEQ, D), 1.0),
        'c': nrm(ks[1], (BATCH, D), 1.0),
        'ctx': nrm(ks[2], (BATCH, CTX_LEN, D), 1.0),
        'c_ctx': nrm(ks[3], (D,), 1.0),
        'w_mod': nrm(ks[4], (DEPTH, D, 6 * D), 0.5 * D ** -0.5),
        'b_mod': nrm(ks[5], (DEPTH, 6 * D), 0.02),
        'w_in': nrm(ks[6], (DEPTH, D, PROJ_W), D ** -0.5),
        'attn_sink': nrm(ks[7], (DEPTH, A_HEADS), 0.5),
        'na_rpb': nrm(ks[8], (DEPTH, B_HEADS, 2 * NA_ROWS - 1, 2 * NA_COLS - 1), 0.1),
        'conv_w': nrm(ks[9], (DEPTH, C_KSIZE, C_CH), C_KSIZE ** -0.5),
        'conv_b': nrm(ks[14], (DEPTH, C_CH), 0.02),
        'conv_ln_g': 1.0 + nrm(ks[15], (DEPTH, C_CH), 0.02),
        'conv_ln_b': nrm(ks[16], (DEPTH, C_CH), 0.02),
        'mlstm_gate_b': mlstm_gate_b,
        'w_branch': nrm(ks[17], (DEPTH, N_BRANCH, BRANCH_W, D), beta * BRANCH_W ** -0.5),
        'w_gate': nrm(ks[18], (DEPTH, D, N_BRANCH * D), D ** -0.5),
        'b_gate': nrm(ks[19], (DEPTH, N_BRANCH * D), 0.02),
        'w_out': nrm(ks[20], (DEPTH, D, D), beta * D ** -0.5),
        'ln1_g': 1.0 + nrm(ks[21], (DEPTH, D), 0.02),
        'ln1_b': nrm(ks[22], (DEPTH, D), 0.02),
        'w_router': nrm(ks[23], (DEPTH, D, N_EXPERTS), D ** -0.5),
        'w_exp_gate': nrm(ks[24], (DEPTH, N_EXPERTS, D, EXPERT_FF), D ** -0.5),
        'w_exp_up': nrm(ks[25], (DEPTH, N_EXPERTS, D, EXPERT_FF), D ** -0.5),
        'w_exp_down': nrm(ks[26], (DEPTH, N_EXPERTS, EXPERT_FF, D), beta * EXPERT_FF ** -0.5),
        'ln2_g': 1.0 + nrm(ks[27], (DEPTH, D), 0.02),
        'ln2_b': nrm(ks[28], (DEPTH, D), 0.02),
    }


def reference(x, c, ctx, c_ctx, w_mod, b_mod, w_in, attn_sink, na_rpb, conv_w, conv_b, conv_ln_g, conv_ln_b,
              mlstm_gate_b, w_branch, w_gate, b_gate, w_out, ln1_g, ln1_b, w_router, w_exp_gate, w_exp_up,
              w_exp_down, ln2_g, ln2_b):
    alpha = (2.0 * DEPTH) ** 0.25
    xl, xc = x, ctx
    s_lat = jax.nn.silu(c)
    s_ctx = jax.nn.silu(c_ctx)
    for l in range(DEPTH):
        need_ctx = l < DEPTH - 1
        mod_l = (s_lat @ w_mod[l] + b_mod[l])[:, None, :]
        mod_c = s_ctx @ w_mod[l] + b_mod[l]
        sh1, sc1, g1, sh2, sc2, g2 = jnp.split(mod_l, 6, axis=-1)
        csh1, csc1, cg1, csh2, csc2, cg2 = jnp.split(mod_c, 6, axis=-1)
        hl = _layer_norm(xl) * (1.0 + sc1) + sh1
        hc = _layer_norm(xc) * (1.0 + csc1) + csh1
        yl, yc = _token_mixing(hl, hc, need_ctx, w_in[l], attn_sink[l], na_rpb[l], conv_w[l], conv_b[l],
                               conv_ln_g[l], conv_ln_b[l], mlstm_gate_b[l], w_branch[l], w_gate[l], b_gate[l], w_out[l])
        xl = _layer_norm(alpha * xl + g1 * yl, ln1_g[l], ln1_b[l])
        if need_ctx:
            xc = _layer_norm(alpha * xc + cg1 * yc, ln1_g[l], ln1_b[l])
        hl = _layer_norm(xl) * (1.0 + sc2) + sh2
        ml = _expert_choice_ffn(hl, w_router[l], w_exp_gate[l], w_exp_up[l], w_exp_down[l])
        xl = _layer_norm(alpha * xl + g2 * ml, ln2_g[l], ln2_b[l])
        if need_ctx:
            hc = _layer_norm(xc) * (1.0 + csc2) + csh2
            mc = _expert_choice_ffn(hc, w_router[l], w_exp_gate[l], w_exp_up[l], w_exp_down[l])
            xc = _layer_norm(alpha * xc + cg2 * mc, ln2_g[l], ln2_b[l])
    return xl
```

```python
import functools
import math

import numpy as np
import jax
import jax.numpy as jnp
from jax import lax
from jax.experimental import pallas as pl
from jax.experimental.pallas import tpu as pltpu

F32 = jnp.float32
BF16 = jnp.bfloat16

D_MODEL = 1024
GRID_W = 64
HEAD_DIM = 64
A_HEADS = 8
A_KV_HEADS = 2
A_WINDOW = 128
B_HEADS = 8
NA_ROWS = 8
NA_COLS = 16
C_CH = 512
C_KSIZE = 31
M_HEADS = 4
M_DIM = 128
N_BRANCH = 4
BRANCH_W = 512
N_EXPERTS = 16
EXPERT_FF = 1024
CAPACITY_FACTOR = 2
ROPE_BASE = 10000.0
LN_EPS = 1e-6
NEG_INF = -1e30

BLK = 128
TM = 256
LANES = 128
PROJ_W = 5392
PROJ_PAD = 5504
OFF_A, OFF_AKV, OFF_B, OFF_C, OFF_D, OFF_G = 0, 512, 768, 2304, 3328, 5376
MIB = 1 << 20


def _cparams(sem, vmem_mib=None):
    kw = dict(dimension_semantics=sem)
    if vmem_mib is not None:
        kw["vmem_limit_bytes"] = vmem_mib * MIB
    return pltpu.CompilerParams(**kw)


def _ln(x):
    mu = jnp.mean(x, axis=-1, keepdims=True)
    xc = x - mu
    var = jnp.mean(xc * xc, axis=-1, keepdims=True)
    return xc * lax.rsqrt(var + LN_EPS)


def _dot(a, b):
    return jnp.dot(a, b, preferred_element_type=F32)


def _dot_nt(a, b):
    return lax.dot_general(a, b, (((1,), (1,)), ((), ())), preferred_element_type=F32)


def _split2(x):
    hi = x.astype(BF16)
    lo = (x - hi.astype(F32)).astype(BF16)
    return hi, lo


def _split3(x):
    hi = x.astype(BF16)
    r = x - hi.astype(F32)
    mid = r.astype(BF16)
    lo = (r - mid.astype(F32)).astype(BF16)
    return hi, mid, lo


def _dot3(x, w):
    xh, xl = _split2(x)
    wh, wl = _split2(w)
    return _dot(xh, wh) + (_dot(xh, wl) + _dot(xl, wh))


def _mod_kernel(c_ref, w_ref, b_ref, o_ref):
    c = c_ref[...]
    s = c * jax.nn.sigmoid(c)
    o_ref[0] = _dot3(s, w_ref[0]) + b_ref[0]


def _modulation(c8, w_mod, b_mod):
    depth, d, d6 = w_mod.shape
    nj = d6 // d
    return pl.pallas_call(
        _mod_kernel,
        grid=(depth, nj),
        in_specs=[
            pl.BlockSpec((8, d), lambda l, j: (0, 0)),
            pl.BlockSpec((1, d, d), lambda l, j: (l, 0, j)),
            pl.BlockSpec((1, 1, d), lambda l, j: (l, 0, j)),
        ],
        out_specs=pl.BlockSpec((1, 8, d), lambda l, j: (l, 0, j)),
        out_shape=jax.ShapeDtypeStruct((depth, 8, d6), F32),
        compiler_params=_cparams(("arbitrary", "arbitrary"), 40),
        name="modulation",
    )(c8, w_mod, b_mod.reshape(depth, 1, d6))


def _rope(x, cos, sa, sb):
    parts = []
    for j in range(x.shape[1] // LANES):
        xj = x[:, j * LANES:(j + 1) * LANES]
        parts.append(xj * cos + pltpu.roll(xj, LANES - 16, 1) * sa + pltpu.roll(xj, 16, 1) * sb)
    return parts[0] if len(parts) == 1 else jnp.concatenate(parts, axis=1)


def _inproj_kernel(x_ref, mod_ref, cos_ref, sa_ref, sb_ref, w_ref, gb_ref,
                   qa_o, kva_o, qb_o, kb_o, vb_o, yc_o, qd_o, kd_o, vd_o, so_o, g_o):
    d = D_MODEL
    x = x_ref[0]
    mod = mod_ref[0]
    h = (_ln(x) * (1.0 + mod[:, d:2 * d]) + mod[:, 0:d]).astype(BF16)
    cos, sa, sb = cos_ref[...], sa_ref[...], sb_ref[...]
    qscale = HEAD_DIM ** -0.5

    qa = _dot(h, w_ref[:, OFF_A:OFF_AKV])
    qa_o[0] = (_rope(qa, cos, sa, sb) * qscale).astype(BF16)
    kva = _dot(h, w_ref[:, OFF_AKV:OFF_B])
    kva_o[0, :, 0:LANES] = _rope(kva[:, 0:LANES], cos, sa, sb).astype(BF16)
    kva_o[0, :, LANES:2 * LANES] = kva[:, LANES:2 * LANES].astype(BF16)

    qb_o[0] = (_dot(h, w_ref[:, OFF_B:OFF_B + 512]) * qscale).astype(BF16)
    kb_o[0] = _dot(h, w_ref[:, OFF_B + 512:OFF_B + 1024]).astype(BF16)
    vb_o[0] = _dot(h, w_ref[:, OFF_B + 1024:OFF_C]).astype(BF16)

    ua = _dot(h, w_ref[:, OFF_C:OFF_C + C_CH])
    ug = _dot(h, w_ref[:, OFF_C + C_CH:OFF_D])
    yc_o[0] = ua * jax.nn.sigmoid(ug)

    qd_o[0] = _dot(h, w_ref[:, OFF_D:OFF_D + 512]).astype(BF16)
    kd_o[0] = (_dot(h, w_ref[:, OFF_D + 512:OFF_D + 1024]) * (M_DIM ** -0.5)).astype(BF16)
    vd_o[0] = _dot(h, w_ref[:, OFF_D + 1024:OFF_D + 1536]).astype(BF16)
    so_o[0] = jax.nn.sigmoid(_dot(h, w_ref[:, OFF_D + 1536:OFF_G])).astype(BF16)

    g = _dot(h, w_ref[:, OFF_G:PROJ_PAD])
    g_o[0] = g[:, 0:16] + gb_ref[...]


def _mod_index(lt, nb):
    return lambda b, t: (jnp.where(t < lt, nb, b), 0, 0)


def _inproj(x, mod3, tabs, w_bf, gate_b, lctx):
    nb, tt, d = x.shape
    nt = tt // TM
    lt = lctx // TM
    tok = lambda n: pl.BlockSpec((1, TM, n), lambda b, t: (b, t, 0))
    tab = pl.BlockSpec((TM, LANES), lambda b, t: (t, 0))
    sds = lambda n, dt: jax.ShapeDtypeStruct((nb, tt, n), dt)
    return pl.pallas_call(
        _inproj_kernel,
        grid=(nb, nt),
        in_specs=[
            tok(d),
            pl.BlockSpec((1, 1, 6 * d), _mod_index(lt, nb)),
            tab, tab, tab,
            pl.BlockSpec((d, PROJ_PAD), lambda b, t: (0, 0)),
            pl.BlockSpec((1, 16), lambda b, t: (0, 0)),
        ],
        out_specs=[tok(512), tok(256), tok(512), tok(512), tok(512), tok(512),
                   tok(512), tok(512), tok(512), tok(512), tok(16)],
        out_shape=[sds(512, BF16), sds(256, BF16), sds(512, BF16), sds(512, BF16), sds(512, BF16),
                   sds(512, F32), sds(512, BF16), sds(512, BF16), sds(512, BF16), sds(512, BF16),
                   sds(16, F32)],
        compiler_params=_cparams(("parallel", "parallel"), 48),
        name="inproj",
    )(x, mod3, tabs[0], tabs[1], tabs[2], w_bf, gate_b.reshape(1, 16))


def _softmax_pv(s, v, sink=None):
    m = jnp.max(s, axis=-1, keepdims=True)
    if sink is not None:
        m = jnp.maximum(m, sink)
    p = jnp.exp(s - m)
    l = jnp.sum(p, axis=-1, keepdims=True)
    if sink is not None:
        l = l + jnp.exp(sink - m)
    return _dot(p.astype(BF16), v) / l


def _attn_a_kernel(sink_ref, q_ref, kp_ref, kc_ref, kn_ref, kx_ref, mask_ref, o_ref):
    q = q_ref[0]
    kv = jnp.concatenate([kp_ref[0], kc_ref[0], kn_ref[0], kx_ref[0]], axis=0)
    mask = mask_ref[0]
    group = A_HEADS // A_KV_HEADS
    for g in range(A_KV_HEADS):
        k = kv[:, g * HEAD_DIM:(g + 1) * HEAD_DIM]
        v = kv[:, LANES + g * HEAD_DIM:LANES + (g + 1) * HEAD_DIM]
        for r in range(group):
            hh = g * group + r
            s = _dot_nt(q[:, hh * HEAD_DIM:(hh + 1) * HEAD_DIM], k) + mask
            o = _softmax_pv(s, v, sink_ref[hh])
            o_ref[0, :, hh * HEAD_DIM:(hh + 1) * HEAD_DIM] = o.astype(BF16)


def _attn_a_mask(lctx):
    i = np.arange(BLK)[:, None]
    j = np.arange(BLK)[None, :]
    ok_prev = (j >= i)
    ok_next = (j <= i)
    yes = np.ones((BLK, BLK), bool)
    no = np.zeros((BLK, BLK), bool)
    ctx = np.ones((BLK, lctx), bool)
    variants = [
        np.concatenate([ok_prev, yes, ok_next, ctx], 1),
        np.concatenate([no, yes, ok_next, ctx], 1),
        np.concatenate([ok_prev, yes, no, ctx], 1),
        np.concatenate([no, no, no, ctx], 1),
    ]
    return jnp.asarray(np.where(np.stack(variants), 0.0, NEG_INF).astype(np.float32))


def _attn_a(qa, kva, sink, lctx):
    nb, tt, _ = qa.shape
    nblk = tt // BLK
    lb = lctx // BLK
    assert nblk - lb >= 2
    mask = _attn_a_mask(lctx)

    def variant(t):
        return jnp.where(t < lb, 3, jnp.where(t == lb, 1, jnp.where(t == nblk - 1, 2, 0)))

    kvb = lambda f: pl.BlockSpec((1, BLK, 256), lambda b, t: (b, f(t), 0))
    return pl.pallas_call(
        _attn_a_kernel,
        grid=(nb, nblk),
        in_specs=[
            pl.BlockSpec(memory_space=pltpu.SMEM),
            pl.BlockSpec((1, BLK, 512), lambda b, t: (b, t, 0)),
            kvb(lambda t: jnp.maximum(t - 1, 0)),
            kvb(lambda t: t),
            kvb(lambda t: jnp.minimum(t + 1, nblk - 1)),
            pl.BlockSpec((1, lctx, 256), lambda b, t: (b, 0, 0)),
            pl.BlockSpec((1, BLK, 3 * BLK + lctx), lambda b, t: (variant(t), 0, 0)),
        ],
        out_specs=pl.BlockSpec((1, BLK, 512), lambda b, t: (b, t, 0)),
        out_shape=jax.ShapeDtypeStruct((nb, tt, 512), BF16),
        compiler_params=_cparams(("parallel", "parallel")),
        name="attn_window",
    )(sink, qa, kva, kva, kva, kva, mask)


NB_KBLK = 5


def _attn_b_kernel(q_ref, k0, k1, k2, k3, k4, kx, v0, v1, v2, v3, v4, vx, bias_ref, o_ref):
    q = q_ref[0]
    k = jnp.concatenate([k0[0], k1[0], k2[0], k3[0], k4[0], kx[0]], axis=0)
    v = jnp.concatenate([v0[0], v1[0], v2[0], v3[0], v4[0], vx[0]], axis=0)
    nloc = NB_KBLK * BLK
    for hh in range(B_HEADS):
        sl = slice(hh * HEAD_DIM, (hh + 1) * HEAD_DIM)
        s = _dot_nt(q[:, sl], k[:, sl])
        s = jnp.concatenate([s[:, :nloc] + bias_ref[0, hh], s[:, nloc:]], axis=1)
        o_ref[0, :, sl] = _softmax_pv(s, v[:, sl]).astype(BF16)


def _attn_b_bias(rpb, nlat):
    rows = 2 * nlat
    wh = min(NA_ROWS, rows)
    reps = [0, 1, 2, nlat - 2, nlat - 1]
    i = np.arange(BLK)
    kk = np.arange(NB_KBLK * BLK)
    tabs = []
    for dlt, j in enumerate(reps):
        base = j - dlt
        r = 2 * j + i // GRID_W
        qc = i % GRID_W
        kr = 2 * base + kk // GRID_W
        kc = kk % GRID_W
        start = np.clip(r - wh // 2, 0, rows - wh)
        row_ok = (kr[None, :] >= start[:, None]) & (kr[None, :] < start[:, None] + wh)
        c0 = np.clip(qc - NA_COLS // 2, 0, GRID_W - NA_COLS)
        col_ok = (kc[None, :] >= c0[:, None]) & (kc[None, :] < c0[:, None] + NA_COLS)
        dr = np.clip(kr[None, :] - r[:, None] + NA_ROWS - 1, 0, 2 * NA_ROWS - 2)
        dc = np.clip(kc[None, :] - qc[:, None] + NA_COLS - 1, 0, 2 * NA_COLS - 2)
        bias = rpb[:, dr, dc].astype(F32)
        tabs.append(jnp.where((row_ok & col_ok)[None], bias, NEG_INF))
    tabs.append(jnp.full_like(tabs[0], NEG_INF))
    return jnp.stack(tabs)


def _attn_b(qb, kb, vb, rpb, lctx):
    nb, tt, _ = qb.shape
    nblk = tt // BLK
    lb = lctx // BLK
    nlat = nblk - lb
    assert nlat >= NB_KBLK
    bias = _attn_b_bias(rpb, nlat)

    def base(t):
        return jnp.clip(t - lb - 2, 0, nlat - NB_KBLK) + lb

    def variant(t):
        return jnp.where(t < lb, 5, t - base(t))

    loc = lambda i: pl.BlockSpec((1, BLK, 512), lambda b, t: (b, base(t) + i, 0))
    ctx = pl.BlockSpec((1, lctx, 512), lambda b, t: (b, 0, 0))
    return pl.pallas_call(
        _attn_b_kernel,
        grid=(nb, nblk),
        in_specs=[pl.BlockSpec((1, BLK, 512), lambda b, t: (b, t, 0))]
        + [loc(i) for i in range(NB_KBLK)] + [ctx]
        + [loc(i) for i in range(NB_KBLK)] + [ctx]
        + [pl.BlockSpec((1, B_HEADS, BLK, NB_KBLK * BLK), lambda b, t: (variant(t), 0, 0, 0))],
        out_specs=pl.BlockSpec((1, BLK, 512), lambda b, t: (b, t, 0)),
        out_shape=jax.ShapeDtypeStruct((nb, tt, 512), BF16),
        compiler_params=_cparams(("parallel", "parallel")),
        name="attn_neighbourhood",
    )(qb, *([kb] * (NB_KBLK + 1)), *([vb] * (NB_KBLK + 1)), bias)


HALO = 16


def _conv_kernel(prev_ref, cur_ref, next_ref, w_ref, b_ref, g_ref, bb_ref, o_ref, ext_ref, *, lb, nblk):
    t = pl.program_id(1)
    has_prev = jnp.logical_and(t != 0, t != lb)
    has_next = jnp.logical_and(t != lb - 1, t != nblk - 1)
    ext_ref[0:HALO, :] = jnp.where(has_prev, prev_ref[0], 0.0)
    ext_ref[HALO:HALO + BLK, :] = cur_ref[0]
    ext_ref[HALO + BLK:2 * HALO + BLK, :] = jnp.where(has_next, next_ref[0], 0.0)
    pad = C_KSIZE // 2
    acc = jnp.zeros((BLK, C_CH), F32)
    for kk in range(C_KSIZE):
        acc = acc + ext_ref[pl.ds(HALO - pad + kk, BLK), :] * w_ref[kk:kk + 1, :]
    y = _ln(acc + b_ref[...]) * g_ref[...] + bb_ref[...]
    o_ref[0] = (y * jax.nn.sigmoid(y)).astype(BF16)


def _conv(yc, w, b, g, bb, lctx):
    nb, tt, ch = yc.shape
    nblk = tt // BLK
    lb = lctx // BLK
    per = BLK // HALO
    vec = pl.BlockSpec((1, ch), lambda b_, t: (0, 0))
    return pl.pallas_call(
        functools.partial(_conv_kernel, lb=lb, nblk=nblk),
        grid=(nb, nblk),
        in_specs=[
            pl.BlockSpec((1, HALO, ch), lambda b_, t: (b_, jnp.maximum(t * per - 1, 0), 0)),
            pl.BlockSpec((1, BLK, ch), lambda b_, t: (b_, t, 0)),
            pl.BlockSpec((1, HALO, ch), lambda b_, t: (b_, jnp.minimum((t + 1) * per, nblk * per - 1), 0)),
            pl.BlockSpec((C_KSIZE, ch), lambda b_, t: (0, 0)),
            vec, vec, vec,
        ],
        out_specs=pl.BlockSpec((1, BLK, ch), lambda b_, t: (b_, t, 0)),
        out_shape=jax.ShapeDtypeStruct((nb, tt, ch), BF16),
        scratch_shapes=[pltpu.VMEM((BLK + 2 * HALO, ch), F32)],
        compiler_params=_cparams(("parallel", "parallel")),
        name="conformer_conv",
    )(yc, yc, yc, w, b.reshape(1, ch), g.reshape(1, ch), bb.reshape(1, ch))


def _mlstm_kernel(q_ref, k_ref, v_ref, gc_ref, gr_ref, h_ref, c_st, n_st, m_st):
    dirn = pl.program_id(1)
    step = pl.program_id(2)

    @pl.when(step == 0)
    def _():
        c_st[...] = jnp.zeros_like(c_st)
        n_st[...] = jnp.zeros_like(n_st)
        m_st[...] = jnp.zeros_like(m_st)

    rr = lax.broadcasted_iota(jnp.int32, (BLK, BLK), 0)
    cc = lax.broadcasted_iota(jnp.int32, (BLK, BLK), 1)
    before = (rr - cc) * jnp.where(dirn == 0, 1, -1) >= 0
    bmat = jnp.where(before, 1.0, 0.0).astype(BF16)

    gc = gc_ref[0, 0]
    gr = gr_ref[0, 0]
    ig_c = gc[:, 0:M_HEADS]
    lf_c = jax.nn.log_sigmoid(gc[:, M_HEADS:2 * M_HEADS])
    ig_r = gr[0:M_HEADS, :]
    lf_r = jax.nn.log_sigmoid(gr[M_HEADS:2 * M_HEADS, :])
    fc = sum(_dot(bmat, part) for part in _split3(lf_c))
    fr = sum(_dot_nt(part, bmat) for part in _split3(lf_r))

    q_all, k_all, v_all = q_ref[0], k_ref[0], v_ref[0]
    for hh in range(M_HEADS):
        sl = slice(hh * M_DIM, (hh + 1) * M_DIM)
        q, k, v = q_all[:, sl], k_all[:, sl], v_all[:, sl]
        c_old = c_st[hh]
        n_old = n_st[hh]
        m_old = m_st[hh][:, 0:1]
        f_c = fc[:, hh:hh + 1]
        f_r = fr[hh:hh + 1, :]
        i_c = ig_c[:, hh:hh + 1]
        i_r = ig_r[hh:hh + 1, :]

        a = f_c + m_old
        logw = jnp.where(before, f_c - f_r + i_r, -jnp.inf)
        mt = jnp.maximum(a, jnp.max(logw, axis=-1, keepdims=True))
        w_inter = jnp.exp(a - mt)
        s = _dot_nt(q, k) * jnp.exp(logw - mt)
        qf = q.astype(F32)
        num = w_inter * _dot(q, c_old.astype(BF16)) + _dot(s.astype(BF16), v)
        den = w_inter * jnp.sum(qf * n_old, axis=-1, keepdims=True) + jnp.sum(s, axis=-1, keepdims=True)
        hout = num / jnp.maximum(jnp.abs(den), jnp.exp(-mt))
        h_ref[0, 0, :, sl] = hout.astype(BF16)

        f_tot = jnp.sum(lf_r[hh:hh + 1, :], axis=-1, keepdims=True)
        g_r = f_tot - f_r + i_r
        m_new = jnp.maximum(f_tot + m_old, jnp.max(g_r, axis=-1, keepdims=True))
        decay = jnp.exp(f_tot + m_old - m_new)
        kw = k.astype(F32) * jnp.exp(f_tot - f_c + i_c - m_new)
        c_st[hh] = decay * c_old + _dot(kw.T.astype(BF16), v)
        n_st[hh] = decay * n_old + jnp.sum(kw, axis=0, keepdims=True)
        m_st[hh] = jnp.broadcast_to(m_new, (1, LANES))


def _mlstm(qd, kd, vd, gates, lctx):
    nb, tt, _ = qd.shape
    nblk = tt // BLK
    lb = lctx // BLK
    gcol = jnp.stack([gates[..., 0:8], gates[..., 8:16]])
    grow = jnp.swapaxes(gcol, 2, 3)

    def blk(d, i):
        return jnp.where(d == 0, i, jnp.where(i < lb, lb - 1 - i, nblk - 1 + lb - i))

    tok = pl.BlockSpec((1, BLK, 512), lambda b, d, i: (b, blk(d, i), 0))
    return pl.pallas_call(
        _mlstm_kernel,
        grid=(nb, 2, nblk),
        in_specs=[
            tok, tok, tok,
            pl.BlockSpec((1, 1, BLK, 8), lambda b, d, i: (d, b, blk(d, i), 0)),
            pl.BlockSpec((1, 1, 8, BLK), lambda b, d, i: (d, b, 0, blk(d, i))),
        ],
        out_specs=pl.BlockSpec((1, 1, BLK, 512), lambda b, d, i: (d, b, blk(d, i), 0)),
        out_shape=jax.ShapeDtypeStruct((2, nb, tt, 512), BF16),
        scratch_shapes=[pltpu.VMEM((M_HEADS, M_DIM, M_DIM), F32),
                        pltpu.VMEM((M_HEADS, 1, M_DIM), F32),
                        pltpu.VMEM((M_HEADS, 1, LANES), F32)],
        compiler_params=_cparams(("parallel", "parallel", "arbitrary")),
        name="mlstm",
    )(qd, kd, vd, gcol, grow)


def _merge_kernel(x_ref, mod_ref, ya_ref, yb_ref, yc_ref, hf_ref, hb_ref, so_ref,
                  wg_ref, bg_ref, wbr_ref, wo_ref, g1_ref, b1_ref, wr_ref,
                  x1_o, hp_o, aff_o, *, alpha):
    d = D_MODEL
    x = x_ref[0]
    mod = mod_ref[0]
    h = (_ln(x) * (1.0 + mod[:, d:2 * d]) + mod[:, 0:d]).astype(BF16)
    yd = (so_ref[0].astype(F32) * (hf_ref[0, 0].astype(F32) + hb_ref[0, 0].astype(F32))).astype(BF16)
    ys = (ya_ref[0], yb_ref[0], yc_ref[0], yd)
    z = None
    for i in range(N_BRANCH):
        gate = jax.nn.sigmoid(_dot(h, wg_ref[:, i * d:(i + 1) * d]) + bg_ref[:, i * d:(i + 1) * d])
        term = gate * _dot(ys[i], wbr_ref[i])
        z = term if z is None else z + term
    y = _dot(z.astype(BF16), wo_ref[...])
    x1 = _ln(alpha * x + mod[:, 2 * d:3 * d] * y) * g1_ref[...] + b1_ref[...]
    x1_o[0] = x1

    h2 = _ln(x1) * (1.0 + mod[:, 4 * d:5 * d]) + mod[:, 3 * d:4 * d]
    hb16 = h2.astype(BF16)
    bits = pltpu.bitcast(hb16.astype(F32), jnp.uint32)
    hp_o[0] = (bits[:, d // 2:] & jnp.uint32(0xFFFF0000)) | (bits[:, :d // 2] >> 16)

    logits = _dot3(h2, wr_ref[...])[:, 0:N_EXPERTS]
    e = jnp.exp(logits - jnp.max(logits, axis=-1, keepdims=True))
    aff_o[0] = e / jnp.sum(e, axis=-1, keepdims=True)


def _merge(x, mod3, ya, yb, yc, hfb, so, wg, bg, wbr, wo, g1, b1, wr, lctx, alpha):
    nb, tt, d = x.shape
    nt = tt // TM
    lt = lctx // TM
    tok = lambda n: pl.BlockSpec((1, TM, n), lambda b, t: (b, t, 0))
    const = lambda shape: pl.BlockSpec(shape, lambda b, t: (0,) * len(shape))
    return pl.pallas_call(
        functools.partial(_merge_kernel, alpha=alpha),
        grid=(nb, nt),
        in_specs=[
            tok(d),
            pl.BlockSpec((1, 1, 6 * d), _mod_index(lt, nb)),
            tok(512), tok(512), tok(512),
            pl.BlockSpec((1, 1, TM, 512), lambda b, t: (0, b, t, 0)),
            pl.BlockSpec((1, 1, TM, 512), lambda b, t: (1, b, t, 0)),
            tok(512),
            const((d, N_BRANCH * d)), const((1, N_BRANCH * d)), const((N_BRANCH, BRANCH_W, d)),
            const((d, d)), const((1, d)), const((1, d)), const((d, LANES)),
        ],
        out_specs=[tok(d), tok(d // 2), tok(N_EXPERTS)],
        out_shape=[jax.ShapeDtypeStruct((nb, tt, d), F32),
                   jax.ShapeDtypeStruct((nb, tt, d // 2), jnp.uint32),
                   jax.ShapeDtypeStruct((nb, tt, N_EXPERTS), F32)],
        compiler_params=_cparams(("parallel", "parallel"), 56),
        name="merge",
    )(x, mod3, ya, yb, yc, hfb, hfb, so, wg, bg.reshape(1, -1), wbr, wo,
      g1.reshape(1, d), b1.reshape(1, d), wr)


def _route_one(a_ref, cpos_ref, idx_o, val_o, *, ntok, cap, tok_off, slot_off, sblk):
    a = a_ref[0]
    bits = pltpu.bitcast(a, jnp.int32)
    thr = jnp.zeros((N_EXPERTS, 1), jnp.int32)
    for bit in range(30, -1, -1):
        cand = thr | jnp.int32(1 << bit)
        cnt = jnp.sum(jnp.where(bits >= cand, 1.0, 0.0), axis=-1, keepdims=True)
        thr = jnp.where(cnt >= cap, cand, thr)
    gt = bits > thr
    eq = bits == thr
    need = cap - jnp.sum(jnp.where(gt, 1.0, 0.0), axis=-1, keepdims=True)

    rr = lax.broadcasted_iota(jnp.int32, (LANES, LANES), 0)
    cc = lax.broadcasted_iota(jnp.int32, (LANES, LANES), 1)
    upper = jnp.where(rr <= cc, 1.0, 0.0).astype(BF16)

    def cumsum_blocks(mask_f):
        run = jnp.zeros((N_EXPERTS, 1), F32)
        out = []
        for c in range(ntok // LANES):
            blk = mask_f[:, c * LANES:(c + 1) * LANES]
            out.append(_dot(blk.astype(BF16), upper) + run)
            run = run + jnp.sum(blk, axis=-1, keepdims=True)
        return out

    eq_f = jnp.where(eq, 1.0, 0.0)
    cum_eq = cumsum_blocks(eq_f)
    sel_parts = []
    for c in range(ntok // LANES):
        sl = slice(c * LANES, (c + 1) * LANES)
        sel_parts.append(jnp.where(gt[:, sl] | (eq[:, sl] & (cum_eq[c] <= need)), 1.0, 0.0))
    sel_f = jnp.concatenate(sel_parts, axis=1)
    cpos = cumsum_blocks(sel_f)
    for c in range(ntok // LANES):
        cpos_ref[:, c * LANES:(c + 1) * LANES] = cpos[c] * sel_parts[c]

    lane = lax.broadcasted_iota(jnp.int32, (1, LANES), 1).astype(F32)
    slot1 = lax.broadcasted_iota(jnp.int32, (sblk, 1), 0).astype(F32) + 1.0

    for e in range(N_EXPERTS):
        def per_slot_block(sb, carry, e=e):
            want = slot1 + (sb * sblk).astype(F32)
            acc_i = jnp.zeros((sblk, LANES), F32)
            acc_v = jnp.zeros((sblk, LANES), F32)
            for c in range(ntok // LANES):
                sl = slice(c * LANES, (c + 1) * LANES)
                hit = cpos_ref[e:e + 1, sl] == want
                acc_i = acc_i + jnp.where(hit, lane + float(c * LANES + tok_off), 0.0)
                acc_v = acc_v + jnp.where(hit, a_ref[0, e:e + 1, sl], 0.0)
            row0 = pl.multiple_of(slot_off + sb * sblk, 8)
            idx_o[0, e, pl.ds(row0, sblk), :] = jnp.sum(acc_i, axis=-1, keepdims=True).astype(jnp.int32)
            val_o[0, e, pl.ds(row0, sblk), :] = jnp.sum(acc_v, axis=-1, keepdims=True)
            return carry
        lax.fori_loop(0, cap // sblk, per_slot_block, 0)


def _route_kernel(al_ref, ac_ref, idx_o, val_o, cpl_ref, cpc_ref, *, s, lctx, cap_l, cap_c):
    _route_one(al_ref, cpl_ref, idx_o, val_o, ntok=s, cap=cap_l, tok_off=lctx, slot_off=0, sblk=min(64, cap_l))
    _route_one(ac_ref, cpc_ref, idx_o, val_o, ntok=lctx, cap=cap_c, tok_off=0, slot_off=cap_l, sblk=min(64, cap_c))


def _route(aff, lctx):
    nb, tt, ne = aff.shape
    s = tt - lctx
    cap_l = CAPACITY_FACTOR * s // ne
    cap_c = CAPACITY_FACTOR * lctx // ne
    capt = cap_l + cap_c
    aff_t = jnp.swapaxes(aff, 1, 2)
    out = lambda dt: jax.ShapeDtypeStruct((nb, ne, capt, 1), dt)
    return pl.pallas_call(
        functools.partial(_route_kernel, s=s, lctx=lctx, cap_l=cap_l, cap_c=cap_c),
        grid=(nb,),
        in_specs=[pl.BlockSpec((1, ne, s), lambda b: (b, 0, 0)),
                  pl.BlockSpec((1, ne, lctx), lambda b: (b, 0, 0))],
        out_specs=[pl.BlockSpec((1, ne, capt, 1), lambda b: (b, 0, 0, 0))] * 2,
        out_shape=[out(jnp.int32), out(F32)],
        scratch_shapes=[pltpu.VMEM((ne, s), F32), pltpu.VMEM((ne, lctx), F32)],
        compiler_params=_cparams(("parallel",), 40),
        name="route",
    )(aff_t[:, :, lctx:], aff_t[:, :, :lctx])


def _gather_kernel(idx_ref, x_ref, o_ref, *, capt):
    def body(c, carry):
        o_ref[0, 0, pl.ds(c, 1), :] = x_ref[0, pl.ds(idx_ref[0, 0, c], 1), :]
        return carry
    lax.fori_loop(0, capt, body, 0, unroll=8)


def _gather(idx_s, hp, capt):
    nb, tt, w = hp.shape
    ne = N_EXPERTS
    return pl.pallas_call(
        functools.partial(_gather_kernel, capt=capt),
        grid=(nb, ne),
        in_specs=[pl.BlockSpec((1, 1, capt), lambda b, e: (b * ne + e, 0, 0), memory_space=pltpu.SMEM),
                  pl.BlockSpec((1, tt, w), lambda b, e: (b, 0, 0))],
        out_specs=pl.BlockSpec((1, 1, capt, w), lambda b, e: (b, e, 0, 0)),
        out_shape=jax.ShapeDtypeStruct((nb, ne, capt, w), jnp.uint32),
        compiler_params=_cparams(("parallel", "arbitrary"), 40),
        name="moe_gather",
    )(idx_s, hp)


def _ffn_kernel(x_ref, w1_ref, w3_ref, w2_ref, val_ref, y_ref, w1b, w3b, w2b):
    @pl.when(pl.program_id(1) == 0)
    def _():
        w1b[...] = w1_ref[0].astype(BF16)
        w3b[...] = w3_ref[0].astype(BF16)
        w2b[...] = w2_ref[0].astype(BF16)

    packed = x_ref[0, 0]
    lo = pltpu.bitcast(packed << 16, F32)
    hi = pltpu.bitcast(packed & jnp.uint32(0xFFFF0000), F32)
    xg = jnp.concatenate([lo, hi], axis=1).astype(BF16)
    a = _dot(xg, w1b[...])
    hid = (a * jax.nn.sigmoid(a) * _dot(xg, w3b[...])).astype(BF16)
    y_ref[0, 0] = _dot(hid, w2b[...]) * val_ref[0, 0]


def _ffn(xg, w1, w3, w2, vals):
    nb, ne, capt, w = xg.shape
    d, ff = w1.shape[1], w1.shape[2]
    return pl.pallas_call(
        _ffn_kernel,
        grid=(ne, nb),
        in_specs=[pl.BlockSpec((1, 1, capt, w), lambda e, b: (b, e, 0, 0)),
                  pl.BlockSpec((1, d, ff), lambda e, b: (e, 0, 0)),
                  pl.BlockSpec((1, d, ff), lambda e, b: (e, 0, 0)),
                  pl.BlockSpec((1, ff, d), lambda e, b: (e, 0, 0)),
                  pl.BlockSpec((1, 1, capt, 1), lambda e, b: (b, e, 0, 0))],
        out_specs=pl.BlockSpec((1, 1, capt, d), lambda e, b: (b, e, 0, 0)),
        out_shape=jax.ShapeDtypeStruct((nb, ne, capt, d), F32),
        scratch_shapes=[pltpu.VMEM((d, ff), BF16), pltpu.VMEM((d, ff), BF16), pltpu.VMEM((ff, d), BF16)],
        compiler_params=_cparams(("parallel", "arbitrary"), 56),
        name="moe_ffn",
    )(xg, w1, w3, w2, vals)


def _scatter_kernel(idx_ref, y_ref, o_ref, *, capt):
    @pl.when(pl.program_id(1) == 0)
    def _():
        o_ref[...] = jnp.zeros_like(o_ref)

    def body(c, carry):
        row = pl.ds(idx_ref[0, 0, c], 1)
        o_ref[0, row, :] = o_ref[0, row, :] + y_ref[0, 0, pl.ds(c, 1), :]
        return carry
    lax.fori_loop(0, capt, body, 0, unroll=4)


def _scatter(idx_s, y, tt):
    nb, ne, capt, d = y.shape
    return pl.pallas_call(
        functools.partial(_scatter_kernel, capt=capt),
        grid=(nb, ne),
        in_specs=[pl.BlockSpec((1, 1, capt), lambda b, e: (b * ne + e, 0, 0), memory_space=pltpu.SMEM),
                  pl.BlockSpec((1, 1, capt, d), lambda b, e: (b, e, 0, 0))],
        out_specs=pl.BlockSpec((1, tt, d), lambda b, e: (b, 0, 0)),
        out_shape=jax.ShapeDtypeStruct((nb, tt, d), F32),
        compiler_params=_cparams(("parallel", "arbitrary"), 56),
        name="moe_scatter",
    )(idx_s, y)


def _post_kernel(x_ref, ml_ref, mod_ref, g_ref, b_ref, o_ref, *, alpha):
    d = D_MODEL
    g2 = mod_ref[0][:, 5 * d:6 * d]
    o_ref[0] = _ln(alpha * x_ref[0] + g2 * ml_ref[0]) * g_ref[...] + b_ref[...]


def _post(x1, ml, mod3, g, b, lctx, alpha):
    nb, tt, d = x1.shape
    lt = lctx // TM
    tok = pl.BlockSpec((1, TM, d), lambda b_, t: (b_, t, 0))
    vec = pl.BlockSpec((1, d), lambda b_, t: (0, 0))
    return pl.pallas_call(
        functools.partial(_post_kernel, alpha=alpha),
        grid=(nb, tt // TM),
        in_specs=[tok, tok, pl.BlockSpec((1, 1, 6 * d), _mod_index(lt, nb)), vec, vec],
        out_specs=tok,
        out_shape=jax.ShapeDtypeStruct((nb, tt, d), F32),
        compiler_params=_cparams(("parallel", "parallel")),
        name="moe_post",
    )(x1, ml, mod3, g.reshape(1, d), b.reshape(1, d))


def _rope_tables(s, lctx):
    half = HEAD_DIM // 2
    nf = half // 2
    inv = ROPE_BASE ** (-jnp.arange(nf, dtype=F32) / nf)
    t = jnp.arange(s)
    lane = np.arange(LANES)
    dd = lane % HEAD_DIM
    use_col = jnp.asarray(dd >= half)[None, :]
    pos = jnp.where(use_col, (t % GRID_W)[:, None], (t // GRID_W)[:, None]).astype(F32)
    ang = pos * inv[jnp.asarray(dd % nf)][None, :]
    cos, sin = jnp.cos(ang), jnp.sin(ang)
    first = jnp.asarray((dd % half) < nf)[None, :]
    sa = jnp.where(first, -sin, 0.0)
    sb = jnp.where(first, 0.0, sin)
    pad = lambda a, v: jnp.concatenate([jnp.full((lctx, LANES), v, F32), a], axis=0)
    return pad(cos, 1.0), pad(sa, 0.0), pad(sb, 0.0)


def kernel(x, c, ctx, c_ctx, w_mod, b_mod, w_in, attn_sink, na_rpb, conv_w, conv_b, conv_ln_g, conv_ln_b,
           mlstm_gate_b, w_branch, w_gate, b_gate, w_out, ln1_g, ln1_b, w_router, w_exp_gate, w_exp_up,
           w_exp_down, ln2_g, ln2_b):
    nb, s, d = x.shape
    lctx = ctx.shape[1]
    depth = w_mod.shape[0]
    assert d == D_MODEL and nb + 1 <= 8 and lctx % TM == 0 and s % TM == 0 and s % GRID_W == 0
    alpha = (2.0 * depth) ** 0.25
    tt = lctx + s
    cap_t = CAPACITY_FACTOR * s // N_EXPERTS + CAPACITY_FACTOR * lctx // N_EXPERTS

    c8 = jnp.concatenate([c, c_ctx[None, :], jnp.zeros((8 - nb - 1, d), F32)], axis=0)
    mod_all = _modulation(c8, w_mod, b_mod)
    tabs = _rope_tables(s, lctx)
    xs = jnp.concatenate([ctx, x], axis=1)

    for l in range(depth):
        mod3 = mod_all[l].reshape(8, 1, 6 * d)
        w_bf = jnp.pad(w_in[l], ((0, 0), (0, PROJ_PAD - PROJ_W))).astype(BF16)
        (qa, kva, qb, kb, vb, yc0, qd, kd, vd, so, gates) = _inproj(xs, mod3, tabs, w_bf, mlstm_gate_b[l], lctx)
        ya = _attn_a(qa, kva, attn_sink[l], lctx)
        yb = _attn_b(qb, kb, vb, na_rpb[l], lctx)
        yc = _conv(yc0, conv_w[l], conv_b[l], conv_ln_g[l], conv_ln_b[l], lctx)
        hfb = _mlstm(qd, kd, vd, gates, lctx)
        wr = jnp.pad(w_router[l], ((0, 0), (0, LANES - N_EXPERTS)))
        x1, hp, aff = _merge(xs, mod3, ya, yb, yc, hfb, so, w_gate[l].astype(BF16), b_gate[l],
                             w_branch[l].astype(BF16), w_out[l].astype(BF16), ln1_g[l], ln1_b[l], wr,
                             lctx, alpha)
        idx, vals = _route(aff, lctx)
        idx_s = idx.reshape(nb * N_EXPERTS, 1, cap_t)
        xg = _gather(idx_s, hp, cap_t)
        y = _ffn(xg, w_exp_gate[l], w_exp_up[l], w_exp_down[l], vals)
        ml = _scatter(idx_s, y, tt)
        xs = _post(x1, ml, mod3, ln2_g[l], ln2_b[l], lctx, alpha)
    return xs[:, lctx:, :]
```

```python
import functools
import math

import numpy as np
import jax
import jax.numpy as jnp
from jax import lax
from jax.experimental import pallas as pl
from jax.experimental.pallas import tpu as pltpu

F32 = jnp.float32
BF16 = jnp.bfloat16

D_MODEL = 1024
GRID_W = 64
HEAD_DIM = 64
A_HEADS = 8
A_KV_HEADS = 2
A_WINDOW = 128
B_HEADS = 8
NA_ROWS = 8
NA_COLS = 16
C_CH = 512
C_KSIZE = 31
M_HEADS = 4
M_DIM = 128
N_BRANCH = 4
BRANCH_W = 512
N_EXPERTS = 16
EXPERT_FF = 1024
CAPACITY_FACTOR = 2
ROPE_BASE = 10000.0
LN_EPS = 1e-6
NEG_INF = -1e30

BLK = 128
TM = 256
LANES = 128
PROJ_W = 5392
PROJ_PAD = 5504
OFF_A, OFF_AKV, OFF_B, OFF_C, OFF_D, OFF_G = 0, 512, 768, 2304, 3328, 5376
MIB = 1 << 20


def _cparams(sem, vmem_mib=None):
    kw = dict(dimension_semantics=sem)
    if vmem_mib is not None:
        kw["vmem_limit_bytes"] = vmem_mib * MIB
    return pltpu.CompilerParams(**kw)


def _ln(x):
    mu = jnp.mean(x, axis=-1, keepdims=True)
    xc = x - mu
    var = jnp.mean(xc * xc, axis=-1, keepdims=True)
    return xc * lax.rsqrt(var + LN_EPS)


def _dot(a, b):
    return jnp.dot(a, b, preferred_element_type=F32)


def _dot_nt(a, b):
    return lax.dot_general(a, b, (((1,), (1,)), ((), ())), preferred_element_type=F32)


def _split2(x):
    hi = x.astype(BF16)
    lo = (x - hi.astype(F32)).astype(BF16)
    return hi, lo


def _split3(x):
    hi = x.astype(BF16)
    r = x - hi.astype(F32)
    mid = r.astype(BF16)
    lo = (r - mid.astype(F32)).astype(BF16)
    return hi, mid, lo


def _dot3(x, w):
    xh, xl = _split2(x)
    wh, wl = _split2(w)
    return _dot(xh, wh) + (_dot(xh, wl) + _dot(xl, wh))


def _mod_kernel(c_ref, w_ref, b_ref, o_ref):
    c = c_ref[...]
    s = c * jax.nn.sigmoid(c)
    o_ref[0] = _dot3(s, w_ref[0]) + b_ref[0]


def _modulation(c8, w_mod, b_mod):
    depth, d, d6 = w_mod.shape
    nj = d6 // d
    return pl.pallas_call(
        _mod_kernel,
        grid=(depth, nj),
        in_specs=[
            pl.BlockSpec((8, d), lambda l, j: (0, 0)),
            pl.BlockSpec((1, d, d), lambda l, j: (l, 0, j)),
            pl.BlockSpec((1, 1, d), lambda l, j: (l, 0, j)),
        ],
        out_specs=pl.BlockSpec((1, 8, d), lambda l, j: (l, 0, j)),
        out_shape=jax.ShapeDtypeStruct((depth, 8, d6), F32),
        compiler_params=_cparams(("arbitrary", "arbitrary"), 40),
        name="modulation",
    )(c8, w_mod, b_mod.reshape(depth, 1, d6))


def _rope(x, cos, sa, sb):
    parts = []
    for j in range(x.shape[1] // LANES):
        xj = x[:, j * LANES:(j + 1) * LANES]
        parts.append(xj * cos + pltpu.roll(xj, LANES - 16, 1) * sa + pltpu.roll(xj, 16, 1) * sb)
    return parts[0] if len(parts) == 1 else jnp.concatenate(parts, axis=1)


def _inproj_kernel(x_ref, mod_ref, cos_ref, sa_ref, sb_ref, w_ref, gb_ref,
                   qa_o, kva_o, qb_o, kb_o, vb_o, yc_o, qd_o, kd_o, vd_o, so_o, g_o):
    d = D_MODEL
    x = x_ref[0]
    mod = mod_ref[0]
    h = (_ln(x) * (1.0 + mod[:, d:2 * d]) + mod[:, 0:d]).astype(BF16)
    cos, sa, sb = cos_ref[...], sa_ref[...], sb_ref[...]
    qscale = HEAD_DIM ** -0.5

    qa = _dot(h, w_ref[:, OFF_A:OFF_AKV])
    qa_o[0] = (_rope(qa, cos, sa, sb) * qscale).astype(BF16)
    kva = _dot(h, w_ref[:, OFF_AKV:OFF_B])
    kva_o[0, :, 0:LANES] = _rope(kva[:, 0:LANES], cos, sa, sb).astype(BF16)
    kva_o[0, :, LANES:2 * LANES] = kva[:, LANES:2 * LANES].astype(BF16)

    qb_o[0] = (_dot(h, w_ref[:, OFF_B:OFF_B + 512]) * qscale).astype(BF16)
    kb_o[0] = _dot(h, w_ref[:, OFF_B + 512:OFF_B + 1024]).astype(BF16)
    vb_o[0] = _dot(h, w_ref[:, OFF_B + 1024:OFF_C]).astype(BF16)

    ua = _dot(h, w_ref[:, OFF_C:OFF_C + C_CH])
    ug = _dot(h, w_ref[:, OFF_C + C_CH:OFF_D])
    yc_o[0] = ua * jax.nn.sigmoid(ug)

    qd_o[0] = _dot(h, w_ref[:, OFF_D:OFF_D + 512]).astype(BF16)
    kd_o[0] = (_dot(h, w_ref[:, OFF_D + 512:OFF_D + 1024]) * (M_DIM ** -0.5)).astype(BF16)
    vd_o[0] = _dot(h, w_ref[:, OFF_D + 1024:OFF_D + 1536]).astype(BF16)
    so_o[0] = jax.nn.sigmoid(_dot(h, w_ref[:, OFF_D + 1536:OFF_G])).astype(BF16)

    g = _dot(h, w_ref[:, OFF_G:PROJ_PAD])
    g_o[0] = g[:, 0:16] + gb_ref[...]


def _mod_index(lt, nb):
    return lambda b, t: (jnp.where(t < lt, nb, b), 0, 0)


def _inproj(x, mod3, tabs, w_bf, gate_b, lctx):
    nb, tt, d = x.shape
    nt = tt // TM
    lt = lctx // TM
    tok = lambda n: pl.BlockSpec((1, TM, n), lambda b, t: (b, t, 0))
    tab = pl.BlockSpec((TM, LANES), lambda b, t: (t, 0))
    sds = lambda n, dt: jax.ShapeDtypeStruct((nb, tt, n), dt)
    return pl.pallas_call(
        _inproj_kernel,
        grid=(nb, nt),
        in_specs=[
            tok(d),
            pl.BlockSpec((1, 1, 6 * d), _mod_index(lt, nb)),
            tab, tab, tab,
            pl.BlockSpec((d, PROJ_PAD), lambda b, t: (0, 0)),
            pl.BlockSpec((1, 16), lambda b, t: (0, 0)),
        ],
        out_specs=[tok(512), tok(256), tok(512), tok(512), tok(512), tok(512),
                   tok(512), tok(512), tok(512), tok(512), tok(16)],
        out_shape=[sds(512, BF16), sds(256, BF16), sds(512, BF16), sds(512, BF16), sds(512, BF16),
                   sds(512, F32), sds(512, BF16), sds(512, BF16), sds(512, BF16), sds(512, BF16),
                   sds(16, F32)],
        compiler_params=_cparams(("parallel", "parallel"), 48),
        name="inproj",
    )(x, mod3, tabs[0], tabs[1], tabs[2], w_bf, gate_b.reshape(1, 16))


def _softmax_pv(s, v):
    p = jnp.exp(s - jnp.max(s, axis=-1, keepdims=True))
    return _dot(p.astype(BF16), v) / jnp.sum(p, axis=-1, keepdims=True)


def _attn_a_kernel(sink_ref, q_ref, kp_ref, kc_ref, kn_ref, kx_ref, mask_ref, o_ref):
    q = q_ref[0]
    kv = jnp.concatenate([kp_ref[0], kc_ref[0], kn_ref[0], kx_ref[0]], axis=0)
    mask = mask_ref[0]
    group = A_HEADS // A_KV_HEADS
    for g in range(A_KV_HEADS):
        k = kv[:, g * HEAD_DIM:(g + 1) * HEAD_DIM]
        v = kv[:, LANES + g * HEAD_DIM:LANES + (g + 1) * HEAD_DIM]
        heads = range(g * group, (g + 1) * group)
        qs = jnp.concatenate([q[:, hh * HEAD_DIM:(hh + 1) * HEAD_DIM] for hh in heads], axis=0)
        s = _dot_nt(qs, k)
        ps, ls = [], []
        for r, hh in enumerate(heads):
            sr = s[r * BLK:(r + 1) * BLK] + mask
            sink = sink_ref[hh]
            m = jnp.maximum(jnp.max(sr, axis=-1, keepdims=True), sink)
            p = jnp.exp(sr - m)
            ls.append(jnp.sum(p, axis=-1, keepdims=True) + jnp.exp(sink - m))
            ps.append(p.astype(BF16))
        o = _dot(jnp.concatenate(ps, axis=0), v)
        for r, hh in enumerate(heads):
            o_ref[0, :, hh * HEAD_DIM:(hh + 1) * HEAD_DIM] = (o[r * BLK:(r + 1) * BLK] / ls[r]).astype(BF16)


def _attn_a_mask(lctx):
    i = np.arange(BLK)[:, None]
    j = np.arange(BLK)[None, :]
    ok_prev = (j >= i)
    ok_next = (j <= i)
    yes = np.ones((BLK, BLK), bool)
    no = np.zeros((BLK, BLK), bool)
    ctx = np.ones((BLK, lctx), bool)
    variants = [
        np.concatenate([ok_prev, yes, ok_next, ctx], 1),
        np.concatenate([no, yes, ok_next, ctx], 1),
        np.concatenate([ok_prev, yes, no, ctx], 1),
        np.concatenate([no, no, no, ctx], 1),
    ]
    return jnp.asarray(np.where(np.stack(variants), 0.0, NEG_INF).astype(np.float32))


def _attn_a(qa, kva, sink, lctx):
    nb, tt, _ = qa.shape
    nblk = tt // BLK
    lb = lctx // BLK
    assert nblk - lb >= 2
    mask = _attn_a_mask(lctx)

    def variant(t):
        return jnp.where(t < lb, 3, jnp.where(t == lb, 1, jnp.where(t == nblk - 1, 2, 0)))

    kvb = lambda f: pl.BlockSpec((1, BLK, 256), lambda b, t: (b, f(t), 0))
    return pl.pallas_call(
        _attn_a_kernel,
        grid=(nb, nblk),
        in_specs=[
            pl.BlockSpec(memory_space=pltpu.SMEM),
            pl.BlockSpec((1, BLK, 512), lambda b, t: (b, t, 0)),
            kvb(lambda t: jnp.maximum(t - 1, 0)),
            kvb(lambda t: t),
            kvb(lambda t: jnp.minimum(t + 1, nblk - 1)),
            pl.BlockSpec((1, lctx, 256), lambda b, t: (b, 0, 0)),
            pl.BlockSpec((1, BLK, 3 * BLK + lctx), lambda b, t: (variant(t), 0, 0)),
        ],
        out_specs=pl.BlockSpec((1, BLK, 512), lambda b, t: (b, t, 0)),
        out_shape=jax.ShapeDtypeStruct((nb, tt, 512), BF16),
        compiler_params=_cparams(("parallel", "parallel")),
        name="attn_window",
    )(sink, qa, kva, kva, kva, kva, mask)


NB_KBLK = 5


def _attn_b_kernel(q_ref, k0, k1, k2, k3, k4, kx, v0, v1, v2, v3, v4, vx, bias_ref, o_ref):
    q = q_ref[0]
    k = jnp.concatenate([k0[0], k1[0], k2[0], k3[0], k4[0], kx[0]], axis=0)
    v = jnp.concatenate([v0[0], v1[0], v2[0], v3[0], v4[0], vx[0]], axis=0)
    nloc = NB_KBLK * BLK
    for hh in range(B_HEADS):
        sl = slice(hh * HEAD_DIM, (hh + 1) * HEAD_DIM)
        s = _dot_nt(q[:, sl], k[:, sl])
        s = jnp.concatenate([s[:, :nloc] + bias_ref[0, hh], s[:, nloc:]], axis=1)
        o_ref[0, :, sl] = _softmax_pv(s, v[:, sl]).astype(BF16)


def _attn_b_bias(rpb, nlat):
    rows = 2 * nlat
    wh = min(NA_ROWS, rows)
    n = GRID_W
    qrows, krows = BLK // n, NB_KBLK * BLK // n
    edge = n - NA_COLS
    g = jnp.concatenate([jnp.repeat(rpb[..., :1], edge, axis=-1), rpb.astype(F32),
                         jnp.repeat(rpb[..., -1:], edge + 1, axis=-1)], axis=-1)
    g = jnp.roll(g, -(n - 1), axis=-1)
    toep = jnp.tile(g, (1, 1, n))[..., :n * (2 * n - 1)].reshape(g.shape[:2] + (n, 2 * n - 1))[..., :n]

    reps = [0, 1, 2, nlat - 2, nlat - 1]
    i = np.arange(BLK)
    kk = np.arange(NB_KBLK * BLK)
    tabs = []
    for dlt, j in enumerate(reps):
        base = j - dlt
        r = 2 * j + i // n
        qc = i % n
        kr = 2 * base + kk // n
        kc = kk % n
        start = np.clip(r - wh // 2, 0, rows - wh)
        row_ok = (kr[None, :] >= start[:, None]) & (kr[None, :] < start[:, None] + wh)
        c0 = np.clip(qc - NA_COLS // 2, 0, n - NA_COLS)
        col_ok = (kc[None, :] >= c0[:, None]) & (kc[None, :] < c0[:, None] + NA_COLS)
        dr = np.clip(2 * (base - j) + np.arange(krows)[None, :] - np.arange(qrows)[:, None] + NA_ROWS - 1,
                     0, 2 * NA_ROWS - 2)
        slabs = jnp.stack([jnp.stack([toep[:, dr[a, b]] for b in range(krows)], axis=1)
                           for a in range(qrows)], axis=1)
        bias = slabs.transpose(0, 1, 3, 2, 4).reshape(rpb.shape[0], BLK, NB_KBLK * BLK)
        tabs.append(jnp.where(jnp.asarray(row_ok & col_ok)[None], bias, NEG_INF))
    tabs.append(jnp.full_like(tabs[0], NEG_INF))
    return jnp.stack(tabs)


def _attn_b(qb, kb, vb, rpb, lctx):
    nb, tt, _ = qb.shape
    nblk = tt // BLK
    lb = lctx // BLK
    nlat = nblk - lb
    assert nlat >= NB_KBLK
    bias = _attn_b_bias(rpb, nlat)

    def base(t):
        return jnp.clip(t - lb - 2, 0, nlat - NB_KBLK) + lb

    def variant(t):
        return jnp.where(t < lb, 5, t - base(t))

    loc = lambda i: pl.BlockSpec((1, BLK, 512), lambda b, t: (b, base(t) + i, 0))
    ctx = pl.BlockSpec((1, lctx, 512), lambda b, t: (b, 0, 0))
    return pl.pallas_call(
        _attn_b_kernel,
        grid=(nb, nblk),
        in_specs=[pl.BlockSpec((1, BLK, 512), lambda b, t: (b, t, 0))]
        + [loc(i) for i in range(NB_KBLK)] + [ctx]
        + [loc(i) for i in range(NB_KBLK)] + [ctx]
        + [pl.BlockSpec((1, B_HEADS, BLK, NB_KBLK * BLK), lambda b, t: (variant(t), 0, 0, 0))],
        out_specs=pl.BlockSpec((1, BLK, 512), lambda b, t: (b, t, 0)),
        out_shape=jax.ShapeDtypeStruct((nb, tt, 512), BF16),
        compiler_params=_cparams(("parallel", "parallel")),
        name="attn_neighbourhood",
    )(qb, *([kb] * (NB_KBLK + 1)), *([vb] * (NB_KBLK + 1)), bias)


HALO = 16


SUB = 8


NCB = C_CH // LANES


def _conv_kernel(prev_ref, cur_ref, next_ref, w_ref, b_ref, g_ref, bb_ref, o_ref, sh_ref, acc_ref, *, lb, nblk):
    t = pl.program_id(1)
    has_prev = jnp.logical_and(t != 0, t != lb)
    has_next = jnp.logical_and(t != lb - 1, t != nblk - 1)
    prev = jnp.where(has_prev, prev_ref[0], 0.0)
    nxt = jnp.where(has_next, next_ref[0], 0.0)
    cur = cur_ref[0]
    for cb in range(NCB):
        cs = slice(cb * LANES, (cb + 1) * LANES)
        sh_ref[0, cb, 0:HALO, :] = prev[:, cs]
        sh_ref[0, cb, HALO:HALO + BLK, :] = cur[:, cs]
        sh_ref[0, cb, HALO + BLK:2 * HALO + BLK, :] = nxt[:, cs]
    rows = BLK + 2 * HALO - SUB
    pad = C_KSIZE // 2

    def channel_block(cb, carry):
        for r in range(1, SUB):
            sh_ref[r, cb, 0:rows, :] = sh_ref[0, cb, pl.ds(r, rows), :]
        acc = jnp.zeros((BLK // SUB, SUB, LANES), F32)
        for kk in range(C_KSIZE):
            off = HALO - pad + kk
            rows_k = sh_ref[off % SUB, cb, off - off % SUB:off - off % SUB + BLK, :]
            acc = acc + rows_k.reshape(BLK // SUB, SUB, LANES) * w_ref[kk, cb][None]
        acc_ref[cb] = acc.reshape(BLK, LANES)
        return carry

    lax.fori_loop(0, NCB, channel_block, 0)
    y = jnp.concatenate([acc_ref[cb] for cb in range(NCB)], axis=1)
    y = _ln(y + b_ref[...]) * g_ref[...] + bb_ref[...]
    o_ref[0] = (y * jax.nn.sigmoid(y)).astype(BF16)


def _conv(yc, w, b, g, bb, lctx):
    nb, tt, ch = yc.shape
    nblk = tt // BLK
    lb = lctx // BLK
    per = BLK // HALO
    vec = pl.BlockSpec((1, ch), lambda b_, t: (0, 0))
    return pl.pallas_call(
        functools.partial(_conv_kernel, lb=lb, nblk=nblk),
        grid=(nb, nblk),
        in_specs=[
            pl.BlockSpec((1, HALO, ch), lambda b_, t: (b_, jnp.maximum(t * per - 1, 0), 0)),
            pl.BlockSpec((1, BLK, ch), lambda b_, t: (b_, t, 0)),
            pl.BlockSpec((1, HALO, ch), lambda b_, t: (b_, jnp.minimum((t + 1) * per, nblk * per - 1), 0)),
            pl.BlockSpec((C_KSIZE, NCB, SUB, LANES), lambda b_, t: (0, 0, 0, 0)),
            vec, vec, vec,
        ],
        out_specs=pl.BlockSpec((1, BLK, ch), lambda b_, t: (b_, t, 0)),
        out_shape=jax.ShapeDtypeStruct((nb, tt, ch), BF16),
        scratch_shapes=[pltpu.VMEM((SUB, NCB, BLK + 2 * HALO, LANES), F32), pltpu.VMEM((NCB, BLK, LANES), F32)],
        compiler_params=_cparams(("parallel", "parallel")),
        name="conformer_conv",
    )(yc, yc, yc, jnp.broadcast_to(w.reshape(C_KSIZE, NCB, 1, LANES), (C_KSIZE, NCB, SUB, LANES)),
      b.reshape(1, ch), g.reshape(1, ch),
      bb.reshape(1, ch))


def _mlstm_kernel(qf_ref, kf_ref, vf_ref, gcf_ref, grf_ref, qb_ref, kb_ref, vb_ref, gcb_ref, grb_ref,
                  hf_ref, hb_ref, c_st, n_st, m_st):
    @pl.when(pl.program_id(1) == 0)
    def _():
        c_st[...] = jnp.zeros_like(c_st)
        n_st[...] = jnp.zeros_like(n_st)
        m_st[...] = jnp.zeros_like(m_st)

    rr = lax.broadcasted_iota(jnp.int32, (BLK, BLK), 0)
    cc = lax.broadcasted_iota(jnp.int32, (BLK, BLK), 1)
    ch = []
    for dirn, (q_ref, k_ref, v_ref, gc_ref, gr_ref, h_ref) in enumerate(
            ((qf_ref, kf_ref, vf_ref, gcf_ref, grf_ref, hf_ref), (qb_ref, kb_ref, vb_ref, gcb_ref, grb_ref, hb_ref))):
        before = (rr >= cc) if dirn == 0 else (rr <= cc)
        bmat = jnp.where(before, 1.0, 0.0).astype(BF16)
        gc = gc_ref[0, 0]
        gr = gr_ref[0, 0]
        lf_c = jax.nn.log_sigmoid(gc[:, M_HEADS:2 * M_HEADS])
        lf_r = jax.nn.log_sigmoid(gr[M_HEADS:2 * M_HEADS, :])
        fc = sum(_dot(bmat, part) for part in _split3(lf_c))
        fr = sum(_dot_nt(part, bmat) for part in _split3(lf_r))
        f_tot = jnp.sum(lf_r, axis=-1, keepdims=True)
        for hh in range(M_HEADS):
            sl = slice(hh * M_DIM, (hh + 1) * M_DIM)
            st = dirn * M_HEADS + hh
            ch.append(dict(before=before, sl=sl, st=st, h_ref=h_ref,
                           q=q_ref[0, :, sl], k=k_ref[0, :, sl], v=v_ref[0, :, sl],
                           f_c=fc[:, hh:hh + 1], f_r=fr[hh:hh + 1, :], f_tot=f_tot[hh:hh + 1, :],
                           i_c=gc[:, hh:hh + 1], i_r=gr[hh:hh + 1, :],
                           c_old=c_st[st], n_old=n_st[st], m_old=m_st[st][:, 0:1]))

    for c in ch:
        c["qk"] = _dot_nt(c["q"], c["k"])
        c["qc"] = _dot(c["q"], c["c_old"].astype(BF16))
    for c in ch:
        c["a"] = c["f_c"] + c["m_old"]
        c["logw"] = jnp.where(c["before"], c["f_c"] - c["f_r"] + c["i_r"], -jnp.inf)
        c["mt"] = jnp.maximum(c["a"], jnp.max(c["logw"], axis=-1, keepdims=True))
    for c in ch:
        g_r = c["f_tot"] - c["f_r"] + c["i_r"]
        c["m_new"] = jnp.maximum(c["f_tot"] + c["m_old"], jnp.max(g_r, axis=-1, keepdims=True))
        c["decay"] = jnp.exp(c["f_tot"] + c["m_old"] - c["m_new"])
        c["kw"] = c["k"].astype(F32) * jnp.exp(c["f_tot"] - c["f_c"] + c["i_c"] - c["m_new"])
    for c in ch:
        c["s"] = c["qk"] * jnp.exp(c["logw"] - c["mt"])
        c["w_inter"] = jnp.exp(c["a"] - c["mt"])
    for c in ch:
        c["sv"] = _dot(c["s"].astype(BF16), c["v"])
        c["kv"] = lax.dot_general(c["kw"].astype(BF16), c["v"], (((0,), (0,)), ((), ())), preferred_element_type=F32)
    for c in ch:
        num = c["w_inter"] * c["qc"] + c["sv"]
        den = (c["w_inter"] * jnp.sum(c["q"].astype(F32) * c["n_old"], axis=-1, keepdims=True)
               + jnp.sum(c["s"], axis=-1, keepdims=True))
        hout = num / jnp.maximum(jnp.abs(den), jnp.exp(-c["mt"]))
        c["h_ref"][0, :, c["sl"]] = hout.astype(BF16)
    for c in ch:
        st = c["st"]
        c_st[st] = c["decay"] * c["c_old"] + c["kv"]
        n_st[st] = c["decay"] * c["n_old"] + jnp.sum(c["kw"], axis=0, keepdims=True)
        m_st[st] = jnp.broadcast_to(c["m_new"], (1, LANES))


def _mlstm(qd, kd, vd, gates, lctx):
    nb, tt, _ = qd.shape
    nblk = tt // BLK
    lb = lctx // BLK
    gcol = jnp.stack([gates[..., 0:8], gates[..., 8:16]])
    grow = jnp.swapaxes(gcol, 2, 3)

    def bwd(i):
        return jnp.where(i < lb, lb - 1 - i, nblk - 1 + lb - i)

    fwd = lambda i: i
    tok = lambda f: pl.BlockSpec((1, BLK, 512), lambda b, i: (b, f(i), 0))
    gcs = lambda d, f: pl.BlockSpec((1, 1, BLK, 8), lambda b, i: (d, b, f(i), 0))
    grs = lambda d, f: pl.BlockSpec((1, 1, 8, BLK), lambda b, i: (d, b, 0, f(i)))
    out = jax.ShapeDtypeStruct((nb, tt, 512), BF16)
    return pl.pallas_call(
        _mlstm_kernel,
        grid=(nb, nblk),
        in_specs=[tok(fwd), tok(fwd), tok(fwd), gcs(0, fwd), grs(0, fwd),
                  tok(bwd), tok(bwd), tok(bwd), gcs(1, bwd), grs(1, bwd)],
        out_specs=[tok(fwd), tok(bwd)],
        out_shape=[out, out],
        scratch_shapes=[pltpu.VMEM((2 * M_HEADS, M_DIM, M_DIM), F32),
                        pltpu.VMEM((2 * M_HEADS, 1, M_DIM), F32),
                        pltpu.VMEM((2 * M_HEADS, 1, LANES), F32)],
        compiler_params=_cparams(("parallel", "arbitrary")),
        name="mlstm",
    )(qd, kd, vd, gcol, grow, qd, kd, vd, gcol, grow)


def _merge_kernel(x_ref, mod_ref, ya_ref, yb_ref, yc_ref, hf_ref, hb_ref, so_ref,
                  wg_ref, bg_ref, wbr_ref, wo_ref, g1_ref, b1_ref, wr_ref,
                  x1_o, hp_o, aff_o, *, alpha):
    d = D_MODEL
    x = x_ref[0]
    mod = mod_ref[0]
    h = (_ln(x) * (1.0 + mod[:, d:2 * d]) + mod[:, 0:d]).astype(BF16)
    yd = (so_ref[0].astype(F32) * (hf_ref[0].astype(F32) + hb_ref[0].astype(F32))).astype(BF16)
    ys = (ya_ref[0], yb_ref[0], yc_ref[0], yd)
    z = None
    for i in range(N_BRANCH):
        gate = jax.nn.sigmoid(_dot(h, wg_ref[:, i * d:(i + 1) * d]) + bg_ref[:, i * d:(i + 1) * d])
        term = gate * _dot(ys[i], wbr_ref[i])
        z = term if z is None else z + term
    y = _dot(z.astype(BF16), wo_ref[...])
    x1 = _ln(alpha * x + mod[:, 2 * d:3 * d] * y) * g1_ref[...] + b1_ref[...]
    x1_o[0] = x1

    h2 = _ln(x1) * (1.0 + mod[:, 4 * d:5 * d]) + mod[:, 3 * d:4 * d]
    hb16 = h2.astype(BF16)
    bits = pltpu.bitcast(hb16.astype(F32), jnp.uint32)
    hp_o[0] = (bits[:, d // 2:] & jnp.uint32(0xFFFF0000)) | (bits[:, :d // 2] >> 16)

    logits = _dot3(h2, wr_ref[...])[:, 0:N_EXPERTS]
    e = jnp.exp(logits - jnp.max(logits, axis=-1, keepdims=True))
    aff_o[0] = e / jnp.sum(e, axis=-1, keepdims=True)


def _merge(x, mod3, ya, yb, yc, hf, hb, so, wg, bg, wbr, wo, g1, b1, wr, lctx, alpha):
    nb, tt, d = x.shape
    nt = tt // TM
    lt = lctx // TM
    tok = lambda n: pl.BlockSpec((1, TM, n), lambda b, t: (b, t, 0))
    const = lambda shape: pl.BlockSpec(shape, lambda b, t: (0,) * len(shape))
    return pl.pallas_call(
        functools.partial(_merge_kernel, alpha=alpha),
        grid=(nb, nt),
        in_specs=[
            tok(d),
            pl.BlockSpec((1, 1, 6 * d), _mod_index(lt, nb)),
            tok(512), tok(512), tok(512), tok(512), tok(512), tok(512),
            const((d, N_BRANCH * d)), const((1, N_BRANCH * d)), const((N_BRANCH, BRANCH_W, d)),
            const((d, d)), const((1, d)), const((1, d)), const((d, LANES)),
        ],
        out_specs=[tok(d), tok(d // 2), tok(N_EXPERTS)],
        out_shape=[jax.ShapeDtypeStruct((nb, tt, d), F32),
                   jax.ShapeDtypeStruct((nb, tt, d // 2), jnp.uint32),
                   jax.ShapeDtypeStruct((nb, tt, N_EXPERTS), F32)],
        compiler_params=_cparams(("parallel", "parallel"), 56),
        name="merge",
    )(x, mod3, ya, yb, yc, hf, hb, so, wg, bg.reshape(1, -1), wbr, wo,
      g1.reshape(1, d), b1.reshape(1, d), wr)


def _route_one(a_ref, cpos_ref, idx_o, val_o, *, ntok, cap, tok_off, slot_off, sblk):
    a = a_ref[0]
    bits = pltpu.bitcast(a, jnp.int32)
    thr = jnp.zeros((N_EXPERTS, 1), jnp.int32)
    for bit in range(30, -1, -1):
        cand = thr | jnp.int32(1 << bit)
        cnt = jnp.sum(jnp.where(bits >= cand, 1.0, 0.0), axis=-1, keepdims=True)
        thr = jnp.where(cnt >= cap, cand, thr)
    gt = bits > thr
    eq = bits == thr
    need = cap - jnp.sum(jnp.where(gt, 1.0, 0.0), axis=-1, keepdims=True)

    rr = lax.broadcasted_iota(jnp.int32, (LANES, LANES), 0)
    cc = lax.broadcasted_iota(jnp.int32, (LANES, LANES), 1)
    upper = jnp.where(rr <= cc, 1.0, 0.0).astype(BF16)

    def cumsum_blocks(mask_f):
        run = jnp.zeros((N_EXPERTS, 1), F32)
        out = []
        for c in range(ntok // LANES):
            blk = mask_f[:, c * LANES:(c + 1) * LANES]
            out.append(_dot(blk.astype(BF16), upper) + run)
            run = run + jnp.sum(blk, axis=-1, keepdims=True)
        return out

    eq_f = jnp.where(eq, 1.0, 0.0)
    cum_eq = cumsum_blocks(eq_f)
    sel_parts = []
    for c in range(ntok // LANES):
        sl = slice(c * LANES, (c + 1) * LANES)
        sel_parts.append(jnp.where(gt[:, sl] | (eq[:, sl] & (cum_eq[c] <= need)), 1.0, 0.0))
    sel_f = jnp.concatenate(sel_parts, axis=1)
    cpos = cumsum_blocks(sel_f)
    for c in range(ntok // LANES):
        cpos_ref[:, c * LANES:(c + 1) * LANES] = cpos[c] * sel_parts[c]

    lane = lax.broadcasted_iota(jnp.int32, (1, LANES), 1).astype(F32)
    slot1 = lax.broadcasted_iota(jnp.int32, (sblk, 1), 0).astype(F32) + 1.0

    for e in range(N_EXPERTS):
        def per_slot_block(sb, carry, e=e):
            want = slot1 + (sb * sblk).astype(F32)
            acc_i = jnp.zeros((sblk, LANES), F32)
            acc_v = jnp.zeros((sblk, LANES), F32)
            for c in range(ntok // LANES):
                sl = slice(c * LANES, (c + 1) * LANES)
                hit = cpos_ref[e:e + 1, sl] == want
                acc_i = acc_i + jnp.where(hit, lane + float(c * LANES + tok_off), 0.0)
                acc_v = acc_v + jnp.where(hit, a_ref[0, e:e + 1, sl], 0.0)
            row0 = pl.multiple_of(slot_off + sb * sblk, 8)
            idx_o[0, e, pl.ds(row0, sblk), :] = jnp.sum(acc_i, axis=-1, keepdims=True).astype(jnp.int32)
            val_o[0, e, pl.ds(row0, sblk), :] = jnp.sum(acc_v, axis=-1, keepdims=True)
            return carry
        lax.fori_loop(0, cap // sblk, per_slot_block, 0)


def _route_kernel(al_ref, ac_ref, idx_o, val_o, cpl_ref, cpc_ref, *, s, lctx, cap_l, cap_c):
    _route_one(al_ref, cpl_ref, idx_o, val_o, ntok=s, cap=cap_l, tok_off=lctx, slot_off=0, sblk=min(64, cap_l))
    _route_one(ac_ref, cpc_ref, idx_o, val_o, ntok=lctx, cap=cap_c, tok_off=0, slot_off=cap_l, sblk=min(64, cap_c))


def _route(aff, lctx):
    nb, tt, ne = aff.shape
    s = tt - lctx
    cap_l = CAPACITY_FACTOR * s // ne
    cap_c = CAPACITY_FACTOR * lctx // ne
    capt = cap_l + cap_c
    aff_t = jnp.swapaxes(aff, 1, 2)
    out = lambda dt: jax.ShapeDtypeStruct((nb, ne, capt, 1), dt)
    return pl.pallas_call(
        functools.partial(_route_kernel, s=s, lctx=lctx, cap_l=cap_l, cap_c=cap_c),
        grid=(nb,),
        in_specs=[pl.BlockSpec((1, ne, s), lambda b: (b, 0, 0)),
                  pl.BlockSpec((1, ne, lctx), lambda b: (b, 0, 0))],
        out_specs=[pl.BlockSpec((1, ne, capt, 1), lambda b: (b, 0, 0, 0))] * 2,
        out_shape=[out(jnp.int32), out(F32)],
        scratch_shapes=[pltpu.VMEM((ne, s), F32), pltpu.VMEM((ne, lctx), F32)],
        compiler_params=_cparams(("parallel",), 40),
        name="route",
    )(aff_t[:, :, lctx:], aff_t[:, :, :lctx])


def _gather_kernel(idx_ref, x_ref, o_ref, *, capt):
    def body(c, carry):
        o_ref[0, 0, pl.ds(c, 1), :] = x_ref[0, pl.ds(idx_ref[0, 0, c], 1), :]
        return carry
    lax.fori_loop(0, capt, body, 0, unroll=8)


def _gather(idx_s, hp, capt):
    nb, tt, w = hp.shape
    ne = N_EXPERTS
    return pl.pallas_call(
        functools.partial(_gather_kernel, capt=capt),
        grid=(nb, ne),
        in_specs=[pl.BlockSpec((1, 1, capt), lambda b, e: (b * ne + e, 0, 0), memory_space=pltpu.SMEM),
                  pl.BlockSpec((1, tt, w), lambda b, e: (b, 0, 0))],
        out_specs=pl.BlockSpec((1, 1, capt, w), lambda b, e: (b, e, 0, 0)),
        out_shape=jax.ShapeDtypeStruct((nb, ne, capt, w), jnp.uint32),
        compiler_params=_cparams(("parallel", "arbitrary"), 40),
        name="moe_gather",
    )(idx_s, hp)


def _ffn_kernel(x_ref, w1_ref, w3_ref, w2_ref, val_ref, y_ref, w1b, w3b, w2b):
    @pl.when(pl.program_id(1) == 0)
    def _():
        w1b[...] = w1_ref[0, 0].astype(BF16)
        w3b[...] = w3_ref[0, 0].astype(BF16)
        w2b[...] = w2_ref[0, 0].astype(BF16)

    packed = x_ref[0, 0]
    lo = pltpu.bitcast(packed << 16, F32)
    hi = pltpu.bitcast(packed & jnp.uint32(0xFFFF0000), F32)
    xg = jnp.concatenate([lo, hi], axis=1).astype(BF16)
    a = _dot(xg, w1b[...])
    hid = (a * jax.nn.sigmoid(a) * _dot(xg, w3b[...])).astype(BF16)
    y_ref[0, 0] = _dot(hid, w2b[...]) * val_ref[0, 0]


def _ffn(xg, w1, w3, w2, vals, layer):
    nb, ne, capt, w = xg.shape
    d, ff = w1.shape[2], w1.shape[3]
    return pl.pallas_call(
        _ffn_kernel,
        grid=(ne, nb),
        in_specs=[pl.BlockSpec((1, 1, capt, w), lambda e, b: (b, e, 0, 0)),
                  pl.BlockSpec((1, 1, d, ff), lambda e, b: (layer, e, 0, 0)),
                  pl.BlockSpec((1, 1, d, ff), lambda e, b: (layer, e, 0, 0)),
                  pl.BlockSpec((1, 1, ff, d), lambda e, b: (layer, e, 0, 0)),
                  pl.BlockSpec((1, 1, capt, 1), lambda e, b: (b, e, 0, 0))],
        out_specs=pl.BlockSpec((1, 1, capt, d), lambda e, b: (b, e, 0, 0)),
        out_shape=jax.ShapeDtypeStruct((nb, ne, capt, d), F32),
        scratch_shapes=[pltpu.VMEM((d, ff), BF16), pltpu.VMEM((d, ff), BF16), pltpu.VMEM((ff, d), BF16)],
        compiler_params=_cparams(("parallel", "arbitrary"), 56),
        name="moe_ffn",
    )(xg, w1, w3, w2, vals)


def _scatter_kernel(idx_ref, y_ref, o_ref, *, capt):
    @pl.when(pl.program_id(1) == 0)
    def _():
        o_ref[...] = jnp.zeros_like(o_ref)

    def body(c, carry):
        row = pl.ds(idx_ref[0, 0, c], 1)
        o_ref[0, row, :] = o_ref[0, row, :] + y_ref[0, 0, pl.ds(c, 1), :]
        return carry
    lax.fori_loop(0, capt, body, 0, unroll=4)


def _scatter(idx_s, y, tt):
    nb, ne, capt, d = y.shape
    return pl.pallas_call(
        functools.partial(_scatter_kernel, capt=capt),
        grid=(nb, ne),
        in_specs=[pl.BlockSpec((1, 1, capt), lambda b, e: (b * ne + e, 0, 0), memory_space=pltpu.SMEM),
                  pl.BlockSpec((1, 1, capt, d), lambda b, e: (b, e, 0, 0))],
        out_specs=pl.BlockSpec((1, tt, d), lambda b, e: (b, 0, 0)),
        out_shape=jax.ShapeDtypeStruct((nb, tt, d), F32),
        compiler_params=_cparams(("parallel", "arbitrary"), 56),
        name="moe_scatter",
    )(idx_s, y)


def _post_kernel(x_ref, ml_ref, mod_ref, g_ref, b_ref, o_ref, *, alpha):
    d = D_MODEL
    g2 = mod_ref[0][:, 5 * d:6 * d]
    o_ref[0] = _ln(alpha * x_ref[0] + g2 * ml_ref[0]) * g_ref[...] + b_ref[...]


def _post(x1, ml, mod3, g, b, lctx, alpha, latent_only):
    nb, tt, d = x1.shape
    lt = lctx // TM
    t0 = lt if latent_only else 0
    tok = pl.BlockSpec((1, TM, d), lambda b_, t: (b_, t + t0, 0))
    vec = pl.BlockSpec((1, d), lambda b_, t: (0, 0))
    return pl.pallas_call(
        functools.partial(_post_kernel, alpha=alpha),
        grid=(nb, tt // TM - t0),
        in_specs=[tok, tok, pl.BlockSpec((1, 1, 6 * d), lambda b_, t: (jnp.where(t + t0 < lt, nb, b_), 0, 0)),
                  vec, vec],
        out_specs=pl.BlockSpec((1, TM, d), lambda b_, t: (b_, t, 0)),
        out_shape=jax.ShapeDtypeStruct((nb, tt - t0 * TM, d), F32),
        compiler_params=_cparams(("parallel", "parallel")),
        name="moe_post",
    )(x1, ml, mod3, g.reshape(1, d), b.reshape(1, d))


def _rope_tables(s, lctx):
    half = HEAD_DIM // 2
    nf = half // 2
    inv = ROPE_BASE ** (-jnp.arange(nf, dtype=F32) / nf)
    t = jnp.arange(s)
    lane = np.arange(LANES)
    dd = lane % HEAD_DIM
    use_col = jnp.asarray(dd >= half)[None, :]
    pos = jnp.where(use_col, (t % GRID_W)[:, None], (t // GRID_W)[:, None]).astype(F32)
    ang = pos * inv[jnp.asarray(dd % nf)][None, :]
    cos, sin = jnp.cos(ang), jnp.sin(ang)
    first = jnp.asarray((dd % half) < nf)[None, :]
    sa = jnp.where(first, -sin, 0.0)
    sb = jnp.where(first, 0.0, sin)
    pad = lambda a, v: jnp.concatenate([jnp.full((lctx, LANES), v, F32), a], axis=0)
    return pad(cos, 1.0), pad(sa, 0.0), pad(sb, 0.0)


def kernel(x, c, ctx, c_ctx, w_mod, b_mod, w_in, attn_sink, na_rpb, conv_w, conv_b, conv_ln_g, conv_ln_b,
           mlstm_gate_b, w_branch, w_gate, b_gate, w_out, ln1_g, ln1_b, w_router, w_exp_gate, w_exp_up,
           w_exp_down, ln2_g, ln2_b):
    nb, s, d = x.shape
    lctx = ctx.shape[1]
    depth = w_mod.shape[0]
    assert d == D_MODEL and nb + 1 <= 8 and lctx % TM == 0 and s % TM == 0 and s % GRID_W == 0
    alpha = (2.0 * depth) ** 0.25
    tt = lctx + s
    cap_t = CAPACITY_FACTOR * s // N_EXPERTS + CAPACITY_FACTOR * lctx // N_EXPERTS

    c8 = jnp.concatenate([c, c_ctx[None, :], jnp.zeros((8 - nb - 1, d), F32)], axis=0)
    mod_all = _modulation(c8, w_mod, b_mod)
    tabs = _rope_tables(s, lctx)
    xs = jnp.concatenate([ctx, x], axis=1)

    for l in range(depth):
        mod3 = mod_all[l].reshape(8, 1, 6 * d)
        w_bf = jnp.pad(w_in[l], ((0, 0), (0, PROJ_PAD - PROJ_W))).astype(BF16)
        (qa, kva, qb, kb, vb, yc0, qd, kd, vd, so, gates) = _inproj(xs, mod3, tabs, w_bf, mlstm_gate_b[l], lctx)
        ya = _attn_a(qa, kva, attn_sink[l], lctx)
        yb = _attn_b(qb, kb, vb, na_rpb[l], lctx)
        yc = _conv(yc0, conv_w[l], conv_b[l], conv_ln_g[l], conv_ln_b[l], lctx)
        hf, hb = _mlstm(qd, kd, vd, gates, lctx)
        wr = jnp.pad(w_router[l], ((0, 0), (0, LANES - N_EXPERTS)))
        x1, hp, aff = _merge(xs, mod3, ya, yb, yc, hf, hb, so, w_gate[l].astype(BF16), b_gate[l],
                             w_branch[l].astype(BF16), w_out[l].astype(BF16), ln1_g[l], ln1_b[l], wr,
                             lctx, alpha)
        idx, vals = _route(aff, lctx)
        idx_s = idx.reshape(nb * N_EXPERTS, 1, cap_t)
        xg = _gather(idx_s, hp, cap_t)
        y = _ffn(xg, w_exp_gate, w_exp_up, w_exp_down, vals, l)
        ml = _scatter(idx_s, y, tt)
        xs = _post(x1, ml, mod3, ln2_g[l], ln2_b[l], lctx, alpha, latent_only=(l == depth - 1))
    return xs
```

```python
import functools
import math

import numpy as np
import jax
import jax.numpy as jnp
from jax import lax
from jax.experimental import pallas as pl
from jax.experimental.pallas import tpu as pltpu

F32 = jnp.float32
BF16 = jnp.bfloat16

D_MODEL = 1024
GRID_W = 64
HEAD_DIM = 64
A_HEADS = 8
A_KV_HEADS = 2
A_WINDOW = 128
B_HEADS = 8
NA_ROWS = 8
NA_COLS = 16
C_CH = 512
C_KSIZE = 31
M_HEADS = 4
M_DIM = 128
N_BRANCH = 4
BRANCH_W = 512
N_EXPERTS = 16
EXPERT_FF = 1024
CAPACITY_FACTOR = 2
ROPE_BASE = 10000.0
LN_EPS = 1e-6
NEG_INF = -1e30

BLK = 128
TM = 256
LANES = 128
PROJ_W = 5392
PROJ_PAD = 5504
OFF_A, OFF_AKV, OFF_B, OFF_C, OFF_D, OFF_G = 0, 512, 768, 2304, 3328, 5376
MIB = 1 << 20


def _cparams(sem, vmem_mib=None):
    kw = dict(dimension_semantics=sem)
    if vmem_mib is not None:
        kw["vmem_limit_bytes"] = vmem_mib * MIB
    return pltpu.CompilerParams(**kw)


def _ln(x):
    mu = jnp.mean(x, axis=-1, keepdims=True)
    xc = x - mu
    var = jnp.mean(xc * xc, axis=-1, keepdims=True)
    return xc * lax.rsqrt(var + LN_EPS)


def _dot(a, b):
    return jnp.dot(a, b, preferred_element_type=F32)


def _dot_nt(a, b):
    return lax.dot_general(a, b, (((1,), (1,)), ((), ())), preferred_element_type=F32)


def _split2(x):
    hi = x.astype(BF16)
    lo = (x - hi.astype(F32)).astype(BF16)
    return hi, lo


def _split3(x):
    hi = x.astype(BF16)
    r = x - hi.astype(F32)
    mid = r.astype(BF16)
    lo = (r - mid.astype(F32)).astype(BF16)
    return hi, mid, lo


def _dot3(x, w):
    xh, xl = _split2(x)
    wh, wl = _split2(w)
    return _dot(xh, wh) + (_dot(xh, wl) + _dot(xl, wh))


def _mod_kernel(c_ref, w_ref, b_ref, o_ref):
    c = c_ref[...]
    s = c * jax.nn.sigmoid(c)
    o_ref[0] = _dot3(s, w_ref[0]) + b_ref[0]


def _modulation(c8, w_mod, b_mod):
    depth, d, d6 = w_mod.shape
    nj = d6 // d
    return pl.pallas_call(
        _mod_kernel,
        grid=(depth, nj),
        in_specs=[
            pl.BlockSpec((8, d), lambda l, j: (0, 0)),
            pl.BlockSpec((1, d, d), lambda l, j: (l, 0, j)),
            pl.BlockSpec((1, 1, d), lambda l, j: (l, 0, j)),
        ],
        out_specs=pl.BlockSpec((1, 8, d), lambda l, j: (l, 0, j)),
        out_shape=jax.ShapeDtypeStruct((depth, 8, d6), F32),
        compiler_params=_cparams(("arbitrary", "arbitrary"), 40),
        name="modulation",
    )(c8, w_mod, b_mod.reshape(depth, 1, d6))


def _rope(x, cos, sa, sb):
    parts = []
    for j in range(x.shape[1] // LANES):
        xj = x[:, j * LANES:(j + 1) * LANES]
        parts.append(xj * cos + pltpu.roll(xj, LANES - 16, 1) * sa + pltpu.roll(xj, 16, 1) * sb)
    return parts[0] if len(parts) == 1 else jnp.concatenate(parts, axis=1)


def _inproj_kernel(*refs, alpha):
    d = D_MODEL
    if alpha is None:
        x_ref, mod_ref, cos_ref, sa_ref, sb_ref, w_ref, gb_ref = refs[:7]
        outs = refs[7:]
        x = x_ref[0]
    else:
        x1_ref, ml_ref, pmod_ref, pg_ref, pb_ref, mod_ref, cos_ref, sa_ref, sb_ref, w_ref, gb_ref = refs[:11]
        x_o = refs[11]
        outs = refs[12:]
        x = _ln(alpha * x1_ref[0] + pmod_ref[0][:, 5 * d:6 * d] * ml_ref[0]) * pg_ref[...] + pb_ref[...]
        x_o[0] = x
    qa_o, kva_o, qb_o, kb_o, vb_o, yc_o, qd_o, kd_o, vd_o, so_o, g_o = outs
    mod = mod_ref[0]
    h = (_ln(x) * (1.0 + mod[:, d:2 * d]) + mod[:, 0:d]).astype(BF16)
    cos, sa, sb = cos_ref[...], sa_ref[...], sb_ref[...]
    qscale = HEAD_DIM ** -0.5

    qa = _dot(h, w_ref[:, OFF_A:OFF_AKV])
    qa_o[0] = (_rope(qa, cos, sa, sb) * qscale).astype(BF16)
    kva = _dot(h, w_ref[:, OFF_AKV:OFF_B])
    kva_o[0, :, 0:LANES] = _rope(kva[:, 0:LANES], cos, sa, sb).astype(BF16)
    kva_o[0, :, LANES:2 * LANES] = kva[:, LANES:2 * LANES].astype(BF16)

    qb_o[0] = (_dot(h, w_ref[:, OFF_B:OFF_B + 512]) * qscale).astype(BF16)
    kb_o[0] = _dot(h, w_ref[:, OFF_B + 512:OFF_B + 1024]).astype(BF16)
    vb_o[0] = _dot(h, w_ref[:, OFF_B + 1024:OFF_C]).astype(BF16)

    ua = _dot(h, w_ref[:, OFF_C:OFF_C + C_CH])
    ug = _dot(h, w_ref[:, OFF_C + C_CH:OFF_D])
    yc_o[0] = ua * jax.nn.sigmoid(ug)

    qd_o[0] = _dot(h, w_ref[:, OFF_D:OFF_D + 512]).astype(BF16)
    kd_o[0] = (_dot(h, w_ref[:, OFF_D + 512:OFF_D + 1024]) * (M_DIM ** -0.5)).astype(BF16)
    vd_o[0] = _dot(h, w_ref[:, OFF_D + 1024:OFF_D + 1536]).astype(BF16)
    so_o[0] = jax.nn.sigmoid(_dot(h, w_ref[:, OFF_D + 1536:OFF_G])).astype(BF16)

    g = _dot(h, w_ref[:, OFF_G:PROJ_PAD])
    g_o[0] = g[:, 0:16] + gb_ref[...]


def _mod_index(lt, nb):
    return lambda b, t: (jnp.where(t < lt, nb, b), 0, 0)


def _inproj(x, mod3, tabs, w_bf, gate_b, lctx, prev=None):
    nb, tt, d = (x if prev is None else prev[0]).shape
    nt = tt // TM
    lt = lctx // TM
    tok = lambda n: pl.BlockSpec((1, TM, n), lambda b, t: (b, t, 0))
    tab = pl.BlockSpec((TM, LANES), lambda b, t: (t, 0))
    vec = pl.BlockSpec((1, d), lambda b, t: (0, 0))
    modspec = pl.BlockSpec((1, 1, 6 * d), _mod_index(lt, nb))
    sds = lambda n, dt: jax.ShapeDtypeStruct((nb, tt, n), dt)
    in_specs = [modspec, tab, tab, tab,
                pl.BlockSpec((d, PROJ_PAD), lambda b, t: (0, 0)),
                pl.BlockSpec((1, 16), lambda b, t: (0, 0))]
    args = [mod3, tabs[0], tabs[1], tabs[2], w_bf, gate_b.reshape(1, 16)]
    out_specs = [tok(512), tok(256), tok(512), tok(512), tok(512), tok(512),
                 tok(512), tok(512), tok(512), tok(512), tok(16)]
    out_shape = [sds(512, BF16), sds(256, BF16), sds(512, BF16), sds(512, BF16), sds(512, BF16),
                 sds(512, F32), sds(512, BF16), sds(512, BF16), sds(512, BF16), sds(512, BF16),
                 sds(16, F32)]
    if prev is None:
        in_specs = [tok(d)] + in_specs
        args = [x] + args
        alpha = None
    else:
        x1, ml, pmod3, pg, pb, alpha = prev
        in_specs = [tok(d), tok(d), modspec, vec, vec] + in_specs
        args = [x1, ml, pmod3, pg.reshape(1, d), pb.reshape(1, d)] + args
        out_specs = [tok(d)] + out_specs
        out_shape = [sds(d, F32)] + out_shape
    return pl.pallas_call(
        functools.partial(_inproj_kernel, alpha=alpha),
        grid=(nb, nt),
        in_specs=in_specs,
        out_specs=out_specs,
        out_shape=out_shape,
        compiler_params=_cparams(("parallel", "parallel"), 48),
        name="inproj",
    )(*args)


def _attn_a_kernel(sink_ref, q_ref, kp_ref, kc_ref, kn_ref, kx_ref, mask_ref, o_ref):
    q = q_ref[0]
    kv = jnp.concatenate([kp_ref[0], kc_ref[0], kn_ref[0], kx_ref[0]], axis=0)
    mask = mask_ref[0]
    group = A_HEADS // A_KV_HEADS
    for g in range(A_KV_HEADS):
        k = kv[:, g * HEAD_DIM:(g + 1) * HEAD_DIM]
        v = kv[:, LANES + g * HEAD_DIM:LANES + (g + 1) * HEAD_DIM]
        heads = range(g * group, (g + 1) * group)
        qs = jnp.concatenate([q[:, hh * HEAD_DIM:(hh + 1) * HEAD_DIM] for hh in heads], axis=0)
        s = _dot_nt(qs, k)
        ps, ls = [], []
        for r, hh in enumerate(heads):
            sr = s[r * BLK:(r + 1) * BLK] + mask
            sink = sink_ref[hh]
            m = jnp.maximum(jnp.max(sr, axis=-1, keepdims=True), sink)
            p = jnp.exp(sr - m)
            ls.append(jnp.sum(p, axis=-1, keepdims=True) + jnp.exp(sink - m))
            ps.append(p.astype(BF16))
        o = _dot(jnp.concatenate(ps, axis=0), v)
        for r, hh in enumerate(heads):
            o_ref[0, :, hh * HEAD_DIM:(hh + 1) * HEAD_DIM] = (o[r * BLK:(r + 1) * BLK] / ls[r]).astype(BF16)


def _attn_a_mask(lctx):
    i = np.arange(BLK)[:, None]
    j = np.arange(BLK)[None, :]
    ok_prev = (j >= i)
    ok_next = (j <= i)
    yes = np.ones((BLK, BLK), bool)
    no = np.zeros((BLK, BLK), bool)
    ctx = np.ones((BLK, lctx), bool)
    variants = [
        np.concatenate([ok_prev, yes, ok_next, ctx], 1),
        np.concatenate([no, yes, ok_next, ctx], 1),
        np.concatenate([ok_prev, yes, no, ctx], 1),
        np.concatenate([no, no, no, ctx], 1),
    ]
    return jnp.asarray(np.where(np.stack(variants), 0.0, NEG_INF).astype(np.float32))


def _attn_a(qa, kva, sink, lctx):
    nb, tt, _ = qa.shape
    nblk = tt // BLK
    lb = lctx // BLK
    assert nblk - lb >= 2
    mask = _attn_a_mask(lctx)

    def variant(t):
        return jnp.where(t < lb, 3, jnp.where(t == lb, 1, jnp.where(t == nblk - 1, 2, 0)))

    kvb = lambda f: pl.BlockSpec((1, BLK, 256), lambda b, t: (b, f(t), 0))
    return pl.pallas_call(
        _attn_a_kernel,
        grid=(nb, nblk),
        in_specs=[
            pl.BlockSpec(memory_space=pltpu.SMEM),
            pl.BlockSpec((1, BLK, 512), lambda b, t: (b, t, 0)),
            kvb(lambda t: jnp.maximum(t - 1, 0)),
            kvb(lambda t: t),
            kvb(lambda t: jnp.minimum(t + 1, nblk - 1)),
            pl.BlockSpec((1, lctx, 256), lambda b, t: (b, 0, 0)),
            pl.BlockSpec((1, BLK, 3 * BLK + lctx), lambda b, t: (variant(t), 0, 0)),
        ],
        out_specs=pl.BlockSpec((1, BLK, 512), lambda b, t: (b, t, 0)),
        out_shape=jax.ShapeDtypeStruct((nb, tt, 512), BF16),
        compiler_params=_cparams(("parallel", "parallel")),
        name="attn_window",
    )(sink, qa, kva, kva, kva, kva, mask)


NB_KBLK = 5


def _attn_b_kernel(q_ref, k0, k1, k2, k3, k4, kx, v0, v1, v2, v3, v4, vx, bias_ref, o_ref):
    q = q_ref[0]
    k = jnp.concatenate([k0[0], k1[0], k2[0], k3[0], k4[0], kx[0]], axis=0)
    v = jnp.concatenate([v0[0], v1[0], v2[0], v3[0], v4[0], vx[0]], axis=0)
    nloc = NB_KBLK * BLK
    first = lax.broadcasted_iota(jnp.int32, (1, LANES), 1) < HEAD_DIM
    for pair in range(B_HEADS // 2):
        sl = slice(pair * LANES, (pair + 1) * LANES)
        q2, k2, v2 = q[:, sl], k[:, sl], v[:, sl]
        zero = jnp.zeros_like(q2)
        s2 = _dot_nt(jnp.concatenate([jnp.where(first, q2, zero), jnp.where(first, zero, q2)], axis=0), k2)
        ps, ls = [], []
        for j in range(2):
            s = s2[j * BLK:(j + 1) * BLK]
            s = jnp.concatenate([s[:, :nloc] + bias_ref[0, 0, 2 * pair + j], s[:, nloc:]], axis=1)
            p = jnp.exp(s - jnp.max(s, axis=-1, keepdims=True))
            ls.append(jnp.sum(p, axis=-1, keepdims=True))
            ps.append(p.astype(BF16))
        o2 = _dot(jnp.concatenate(ps, axis=0), v2)
        o_ref[0, :, sl] = jnp.where(first, o2[:BLK] / ls[0], o2[BLK:] / ls[1]).astype(BF16)


def _attn_b_bias(rpb, nlat):
    rows = 2 * nlat
    wh = min(NA_ROWS, rows)
    n = GRID_W
    qrows, krows = BLK // n, NB_KBLK * BLK // n
    nvar = NB_KBLK
    edge = n - NA_COLS
    g = jnp.concatenate([jnp.repeat(rpb[..., :1], edge, axis=-1), rpb.astype(F32),
                         jnp.repeat(rpb[..., -1:], edge + 1, axis=-1)], axis=-1)
    g = jnp.roll(g, -(n - 1), axis=-1)
    toep = jnp.tile(g, (1, 1, 1, n))[..., :n * (2 * n - 1)].reshape(g.shape[:3] + (n, 2 * n - 1))[..., :n]

    reps = [0, 1, 2, nlat - 2, nlat - 1]
    i = np.arange(BLK)
    kk = np.arange(NB_KBLK * BLK)
    ok, dr = [], []
    for dlt, j in enumerate(reps):
        base = j - dlt
        r = 2 * j + i // n
        qc = i % n
        kr = 2 * base + kk // n
        kc = kk % n
        start = np.clip(r - wh // 2, 0, rows - wh)
        row_ok = (kr[None, :] >= start[:, None]) & (kr[None, :] < start[:, None] + wh)
        c0 = np.clip(qc - NA_COLS // 2, 0, n - NA_COLS)
        col_ok = (kc[None, :] >= c0[:, None]) & (kc[None, :] < c0[:, None] + NA_COLS)
        ok.append(row_ok & col_ok)
        dr.append(np.clip(2 * (base - j) + np.arange(krows)[None, :] - np.arange(qrows)[:, None] + NA_ROWS - 1,
                          0, 2 * NA_ROWS - 2))
    dr = np.stack(dr).reshape(-1)
    slabs = jnp.stack([toep[:, :, d] for d in dr], axis=2)
    depth, nh = rpb.shape[:2]
    bias = slabs.reshape(depth, nh, nvar, qrows, krows, n, n).transpose(0, 2, 1, 3, 5, 4, 6)
    bias = bias.reshape(depth, nvar, nh, BLK, NB_KBLK * BLK)
    bias = jnp.where(jnp.asarray(np.stack(ok))[None, :, None], bias, NEG_INF)
    return jnp.concatenate([bias, jnp.full((depth, 1, nh, BLK, NB_KBLK * BLK), NEG_INF, F32)], axis=1)


def _attn_b(qb, kb, vb, bias, layer, lctx):
    nb, tt, _ = qb.shape
    nblk = tt // BLK
    lb = lctx // BLK
    nlat = nblk - lb

    def base(t):
        return jnp.clip(t - lb - 2, 0, nlat - NB_KBLK) + lb

    def variant(t):
        return jnp.where(t < lb, 5, t - base(t))

    loc = lambda i: pl.BlockSpec((1, BLK, 512), lambda b, t: (b, base(t) + i, 0))
    ctx = pl.BlockSpec((1, lctx, 512), lambda b, t: (b, 0, 0))
    return pl.pallas_call(
        _attn_b_kernel,
        grid=(nb, nblk),
        in_specs=[pl.BlockSpec((1, BLK, 512), lambda b, t: (b, t, 0))]
        + [loc(i) for i in range(NB_KBLK)] + [ctx]
        + [loc(i) for i in range(NB_KBLK)] + [ctx]
        + [pl.BlockSpec((1, 1, B_HEADS, BLK, NB_KBLK * BLK), lambda b, t: (layer, variant(t), 0, 0, 0))],
        out_specs=pl.BlockSpec((1, BLK, 512), lambda b, t: (b, t, 0)),
        out_shape=jax.ShapeDtypeStruct((nb, tt, 512), BF16),
        compiler_params=_cparams(("parallel", "parallel")),
        name="attn_neighbourhood",
    )(qb, *([kb] * (NB_KBLK + 1)), *([vb] * (NB_KBLK + 1)), bias)


HALO = 16


SUB = 8


NCB = C_CH // LANES


def _conv_kernel(prev_ref, cur_ref, next_ref, w_ref, b_ref, g_ref, bb_ref, o_ref, sh_ref, acc_ref, *, lb, nblk):
    t = pl.program_id(1)
    has_prev = jnp.logical_and(t != 0, t != lb)
    has_next = jnp.logical_and(t != lb - 1, t != nblk - 1)
    prev = jnp.where(has_prev, prev_ref[0], 0.0)
    nxt = jnp.where(has_next, next_ref[0], 0.0)
    cur = cur_ref[0]
    for cb in range(NCB):
        cs = slice(cb * LANES, (cb + 1) * LANES)
        sh_ref[0, cb, 0:HALO, :] = prev[:, cs]
        sh_ref[0, cb, HALO:HALO + BLK, :] = cur[:, cs]
        sh_ref[0, cb, HALO + BLK:2 * HALO + BLK, :] = nxt[:, cs]
    rows = BLK + 2 * HALO - SUB
    pad = C_KSIZE // 2

    def channel_block(cb, carry):
        for r in range(1, SUB):
            sh_ref[r, cb, 0:rows, :] = sh_ref[0, cb, pl.ds(r, rows), :]
        acc = jnp.zeros((BLK // SUB, SUB, LANES), F32)
        for kk in range(C_KSIZE):
            off = HALO - pad + kk
            rows_k = sh_ref[off % SUB, cb, off - off % SUB:off - off % SUB + BLK, :]
            acc = acc + rows_k.reshape(BLK // SUB, SUB, LANES) * w_ref[kk, cb][None]
        acc_ref[cb] = acc.reshape(BLK, LANES)
        return carry

    lax.fori_loop(0, NCB, channel_block, 0)
    y = jnp.concatenate([acc_ref[cb] for cb in range(NCB)], axis=1)
    y = _ln(y + b_ref[...]) * g_ref[...] + bb_ref[...]
    o_ref[0] = (y * jax.nn.sigmoid(y)).astype(BF16)


def _conv(yc, w, b, g, bb, lctx):
    nb, tt, ch = yc.shape
    nblk = tt // BLK
    lb = lctx // BLK
    per = BLK // HALO
    vec = pl.BlockSpec((1, ch), lambda b_, t: (0, 0))
    return pl.pallas_call(
        functools.partial(_conv_kernel, lb=lb, nblk=nblk),
        grid=(nb, nblk),
        in_specs=[
            pl.BlockSpec((1, HALO, ch), lambda b_, t: (b_, jnp.maximum(t * per - 1, 0), 0)),
            pl.BlockSpec((1, BLK, ch), lambda b_, t: (b_, t, 0)),
            pl.BlockSpec((1, HALO, ch), lambda b_, t: (b_, jnp.minimum((t + 1) * per, nblk * per - 1), 0)),
            pl.BlockSpec((C_KSIZE, NCB, SUB, LANES), lambda b_, t: (0, 0, 0, 0)),
            vec, vec, vec,
        ],
        out_specs=pl.BlockSpec((1, BLK, ch), lambda b_, t: (b_, t, 0)),
        out_shape=jax.ShapeDtypeStruct((nb, tt, ch), BF16),
        scratch_shapes=[pltpu.VMEM((SUB, NCB, BLK + 2 * HALO, LANES), F32), pltpu.VMEM((NCB, BLK, LANES), F32)],
        compiler_params=_cparams(("parallel", "parallel")),
        name="conformer_conv",
    )(yc, yc, yc, jnp.broadcast_to(w.reshape(C_KSIZE, NCB, 1, LANES), (C_KSIZE, NCB, SUB, LANES)),
      b.reshape(1, ch), g.reshape(1, ch),
      bb.reshape(1, ch))


def _mlstm_kernel(qf_ref, kf_ref, vf_ref, gcf_ref, grf_ref, qb_ref, kb_ref, vb_ref, gcb_ref, grb_ref,
                  hf_ref, hb_ref, c_st, n_st, m_st):
    @pl.when(pl.program_id(1) == 0)
    def _():
        c_st[...] = jnp.zeros_like(c_st)
        n_st[...] = jnp.zeros_like(n_st)
        m_st[...] = jnp.zeros_like(m_st)

    rr = lax.broadcasted_iota(jnp.int32, (BLK, BLK), 0)
    cc = lax.broadcasted_iota(jnp.int32, (BLK, BLK), 1)
    ch = []
    for dirn, (q_ref, k_ref, v_ref, gc_ref, gr_ref, h_ref) in enumerate(
            ((qf_ref, kf_ref, vf_ref, gcf_ref, grf_ref, hf_ref), (qb_ref, kb_ref, vb_ref, gcb_ref, grb_ref, hb_ref))):
        before = (rr >= cc) if dirn == 0 else (rr <= cc)
        bmat = jnp.where(before, 1.0, 0.0).astype(BF16)
        gc = gc_ref[0, 0]
        gr = gr_ref[0, 0]
        lf_c = jax.nn.log_sigmoid(gc[:, M_HEADS:2 * M_HEADS])
        lf_r = jax.nn.log_sigmoid(gr[M_HEADS:2 * M_HEADS, :])
        fc = sum(_dot(bmat, part) for part in _split3(lf_c))
        fr = sum(_dot_nt(part, bmat) for part in _split3(lf_r))
        f_tot = jnp.sum(lf_r, axis=-1, keepdims=True)
        for hh in range(M_HEADS):
            sl = slice(hh * M_DIM, (hh + 1) * M_DIM)
            st = dirn * M_HEADS + hh
            ch.append(dict(before=before, sl=sl, st=st, h_ref=h_ref,
                           q=q_ref[0, :, sl], k=k_ref[0, :, sl], v=v_ref[0, :, sl],
                           f_c=fc[:, hh:hh + 1], f_r=fr[hh:hh + 1, :], f_tot=f_tot[hh:hh + 1, :],
                           i_c=gc[:, hh:hh + 1], i_r=gr[hh:hh + 1, :],
                           c_old=c_st[st], n_old=n_st[st], m_old=m_st[st][:, 0:1]))

    for c in ch:
        c["qk"] = _dot_nt(c["q"], c["k"])
        c["qc"] = _dot(c["q"], c["c_old"].astype(BF16))
    for c in ch:
        c["a"] = c["f_c"] + c["m_old"]
        c["logw"] = jnp.where(c["before"], c["f_c"] - c["f_r"] + c["i_r"], -jnp.inf)
        c["mt"] = jnp.maximum(c["a"], jnp.max(c["logw"], axis=-1, keepdims=True))
    for c in ch:
        g_r = c["f_tot"] - c["f_r"] + c["i_r"]
        c["m_new"] = jnp.maximum(c["f_tot"] + c["m_old"], jnp.max(g_r, axis=-1, keepdims=True))
        c["decay"] = jnp.exp(c["f_tot"] + c["m_old"] - c["m_new"])
        c["kw"] = c["k"].astype(F32) * jnp.exp(c["f_tot"] - c["f_c"] + c["i_c"] - c["m_new"])
    for c in ch:
        c["s"] = c["qk"] * jnp.exp(c["logw"] - c["mt"])
        c["w_inter"] = jnp.exp(c["a"] - c["mt"])
    for c in ch:
        c["sv"] = _dot(c["s"].astype(BF16), c["v"])
        c["kv"] = lax.dot_general(c["kw"].astype(BF16), c["v"], (((0,), (0,)), ((), ())), preferred_element_type=F32)
    for c in ch:
        num = c["w_inter"] * c["qc"] + c["sv"]
        den = (c["w_inter"] * jnp.sum(c["q"].astype(F32) * c["n_old"], axis=-1, keepdims=True)
               + jnp.sum(c["s"], axis=-1, keepdims=True))
        hout = num / jnp.maximum(jnp.abs(den), jnp.exp(-c["mt"]))
        c["h_ref"][0, :, c["sl"]] = hout.astype(BF16)
    for c in ch:
        st = c["st"]
        c_st[st] = c["decay"] * c["c_old"] + c["kv"]
        n_st[st] = c["decay"] * c["n_old"] + jnp.sum(c["kw"], axis=0, keepdims=True)
        m_st[st] = jnp.broadcast_to(c["m_new"], (1, LANES))


def _mlstm(qd, kd, vd, gates, lctx):
    nb, tt, _ = qd.shape
    nblk = tt // BLK
    lb = lctx // BLK
    gcol = jnp.stack([gates[..., 0:8], gates[..., 8:16]])
    grow = jnp.swapaxes(gcol, 2, 3)

    def bwd(i):
        return jnp.where(i < lb, lb - 1 - i, nblk - 1 + lb - i)

    fwd = lambda i: i
    tok = lambda f: pl.BlockSpec((1, BLK, 512), lambda b, i: (b, f(i), 0))
    gcs = lambda d, f: pl.BlockSpec((1, 1, BLK, 8), lambda b, i: (d, b, f(i), 0))
    grs = lambda d, f: pl.BlockSpec((1, 1, 8, BLK), lambda b, i: (d, b, 0, f(i)))
    out = jax.ShapeDtypeStruct((nb, tt, 512), BF16)
    return pl.pallas_call(
        _mlstm_kernel,
        grid=(nb, nblk),
        in_specs=[tok(fwd), tok(fwd), tok(fwd), gcs(0, fwd), grs(0, fwd),
                  tok(bwd), tok(bwd), tok(bwd), gcs(1, bwd), grs(1, bwd)],
        out_specs=[tok(fwd), tok(bwd)],
        out_shape=[out, out],
        scratch_shapes=[pltpu.VMEM((2 * M_HEADS, M_DIM, M_DIM), F32),
                        pltpu.VMEM((2 * M_HEADS, 1, M_DIM), F32),
                        pltpu.VMEM((2 * M_HEADS, 1, LANES), F32)],
        compiler_params=_cparams(("parallel", "arbitrary")),
        name="mlstm",
    )(qd, kd, vd, gcol, grow, qd, kd, vd, gcol, grow)


def _merge_kernel(x_ref, mod_ref, ya_ref, yb_ref, yc_ref, hf_ref, hb_ref, so_ref,
                  wg_ref, bg_ref, wbr_ref, wo_ref, g1_ref, b1_ref, wr_ref,
                  x1_o, hp_o, aff_o, *, alpha):
    d = D_MODEL
    x = x_ref[0]
    mod = mod_ref[0]
    h = (_ln(x) * (1.0 + mod[:, d:2 * d]) + mod[:, 0:d]).astype(BF16)
    yd = (so_ref[0].astype(F32) * (hf_ref[0].astype(F32) + hb_ref[0].astype(F32))).astype(BF16)
    ys = (ya_ref[0], yb_ref[0], yc_ref[0], yd)
    z = None
    for i in range(N_BRANCH):
        gate = jax.nn.sigmoid(_dot(h, wg_ref[:, i * d:(i + 1) * d]) + bg_ref[:, i * d:(i + 1) * d])
        term = gate * _dot(ys[i], wbr_ref[i])
        z = term if z is None else z + term
    y = _dot(z.astype(BF16), wo_ref[...])
    x1 = _ln(alpha * x + mod[:, 2 * d:3 * d] * y) * g1_ref[...] + b1_ref[...]
    x1_o[0] = x1

    h2 = _ln(x1) * (1.0 + mod[:, 4 * d:5 * d]) + mod[:, 3 * d:4 * d]
    hb16 = h2.astype(BF16)
    bits = pltpu.bitcast(hb16.astype(F32), jnp.uint32)
    hp_o[0] = (bits[:, d // 2:] & jnp.uint32(0xFFFF0000)) | (bits[:, :d // 2] >> 16)

    h2_hi, h2_lo = _split2(h2)
    r_hi = _dot(h2_hi, wr_ref[...])
    r_lo = _dot(h2_lo, wr_ref[...])
    logits = r_hi[:, 0:N_EXPERTS] + (r_hi[:, N_EXPERTS:2 * N_EXPERTS] + r_lo[:, 0:N_EXPERTS])
    e = jnp.exp(logits - jnp.max(logits, axis=-1, keepdims=True))
    aff_o[0] = e / jnp.sum(e, axis=-1, keepdims=True)


def _merge(x, mod3, ya, yb, yc, hf, hb, so, wg, bg, wbr, wo, g1, b1, wr, lctx, alpha):
    nb, tt, d = x.shape
    nt = tt // TM
    lt = lctx // TM
    tok = lambda n: pl.BlockSpec((1, TM, n), lambda b, t: (b, t, 0))
    const = lambda shape: pl.BlockSpec(shape, lambda b, t: (0,) * len(shape))
    return pl.pallas_call(
        functools.partial(_merge_kernel, alpha=alpha),
        grid=(nb, nt),
        in_specs=[
            tok(d),
            pl.BlockSpec((1, 1, 6 * d), _mod_index(lt, nb)),
            tok(512), tok(512), tok(512), tok(512), tok(512), tok(512),
            const((d, N_BRANCH * d)), const((1, N_BRANCH * d)), const((N_BRANCH, BRANCH_W, d)),
            const((d, d)), const((1, d)), const((1, d)), const((d, LANES)),
        ],
        out_specs=[tok(d), tok(d // 2), tok(N_EXPERTS)],
        out_shape=[jax.ShapeDtypeStruct((nb, tt, d), F32),
                   jax.ShapeDtypeStruct((nb, tt, d // 2), jnp.uint32),
                   jax.ShapeDtypeStruct((nb, tt, N_EXPERTS), F32)],
        compiler_params=_cparams(("parallel", "parallel"), 56),
        name="merge",
    )(x, mod3, ya, yb, yc, hf, hb, so, wg, bg.reshape(1, -1), wbr, wo,
      g1.reshape(1, d), b1.reshape(1, d), wr)


def _route_one(a_ref, ones_ref, tab_ref, idx_o, val_o, *, ntok, cap, tok_off, slot_off):
    a = a_ref[0]
    bits = pltpu.bitcast(a, jnp.int32)
    thr = jnp.zeros((N_EXPERTS, 1), jnp.int32)
    for bit in range(30, -1, -1):
        cand = thr | jnp.int32(1 << bit)
        cnt = jnp.sum(jnp.where(bits >= cand, 1.0, 0.0), axis=-1, keepdims=True)
        thr = jnp.where(cnt >= cap, cand, thr)
    gt = bits > thr
    eq = bits == thr
    need = cap - jnp.sum(jnp.where(gt, 1.0, 0.0), axis=-1, keepdims=True)

    rr = lax.broadcasted_iota(jnp.int32, (LANES, LANES), 0)
    cc = lax.broadcasted_iota(jnp.int32, (LANES, LANES), 1)
    upper = jnp.where(rr <= cc, 1.0, 0.0).astype(BF16)

    def cumsum_blocks(mask_f):
        run = jnp.zeros((N_EXPERTS, 1), F32)
        out = []
        for c in range(ntok // LANES):
            blk = mask_f[:, c * LANES:(c + 1) * LANES]
            out.append(_dot(blk.astype(BF16), upper) + run)
            run = run + jnp.sum(blk, axis=-1, keepdims=True)
        return out

    eq_f = jnp.where(eq, 1.0, 0.0)
    cum_eq = cumsum_blocks(eq_f)
    sel_parts = []
    for c in range(ntok // LANES):
        sl = slice(c * LANES, (c + 1) * LANES)
        sel_parts.append(jnp.where(gt[:, sl] | (eq[:, sl] & (cum_eq[c] <= need)), 1.0, 0.0))
    sel_f = jnp.concatenate(sel_parts, axis=1)
    cpos = cumsum_blocks(sel_f)
    nblk = ntok // LANES
    tab_ref[...] = jnp.zeros_like(tab_ref)
    for c in range(nblk):
        for e in range(N_EXPERTS):
            tab_ref[0, e, c:c + 1, :] = cpos[c][e:e + 1, :]
            tab_ref[1, e, c:c + 1, :] = a_ref[0, e:e + 1, c * LANES:(c + 1) * LANES]
    counts = _dot(sel_f.astype(BF16), ones_ref[...])
    blockend = _dot(counts.astype(BF16), upper)
    prevend = blockend - counts

    lane = lax.broadcasted_iota(jnp.int32, (1, LANES), 1).astype(F32)
    slot = lax.broadcasted_iota(jnp.int32, (cap, 1), 0).astype(F32)
    for e in range(N_EXPERTS):
        be = blockend[e:e + 1, :]
        pe = prevend[e:e + 1, :]
        pick = jnp.where(pe <= slot, jnp.where(slot < be, 1.0, 0.0), 0.0).astype(BF16)
        cnt_blk = sum(_dot(pick, part) for part in _split2(tab_ref[0, e]))
        aff_blk = sum(_dot(pick, part) for part in _split3(tab_ref[1, e]))
        within = jnp.sum(jnp.where(cnt_blk <= slot, 1.0, 0.0), axis=-1, keepdims=True)
        nfull = jnp.sum(jnp.where(be <= slot, 1.0, 0.0), axis=-1, keepdims=True)
        idx_o[0, e, slot_off:slot_off + cap, :] = (nfull * LANES + within + tok_off).astype(jnp.int32)
        val_o[0, e, slot_off:slot_off + cap, :] = jnp.sum(jnp.where(lane == within, aff_blk, 0.0),
                                                           axis=-1, keepdims=True)


def _route_kernel(al_ref, ac_ref, onesl_ref, onesc_ref, idx_o, val_o, tab_ref, *, s, lctx, cap_l, cap_c):
    _route_one(al_ref, onesl_ref, tab_ref, idx_o, val_o, ntok=s, cap=cap_l, tok_off=lctx, slot_off=0)
    _route_one(ac_ref, onesc_ref, tab_ref, idx_o, val_o, ntok=lctx, cap=cap_c, tok_off=0, slot_off=cap_l)


def _route(aff, lctx):
    nb, tt, ne = aff.shape
    s = tt - lctx
    cap_l = CAPACITY_FACTOR * s // ne
    cap_c = CAPACITY_FACTOR * lctx // ne
    capt = cap_l + cap_c
    assert s // LANES <= LANES and cap_l % 8 == 0 and cap_c % 8 == 0
    aff_t = jnp.swapaxes(aff, 1, 2)
    ones = lambda n: jnp.asarray(np.arange(n)[:, None] // LANES == np.arange(LANES)[None, :], BF16)
    out = lambda dt: jax.ShapeDtypeStruct((nb, ne, capt, 1), dt)
    return pl.pallas_call(
        functools.partial(_route_kernel, s=s, lctx=lctx, cap_l=cap_l, cap_c=cap_c),
        grid=(nb,),
        in_specs=[pl.BlockSpec((1, ne, s), lambda b: (b, 0, 0)),
                  pl.BlockSpec((1, ne, lctx), lambda b: (b, 0, 0)),
                  pl.BlockSpec((s, LANES), lambda b: (0, 0)),
                  pl.BlockSpec((lctx, LANES), lambda b: (0, 0))],
        out_specs=[pl.BlockSpec((1, ne, capt, 1), lambda b: (b, 0, 0, 0))] * 2,
        out_shape=[out(jnp.int32), out(F32)],
        scratch_shapes=[pltpu.VMEM((2, ne, LANES, LANES), F32)],
        compiler_params=_cparams(("parallel",), 40),
        name="route",
    )(aff_t[:, :, lctx:], aff_t[:, :, :lctx], ones(s), ones(lctx))


def _gather_kernel(idx_ref, x_ref, o_ref, *, capt):
    def body(c, carry):
        o_ref[0, 0, pl.ds(c, 1), :] = x_ref[0, pl.ds(idx_ref[0, 0, c], 1), :]
        return carry
    lax.fori_loop(0, capt, body, 0, unroll=8)


def _gather(idx_s, hp, capt):
    nb, tt, w = hp.shape
    ne = N_EXPERTS
    return pl.pallas_call(
        functools.partial(_gather_kernel, capt=capt),
        grid=(nb, ne),
        in_specs=[pl.BlockSpec((1, 1, capt), lambda b, e: (b * ne + e, 0, 0), memory_space=pltpu.SMEM),
                  pl.BlockSpec((1, tt, w), lambda b, e: (b, 0, 0))],
        out_specs=pl.BlockSpec((1, 1, capt, w), lambda b, e: (b, e, 0, 0)),
        out_shape=jax.ShapeDtypeStruct((nb, ne, capt, w), jnp.uint32),
        compiler_params=_cparams(("parallel", "arbitrary"), 40),
        name="moe_gather",
    )(idx_s, hp)


def _ffn_kernel(x_ref, w1_ref, w3_ref, w2_ref, val_ref, y_ref, w1b, w3b, w2b):
    @pl.when(pl.program_id(1) == 0)
    def _():
        w1b[...] = w1_ref[0, 0].astype(BF16)
        w3b[...] = w3_ref[0, 0].astype(BF16)
        w2b[...] = w2_ref[0, 0].astype(BF16)

    packed = x_ref[0, 0]
    lo = pltpu.bitcast(packed << 16, F32)
    hi = pltpu.bitcast(packed & jnp.uint32(0xFFFF0000), F32)
    xg = jnp.concatenate([lo, hi], axis=1).astype(BF16)
    a = _dot(xg, w1b[...])
    hid = (a * jax.nn.sigmoid(a) * _dot(xg, w3b[...])).astype(BF16)
    y_ref[0, 0] = _dot(hid, w2b[...]) * val_ref[0, 0]


def _ffn(xg, w1, w3, w2, vals, layer):
    nb, ne, capt, w = xg.shape
    d, ff = w1.shape[2], w1.shape[3]
    return pl.pallas_call(
        _ffn_kernel,
        grid=(ne, nb),
        in_specs=[pl.BlockSpec((1, 1, capt, w), lambda e, b: (b, e, 0, 0)),
                  pl.BlockSpec((1, 1, d, ff), lambda e, b: (layer, e, 0, 0)),
                  pl.BlockSpec((1, 1, d, ff), lambda e, b: (layer, e, 0, 0)),
                  pl.BlockSpec((1, 1, ff, d), lambda e, b: (layer, e, 0, 0)),
                  pl.BlockSpec((1, 1, capt, 1), lambda e, b: (b, e, 0, 0))],
        out_specs=pl.BlockSpec((1, 1, capt, d), lambda e, b: (b, e, 0, 0)),
        out_shape=jax.ShapeDtypeStruct((nb, ne, capt, d), F32),
        scratch_shapes=[pltpu.VMEM((d, ff), BF16), pltpu.VMEM((d, ff), BF16), pltpu.VMEM((ff, d), BF16)],
        compiler_params=_cparams(("parallel", "arbitrary"), 56),
        name="moe_ffn",
    )(xg, w1, w3, w2, vals)


def _scatter_kernel(idx_ref, y_ref, o_ref, *, capt):
    @pl.when(pl.program_id(1) == 0)
    def _():
        o_ref[...] = jnp.zeros_like(o_ref)

    def body(c, carry):
        row = pl.ds(idx_ref[0, 0, c], 1)
        o_ref[0, row, :] = o_ref[0, row, :] + y_ref[0, 0, pl.ds(c, 1), :]
        return carry
    lax.fori_loop(0, capt, body, 0, unroll=4)


def _scatter(idx_s, y, tt):
    nb, ne, capt, d = y.shape
    return pl.pallas_call(
        functools.partial(_scatter_kernel, capt=capt),
        grid=(nb, ne),
        in_specs=[pl.BlockSpec((1, 1, capt), lambda b, e: (b * ne + e, 0, 0), memory_space=pltpu.SMEM),
                  pl.BlockSpec((1, 1, capt, d), lambda b, e: (b, e, 0, 0))],
        out_specs=pl.BlockSpec((1, tt, d), lambda b, e: (b, 0, 0)),
        out_shape=jax.ShapeDtypeStruct((nb, tt, d), F32),
        compiler_params=_cparams(("parallel", "arbitrary"), 56),
        name="moe_scatter",
    )(idx_s, y)


def _post_kernel(x_ref, ml_ref, mod_ref, g_ref, b_ref, o_ref, *, alpha):
    d = D_MODEL
    g2 = mod_ref[0][:, 5 * d:6 * d]
    o_ref[0] = _ln(alpha * x_ref[0] + g2 * ml_ref[0]) * g_ref[...] + b_ref[...]


def _post(x1, ml, mod3, g, b, lctx, alpha):
    nb, tt, d = x1.shape
    lt = lctx // TM
    t0 = lt
    tok = pl.BlockSpec((1, TM, d), lambda b_, t: (b_, t + t0, 0))
    vec = pl.BlockSpec((1, d), lambda b_, t: (0, 0))
    return pl.pallas_call(
        functools.partial(_post_kernel, alpha=alpha),
        grid=(nb, tt // TM - t0),
        in_specs=[tok, tok, pl.BlockSpec((1, 1, 6 * d), lambda b_, t: (jnp.where(t + t0 < lt, nb, b_), 0, 0)),
                  vec, vec],
        out_specs=pl.BlockSpec((1, TM, d), lambda b_, t: (b_, t, 0)),
        out_shape=jax.ShapeDtypeStruct((nb, tt - t0 * TM, d), F32),
        compiler_params=_cparams(("parallel", "parallel")),
        name="moe_post",
    )(x1, ml, mod3, g.reshape(1, d), b.reshape(1, d))


def _rope_tables(s, lctx):
    half = HEAD_DIM // 2
    nf = half // 2
    inv = ROPE_BASE ** (-jnp.arange(nf, dtype=F32) / nf)
    t = jnp.arange(s)
    lane = np.arange(LANES)
    dd = lane % HEAD_DIM
    use_col = jnp.asarray(dd >= half)[None, :]
    pos = jnp.where(use_col, (t % GRID_W)[:, None], (t // GRID_W)[:, None]).astype(F32)
    ang = pos * inv[jnp.asarray(dd % nf)][None, :]
    cos, sin = jnp.cos(ang), jnp.sin(ang)
    first = jnp.asarray((dd % half) < nf)[None, :]
    sa = jnp.where(first, -sin, 0.0)
    sb = jnp.where(first, 0.0, sin)
    pad = lambda a, v: jnp.concatenate([jnp.full((lctx, LANES), v, F32), a], axis=0)
    return pad(cos, 1.0), pad(sa, 0.0), pad(sb, 0.0)


def kernel(x, c, ctx, c_ctx, w_mod, b_mod, w_in, attn_sink, na_rpb, conv_w, conv_b, conv_ln_g, conv_ln_b,
           mlstm_gate_b, w_branch, w_gate, b_gate, w_out, ln1_g, ln1_b, w_router, w_exp_gate, w_exp_up,
           w_exp_down, ln2_g, ln2_b):
    nb, s, d = x.shape
    lctx = ctx.shape[1]
    depth = w_mod.shape[0]
    assert d == D_MODEL and nb + 1 <= 8 and lctx % TM == 0 and s % TM == 0 and s % GRID_W == 0
    alpha = (2.0 * depth) ** 0.25
    tt = lctx + s
    cap_t = CAPACITY_FACTOR * s // N_EXPERTS + CAPACITY_FACTOR * lctx // N_EXPERTS

    c8 = jnp.concatenate([c, c_ctx[None, :], jnp.zeros((8 - nb - 1, d), F32)], axis=0)
    mod_all = _modulation(c8, w_mod, b_mod)
    tabs = _rope_tables(s, lctx)
    xs = jnp.concatenate([ctx, x], axis=1)
    nlat = s // BLK
    assert nlat >= NB_KBLK
    bias_b = _attn_b_bias(na_rpb, nlat)

    for l in range(depth):
        mod3 = mod_all[l].reshape(8, 1, 6 * d)
        w_bf = jnp.pad(w_in[l], ((0, 0), (0, PROJ_PAD - PROJ_W))).astype(BF16)
        if l == 0:
            outs = _inproj(xs, mod3, tabs, w_bf, mlstm_gate_b[l], lctx)
        else:
            xs, *outs = _inproj(None, mod3, tabs, w_bf, mlstm_gate_b[l], lctx, prev=prev)
        (qa, kva, qb, kb, vb, yc0, qd, kd, vd, so, gates) = outs
        ya = _attn_a(qa, kva, attn_sink[l], lctx)
        yb = _attn_b(qb, kb, vb, bias_b, l, lctx)
        yc = _conv(yc0, conv_w[l], conv_b[l], conv_ln_g[l], conv_ln_b[l], lctx)
        hf, hb = _mlstm(qd, kd, vd, gates, lctx)
        wr_hi, wr_lo = _split2(w_router[l])
        wr = jnp.pad(jnp.concatenate([wr_hi, wr_lo], axis=1), ((0, 0), (0, LANES - 2 * N_EXPERTS)))
        x1, hp, aff = _merge(xs, mod3, ya, yb, yc, hf, hb, so, w_gate[l].astype(BF16), b_gate[l],
                             w_branch[l].astype(BF16), w_out[l].astype(BF16), ln1_g[l], ln1_b[l], wr,
                             lctx, alpha)
        idx, vals = _route(aff, lctx)
        idx_s = idx.reshape(nb * N_EXPERTS, 1, cap_t)
        xg = _gather(idx_s, hp, cap_t)
        y = _ffn(xg, w_exp_gate, w_exp_up, w_exp_down, vals, l)
        ml = _scatter(idx_s, y, tt)
        prev = (x1, ml, mod3, ln2_g[l], ln2_b[l], alpha)
    return _post(x1, ml, mod3, ln2_g[depth - 1], ln2_b[depth - 1], lctx, alpha)
```

```python
import functools
import math

import numpy as np
import jax
import jax.numpy as jnp
from jax import lax
from jax.experimental import pallas as pl
from jax.experimental.pallas import tpu as pltpu

F32 = jnp.float32
BF16 = jnp.bfloat16

D_MODEL = 1024
GRID_W = 64
HEAD_DIM = 64
A_HEADS = 8
A_KV_HEADS = 2
A_WINDOW = 128
B_HEADS = 8
NA_ROWS = 8
NA_COLS = 16
C_CH = 512
C_KSIZE = 31
M_HEADS = 4
M_DIM = 128
N_BRANCH = 4
BRANCH_W = 512
N_EXPERTS = 16
EXPERT_FF = 1024
CAPACITY_FACTOR = 2
ROPE_BASE = 10000.0
LN_EPS = 1e-6
NEG_INF = -1e30

BLK = 128
TM = 256
LANES = 128
PROJ_W = 5392
PROJ_PAD = 5504
OFF_A, OFF_AKV, OFF_B, OFF_C, OFF_D, OFF_G = 0, 512, 768, 2304, 3328, 5376
MIB = 1 << 20


def _cparams(sem, vmem_mib=None):
    kw = dict(dimension_semantics=sem)
    if vmem_mib is not None:
        kw["vmem_limit_bytes"] = vmem_mib * MIB
    return pltpu.CompilerParams(**kw)


def _ln(x):
    mu = jnp.mean(x, axis=-1, keepdims=True)
    xc = x - mu
    var = jnp.mean(xc * xc, axis=-1, keepdims=True)
    return xc * lax.rsqrt(var + LN_EPS)


def _dot(a, b):
    return jnp.dot(a, b, preferred_element_type=F32)


def _dot_nt(a, b):
    return lax.dot_general(a, b, (((1,), (1,)), ((), ())), preferred_element_type=F32)


def _split2(x):
    hi = x.astype(BF16)
    lo = (x - hi.astype(F32)).astype(BF16)
    return hi, lo


def _split3(x):
    hi = x.astype(BF16)
    r = x - hi.astype(F32)
    mid = r.astype(BF16)
    lo = (r - mid.astype(F32)).astype(BF16)
    return hi, mid, lo


def _dot3(x, w):
    xh, xl = _split2(x)
    wh, wl = _split2(w)
    return _dot(xh, wh) + (_dot(xh, wl) + _dot(xl, wh))


def _mod_kernel(c_ref, w_ref, b_ref, o_ref):
    c = c_ref[...]
    s = c * jax.nn.sigmoid(c)
    o_ref[0] = _dot3(s, w_ref[0]) + b_ref[0]


def _modulation(c8, w_mod, b_mod):
    depth, d, d6 = w_mod.shape
    nj = d6 // d
    return pl.pallas_call(
        _mod_kernel,
        grid=(depth, nj),
        in_specs=[
            pl.BlockSpec((8, d), lambda l, j: (0, 0)),
            pl.BlockSpec((1, d, d), lambda l, j: (l, 0, j)),
            pl.BlockSpec((1, 1, d), lambda l, j: (l, 0, j)),
        ],
        out_specs=pl.BlockSpec((1, 8, d), lambda l, j: (l, 0, j)),
        out_shape=jax.ShapeDtypeStruct((depth, 8, d6), F32),
        compiler_params=_cparams(("arbitrary", "arbitrary"), 40),
        name="modulation",
    )(c8, w_mod, b_mod.reshape(depth, 1, d6))


def _rope(x, cos, sa, sb):
    parts = []
    for j in range(x.shape[1] // LANES):
        xj = x[:, j * LANES:(j + 1) * LANES]
        parts.append(xj * cos + pltpu.roll(xj, LANES - 16, 1) * sa + pltpu.roll(xj, 16, 1) * sb)
    return parts[0] if len(parts) == 1 else jnp.concatenate(parts, axis=1)


def _inproj_kernel(*refs, alpha):
    d = D_MODEL
    if alpha is None:
        x_ref, mod_ref, cos_ref, sa_ref, sb_ref, w_ref, gb_ref = refs[:7]
        outs = refs[7:]
        x = x_ref[0]
    else:
        x1_ref, ml_ref, pmod_ref, pg_ref, pb_ref, mod_ref, cos_ref, sa_ref, sb_ref, w_ref, gb_ref = refs[:11]
        x_o = refs[11]
        outs = refs[12:]
        x = _ln(alpha * x1_ref[0] + pmod_ref[0][:, 5 * d:6 * d] * ml_ref[0]) * pg_ref[...] + pb_ref[...]
        x_o[0] = x
    qa_o, kva_o, qb_o, kb_o, vb_o, yc_o, qd_o, kd_o, vd_o, so_o, g_o = outs
    mod = mod_ref[0]
    h = (_ln(x) * (1.0 + mod[:, d:2 * d]) + mod[:, 0:d]).astype(BF16)
    cos, sa, sb = cos_ref[...], sa_ref[...], sb_ref[...]
    qscale = HEAD_DIM ** -0.5

    qa = _dot(h, w_ref[:, OFF_A:OFF_AKV])
    qa_o[0] = (_rope(qa, cos, sa, sb) * qscale).astype(BF16)
    kva = _dot(h, w_ref[:, OFF_AKV:OFF_B])
    kva_o[0, :, 0:LANES] = _rope(kva[:, 0:LANES], cos, sa, sb).astype(BF16)
    kva_o[0, :, LANES:2 * LANES] = kva[:, LANES:2 * LANES].astype(BF16)

    qb_o[0] = (_dot(h, w_ref[:, OFF_B:OFF_B + 512]) * qscale).astype(BF16)
    kb_o[0] = _dot(h, w_ref[:, OFF_B + 512:OFF_B + 1024]).astype(BF16)
    vb_o[0] = _dot(h, w_ref[:, OFF_B + 1024:OFF_C]).astype(BF16)

    ua = _dot(h, w_ref[:, OFF_C:OFF_C + C_CH])
    ug = _dot(h, w_ref[:, OFF_C + C_CH:OFF_D])
    yc_o[0] = ua * jax.nn.sigmoid(ug)

    qd_o[0] = _dot(h, w_ref[:, OFF_D:OFF_D + 512]).astype(BF16)
    kd_o[0] = (_dot(h, w_ref[:, OFF_D + 512:OFF_D + 1024]) * (M_DIM ** -0.5)).astype(BF16)
    vd_o[0] = _dot(h, w_ref[:, OFF_D + 1024:OFF_D + 1536]).astype(BF16)
    so_o[0] = jax.nn.sigmoid(_dot(h, w_ref[:, OFF_D + 1536:OFF_G])).astype(BF16)

    g = _dot(h, w_ref[:, OFF_G:PROJ_PAD])
    g_o[0] = g[:, 0:16] + gb_ref[...]


def _mod_index(lt, nb):
    return lambda b, t: (jnp.where(t < lt, nb, b), 0, 0)


def _inproj(x, mod3, tabs, w_bf, gate_b, lctx, prev=None):
    nb, tt, d = (x if prev is None else prev[0]).shape
    nt = tt // TM
    lt = lctx // TM
    tok = lambda n: pl.BlockSpec((1, TM, n), lambda b, t: (b, t, 0))
    tab = pl.BlockSpec((TM, LANES), lambda b, t: (t, 0))
    vec = pl.BlockSpec((1, d), lambda b, t: (0, 0))
    modspec = pl.BlockSpec((1, 1, 6 * d), _mod_index(lt, nb))
    sds = lambda n, dt: jax.ShapeDtypeStruct((nb, tt, n), dt)
    in_specs = [modspec, tab, tab, tab,
                pl.BlockSpec((d, PROJ_PAD), lambda b, t: (0, 0)),
                pl.BlockSpec((1, 16), lambda b, t: (0, 0))]
    args = [mod3, tabs[0], tabs[1], tabs[2], w_bf, gate_b.reshape(1, 16)]
    out_specs = [tok(512), tok(256), tok(512), tok(512), tok(512), tok(512),
                 tok(512), tok(512), tok(512), tok(512), tok(16)]
    out_shape = [sds(512, BF16), sds(256, BF16), sds(512, BF16), sds(512, BF16), sds(512, BF16),
                 sds(512, F32), sds(512, BF16), sds(512, BF16), sds(512, BF16), sds(512, BF16),
                 sds(16, F32)]
    if prev is None:
        in_specs = [tok(d)] + in_specs
        args = [x] + args
        alpha = None
    else:
        x1, ml, pmod3, pg, pb, alpha = prev
        in_specs = [tok(d), tok(d), modspec, vec, vec] + in_specs
        args = [x1, ml, pmod3, pg.reshape(1, d), pb.reshape(1, d)] + args
        out_specs = [tok(d)] + out_specs
        out_shape = [sds(d, F32)] + out_shape
    return pl.pallas_call(
        functools.partial(_inproj_kernel, alpha=alpha),
        grid=(nb, nt),
        in_specs=in_specs,
        out_specs=out_specs,
        out_shape=out_shape,
        compiler_params=_cparams(("parallel", "parallel"), 48),
        name="inproj",
    )(*args)


def _attn_a_kernel(sink_ref, q_ref, kp_ref, kc_ref, kn_ref, kx_ref, mask_ref, o_ref):
    q = q_ref[0]
    kv = jnp.concatenate([kp_ref[0], kc_ref[0], kn_ref[0], kx_ref[0]], axis=0)
    mask = mask_ref[0]
    group = A_HEADS // A_KV_HEADS
    for g in range(A_KV_HEADS):
        k = kv[:, g * HEAD_DIM:(g + 1) * HEAD_DIM]
        v = kv[:, LANES + g * HEAD_DIM:LANES + (g + 1) * HEAD_DIM]
        heads = range(g * group, (g + 1) * group)
        qs = jnp.concatenate([q[:, hh * HEAD_DIM:(hh + 1) * HEAD_DIM] for hh in heads], axis=0)
        s = _dot_nt(qs, k)
        ps, ls = [], []
        for r, hh in enumerate(heads):
            sr = s[r * BLK:(r + 1) * BLK] + mask
            sink = sink_ref[hh]
            m = jnp.maximum(jnp.max(sr, axis=-1, keepdims=True), sink)
            p = jnp.exp(sr - m)
            ls.append(jnp.sum(p, axis=-1, keepdims=True) + jnp.exp(sink - m))
            ps.append(p.astype(BF16))
        o = _dot(jnp.concatenate(ps, axis=0), v)
        for r, hh in enumerate(heads):
            o_ref[0, :, hh * HEAD_DIM:(hh + 1) * HEAD_DIM] = (o[r * BLK:(r + 1) * BLK] / ls[r]).astype(BF16)


def _attn_a_mask(lctx):
    i = np.arange(BLK)[:, None]
    j = np.arange(BLK)[None, :]
    ok_prev = (j >= i)
    ok_next = (j <= i)
    yes = np.ones((BLK, BLK), bool)
    no = np.zeros((BLK, BLK), bool)
    ctx = np.ones((BLK, lctx), bool)
    variants = [
        np.concatenate([ok_prev, yes, ok_next, ctx], 1),
        np.concatenate([no, yes, ok_next, ctx], 1),
        np.concatenate([ok_prev, yes, no, ctx], 1),
        np.concatenate([no, no, no, ctx], 1),
    ]
    return jnp.asarray(np.where(np.stack(variants), 0.0, NEG_INF).astype(np.float32))


def _attn_a(qa, kva, sink, lctx):
    nb, tt, _ = qa.shape
    nblk = tt // BLK
    lb = lctx // BLK
    assert nblk - lb >= 2
    mask = _attn_a_mask(lctx)

    def variant(t):
        return jnp.where(t < lb, 3, jnp.where(t == lb, 1, jnp.where(t == nblk - 1, 2, 0)))

    kvb = lambda f: pl.BlockSpec((1, BLK, 256), lambda b, t: (b, f(t), 0))
    return pl.pallas_call(
        _attn_a_kernel,
        grid=(nb, nblk),
        in_specs=[
            pl.BlockSpec(memory_space=pltpu.SMEM),
            pl.BlockSpec((1, BLK, 512), lambda b, t: (b, t, 0)),
            kvb(lambda t: jnp.maximum(t - 1, 0)),
            kvb(lambda t: t),
            kvb(lambda t: jnp.minimum(t + 1, nblk - 1)),
            pl.BlockSpec((1, lctx, 256), lambda b, t: (b, 0, 0)),
            pl.BlockSpec((1, BLK, 3 * BLK + lctx), lambda b, t: (variant(t), 0, 0)),
        ],
        out_specs=pl.BlockSpec((1, BLK, 512), lambda b, t: (b, t, 0)),
        out_shape=jax.ShapeDtypeStruct((nb, tt, 512), BF16),
        compiler_params=_cparams(("parallel", "parallel")),
        name="attn_window",
    )(sink, qa, kva, kva, kva, kva, mask)


NB_KBLK = 5


def _attn_b_kernel(q_ref, k0, k1, k2, k3, k4, kx, v0, v1, v2, v3, v4, vx, bias_ref, o_ref):
    q = q_ref[0]
    k = jnp.concatenate([k0[0], k1[0], k2[0], k3[0], k4[0], kx[0]], axis=0)
    v = jnp.concatenate([v0[0], v1[0], v2[0], v3[0], v4[0], vx[0]], axis=0)
    nloc = NB_KBLK * BLK
    first = lax.broadcasted_iota(jnp.int32, (1, LANES), 1) < HEAD_DIM
    for pair in range(B_HEADS // 2):
        sl = slice(pair * LANES, (pair + 1) * LANES)
        q2, k2, v2 = q[:, sl], k[:, sl], v[:, sl]
        zero = jnp.zeros_like(q2)
        s2 = _dot_nt(jnp.concatenate([jnp.where(first, q2, zero), jnp.where(first, zero, q2)], axis=0), k2)
        ps, ls = [], []
        for j in range(2):
            s = s2[j * BLK:(j + 1) * BLK]
            s = jnp.concatenate([s[:, :nloc] + bias_ref[0, 0, 2 * pair + j], s[:, nloc:]], axis=1)
            p = jnp.exp(s - jnp.max(s, axis=-1, keepdims=True))
            ls.append(jnp.sum(p, axis=-1, keepdims=True))
            ps.append(p.astype(BF16))
        o2 = _dot(jnp.concatenate(ps, axis=0), v2)
        o_ref[0, :, sl] = jnp.where(first, o2[:BLK] / ls[0], o2[BLK:] / ls[1]).astype(BF16)


def _attn_b_bias(rpb, nlat):
    rows = 2 * nlat
    wh = min(NA_ROWS, rows)
    n = GRID_W
    qrows, krows = BLK // n, NB_KBLK * BLK // n
    nvar = NB_KBLK
    edge = n - NA_COLS
    g = jnp.concatenate([jnp.repeat(rpb[..., :1], edge, axis=-1), rpb.astype(F32),
                         jnp.repeat(rpb[..., -1:], edge + 1, axis=-1)], axis=-1)
    g = jnp.roll(g, -(n - 1), axis=-1)
    toep = jnp.tile(g, (1, 1, 1, n))[..., :n * (2 * n - 1)].reshape(g.shape[:3] + (n, 2 * n - 1))[..., :n]

    reps = [0, 1, 2, nlat - 2, nlat - 1]
    i = np.arange(BLK)
    kk = np.arange(NB_KBLK * BLK)
    ok, dr = [], []
    for dlt, j in enumerate(reps):
        base = j - dlt
        r = 2 * j + i // n
        qc = i % n
        kr = 2 * base + kk // n
        kc = kk % n
        start = np.clip(r - wh // 2, 0, rows - wh)
        row_ok = (kr[None, :] >= start[:, None]) & (kr[None, :] < start[:, None] + wh)
        c0 = np.clip(qc - NA_COLS // 2, 0, n - NA_COLS)
        col_ok = (kc[None, :] >= c0[:, None]) & (kc[None, :] < c0[:, None] + NA_COLS)
        ok.append(row_ok & col_ok)
        dr.append(np.clip(2 * (base - j) + np.arange(krows)[None, :] - np.arange(qrows)[:, None] + NA_ROWS - 1,
                          0, 2 * NA_ROWS - 2))
    depth, nh = rpb.shape[:2]
    bias = jnp.stack([jnp.concatenate([jnp.concatenate([toep[:, :, dr[v][a, b]] for b in range(krows)], axis=-1)
                                       for a in range(qrows)], axis=-2) for v in range(nvar)], axis=1)
    bias = jnp.where(jnp.asarray(np.stack(ok))[None, :, None], bias, NEG_INF)
    return jnp.concatenate([bias, jnp.full((depth, 1, nh, BLK, NB_KBLK * BLK), NEG_INF, F32)], axis=1)


def _attn_b(qb, kb, vb, bias, layer, lctx):
    nb, tt, _ = qb.shape
    nblk = tt // BLK
    lb = lctx // BLK
    nlat = nblk - lb

    def base(t):
        return jnp.clip(t - lb - 2, 0, nlat - NB_KBLK) + lb

    def variant(t):
        return jnp.where(t < lb, 5, t - base(t))

    loc = lambda i: pl.BlockSpec((1, BLK, 512), lambda b, t: (b, base(t) + i, 0))
    ctx = pl.BlockSpec((1, lctx, 512), lambda b, t: (b, 0, 0))
    return pl.pallas_call(
        _attn_b_kernel,
        grid=(nb, nblk),
        in_specs=[pl.BlockSpec((1, BLK, 512), lambda b, t: (b, t, 0))]
        + [loc(i) for i in range(NB_KBLK)] + [ctx]
        + [loc(i) for i in range(NB_KBLK)] + [ctx]
        + [pl.BlockSpec((1, 1, B_HEADS, BLK, NB_KBLK * BLK), lambda b, t: (layer, variant(t), 0, 0, 0))],
        out_specs=pl.BlockSpec((1, BLK, 512), lambda b, t: (b, t, 0)),
        out_shape=jax.ShapeDtypeStruct((nb, tt, 512), BF16),
        compiler_params=_cparams(("parallel", "parallel")),
        name="attn_neighbourhood",
    )(qb, *([kb] * (NB_KBLK + 1)), *([vb] * (NB_KBLK + 1)), bias)


HALO = 16


SUB = 8


NCB = C_CH // LANES


def _conv_kernel(prev_ref, cur_ref, next_ref, w_ref, b_ref, g_ref, bb_ref, o_ref, sh_ref, acc_ref, *, lb, nblk):
    t = pl.program_id(1)
    has_prev = jnp.logical_and(t != 0, t != lb)
    has_next = jnp.logical_and(t != lb - 1, t != nblk - 1)
    prev = jnp.where(has_prev, prev_ref[0], 0.0)
    nxt = jnp.where(has_next, next_ref[0], 0.0)
    cur = cur_ref[0]
    for cb in range(NCB):
        cs = slice(cb * LANES, (cb + 1) * LANES)
        sh_ref[0, cb, 0:HALO, :] = prev[:, cs]
        sh_ref[0, cb, HALO:HALO + BLK, :] = cur[:, cs]
        sh_ref[0, cb, HALO + BLK:2 * HALO + BLK, :] = nxt[:, cs]
    rows = BLK + 2 * HALO - SUB
    pad = C_KSIZE // 2

    def channel_block(cb, carry):
        for r in range(1, SUB):
            sh_ref[r, cb, 0:rows, :] = sh_ref[0, cb, pl.ds(r, rows), :]
        acc = jnp.zeros((BLK // SUB, SUB, LANES), F32)
        for kk in range(C_KSIZE):
            off = HALO - pad + kk
            rows_k = sh_ref[off % SUB, cb, off - off % SUB:off - off % SUB + BLK, :]
            acc = acc + rows_k.reshape(BLK // SUB, SUB, LANES) * w_ref[kk, cb][None]
        acc_ref[cb] = acc.reshape(BLK, LANES)
        return carry

    lax.fori_loop(0, NCB, channel_block, 0)
    y = jnp.concatenate([acc_ref[cb] for cb in range(NCB)], axis=1)
    y = _ln(y + b_ref[...]) * g_ref[...] + bb_ref[...]
    o_ref[0] = (y * jax.nn.sigmoid(y)).astype(BF16)


def _conv(yc, w, b, g, bb, lctx):
    nb, tt, ch = yc.shape
    nblk = tt // BLK
    lb = lctx // BLK
    per = BLK // HALO
    vec = pl.BlockSpec((1, ch), lambda b_, t: (0, 0))
    return pl.pallas_call(
        functools.partial(_conv_kernel, lb=lb, nblk=nblk),
        grid=(nb, nblk),
        in_specs=[
            pl.BlockSpec((1, HALO, ch), lambda b_, t: (b_, jnp.maximum(t * per - 1, 0), 0)),
            pl.BlockSpec((1, BLK, ch), lambda b_, t: (b_, t, 0)),
            pl.BlockSpec((1, HALO, ch), lambda b_, t: (b_, jnp.minimum((t + 1) * per, nblk * per - 1), 0)),
            pl.BlockSpec((C_KSIZE, NCB, SUB, LANES), lambda b_, t: (0, 0, 0, 0)),
            vec, vec, vec,
        ],
        out_specs=pl.BlockSpec((1, BLK, ch), lambda b_, t: (b_, t, 0)),
        out_shape=jax.ShapeDtypeStruct((nb, tt, ch), BF16),
        scratch_shapes=[pltpu.VMEM((SUB, NCB, BLK + 2 * HALO, LANES), F32), pltpu.VMEM((NCB, BLK, LANES), F32)],
        compiler_params=_cparams(("parallel", "parallel")),
        name="conformer_conv",
    )(yc, yc, yc, jnp.broadcast_to(w.reshape(C_KSIZE, NCB, 1, LANES), (C_KSIZE, NCB, SUB, LANES)),
      b.reshape(1, ch), g.reshape(1, ch),
      bb.reshape(1, ch))


def _mlstm_kernel(qf_ref, kf_ref, vf_ref, gcf_ref, grf_ref, qb_ref, kb_ref, vb_ref, gcb_ref, grb_ref,
                  hf_ref, hb_ref, c_st, n_st, m_st, *, mb):
    @pl.when(pl.program_id(1) == 0)
    def _():
        c_st[...] = jnp.zeros_like(c_st)
        n_st[...] = jnp.zeros_like(n_st)
        m_st[...] = jnp.zeros_like(m_st)

    rr = lax.broadcasted_iota(jnp.int32, (BLK, BLK), 0)
    cc = lax.broadcasted_iota(jnp.int32, (BLK, BLK), 1)
    ch = []
    dirs = ((qf_ref, kf_ref, vf_ref, gcf_ref, grf_ref, hf_ref), (qb_ref, kb_ref, vb_ref, gcb_ref, grb_ref, hb_ref))
    for bb, dirn in [(bb, dirn) for bb in range(mb) for dirn in range(2)]:
        q_ref, k_ref, v_ref, gc_ref, gr_ref, h_ref = dirs[dirn]
        before = (rr >= cc) if dirn == 0 else (rr <= cc)
        bmat = jnp.where(before, 1.0, 0.0).astype(BF16)
        gc = gc_ref[0, bb]
        gr = gr_ref[0, bb]
        lf_c = jax.nn.log_sigmoid(gc[:, M_HEADS:2 * M_HEADS])
        lf_r = jax.nn.log_sigmoid(gr[M_HEADS:2 * M_HEADS, :])
        fc = sum(_dot(bmat, part) for part in _split3(lf_c))
        fr = sum(_dot_nt(part, bmat) for part in _split3(lf_r))
        f_tot = jnp.sum(lf_r, axis=-1, keepdims=True)
        for hh in range(M_HEADS):
            sl = slice(hh * M_DIM, (hh + 1) * M_DIM)
            st = (bb * 2 + dirn) * M_HEADS + hh
            ch.append(dict(before=before, sl=sl, st=st, h_ref=h_ref, bb=bb,
                           q=q_ref[bb, :, sl], k=k_ref[bb, :, sl], v=v_ref[bb, :, sl],
                           f_c=fc[:, hh:hh + 1], f_r=fr[hh:hh + 1, :], f_tot=f_tot[hh:hh + 1, :],
                           i_c=gc[:, hh:hh + 1], i_r=gr[hh:hh + 1, :],
                           c_old=c_st[st], n_old=n_st[st], m_old=m_st[st][:, 0:1]))

    for c in ch:
        c["qk"] = _dot_nt(c["q"], c["k"])
        c["qc"] = _dot(c["q"], c["c_old"].astype(BF16))
    for c in ch:
        c["a"] = c["f_c"] + c["m_old"]
        c["logw"] = jnp.where(c["before"], c["f_c"] - c["f_r"] + c["i_r"], -jnp.inf)
        c["mt"] = jnp.maximum(c["a"], jnp.max(c["logw"], axis=-1, keepdims=True))
    for c in ch:
        g_r = c["f_tot"] - c["f_r"] + c["i_r"]
        c["m_new"] = jnp.maximum(c["f_tot"] + c["m_old"], jnp.max(g_r, axis=-1, keepdims=True))
        c["decay"] = jnp.exp(c["f_tot"] + c["m_old"] - c["m_new"])
        c["kw"] = c["k"].astype(F32) * jnp.exp(c["f_tot"] - c["f_c"] + c["i_c"] - c["m_new"])
    for c in ch:
        c["s"] = c["qk"] * jnp.exp(c["logw"] - c["mt"])
        c["w_inter"] = jnp.exp(c["a"] - c["mt"])
    for c in ch:
        c["sv"] = _dot(c["s"].astype(BF16), c["v"])
        c["kv"] = lax.dot_general(c["kw"].astype(BF16), c["v"], (((0,), (0,)), ((), ())), preferred_element_type=F32)
    for c in ch:
        num = c["w_inter"] * c["qc"] + c["sv"]
        den = (c["w_inter"] * jnp.sum(c["q"].astype(F32) * c["n_old"], axis=-1, keepdims=True)
               + jnp.sum(c["s"], axis=-1, keepdims=True))
        hout = num / jnp.maximum(jnp.abs(den), jnp.exp(-c["mt"]))
        c["h_ref"][c["bb"], :, c["sl"]] = hout.astype(BF16)
    for c in ch:
        st = c["st"]
        c_st[st] = c["decay"] * c["c_old"] + c["kv"]
        n_st[st] = c["decay"] * c["n_old"] + jnp.sum(c["kw"], axis=0, keepdims=True)
        m_st[st] = jnp.broadcast_to(c["m_new"], (1, LANES))


def _mlstm(qd, kd, vd, gates, lctx):
    nb, tt, _ = qd.shape
    nblk = tt // BLK
    lb = lctx // BLK
    gcol = jnp.stack([gates[..., 0:8], gates[..., 8:16]])
    grow = jnp.swapaxes(gcol, 2, 3)

    def bwd(i):
        return jnp.where(i < lb, lb - 1 - i, nblk - 1 + lb - i)

    fwd = lambda i: i
    mb = max(m for m in (4, 2, 1) if nb % m == 0)
    tok = lambda f: pl.BlockSpec((mb, BLK, 512), lambda b, i: (b, f(i), 0))
    gcs = lambda d, f: pl.BlockSpec((1, mb, BLK, 8), lambda b, i: (d, b, f(i), 0))
    grs = lambda d, f: pl.BlockSpec((1, mb, 8, BLK), lambda b, i: (d, b, 0, f(i)))
    out = jax.ShapeDtypeStruct((nb, tt, 512), BF16)
    nst = 2 * mb * M_HEADS
    return pl.pallas_call(
        functools.partial(_mlstm_kernel, mb=mb),
        grid=(nb // mb, nblk),
        in_specs=[tok(fwd), tok(fwd), tok(fwd), gcs(0, fwd), grs(0, fwd),
                  tok(bwd), tok(bwd), tok(bwd), gcs(1, bwd), grs(1, bwd)],
        out_specs=[tok(fwd), tok(bwd)],
        out_shape=[out, out],
        scratch_shapes=[pltpu.VMEM((nst, M_DIM, M_DIM), F32),
                        pltpu.VMEM((nst, 1, M_DIM), F32),
                        pltpu.VMEM((nst, 1, LANES), F32)],
        compiler_params=_cparams(("parallel", "arbitrary")),
        name="mlstm",
    )(qd, kd, vd, gcol, grow, qd, kd, vd, gcol, grow)


def _merge_kernel(x_ref, mod_ref, ya_ref, yb_ref, yc_ref, hf_ref, hb_ref, so_ref,
                  wg_ref, bg_ref, wbr_ref, wo_ref, g1_ref, b1_ref, wr_ref,
                  x1_o, hp_o, aff_o, *, alpha):
    d = D_MODEL
    x = x_ref[0]
    mod = mod_ref[0]
    h = (_ln(x) * (1.0 + mod[:, d:2 * d]) + mod[:, 0:d]).astype(BF16)
    yd = (so_ref[0].astype(F32) * (hf_ref[0].astype(F32) + hb_ref[0].astype(F32))).astype(BF16)
    ys = (ya_ref[0], yb_ref[0], yc_ref[0], yd)
    z = None
    for i in range(N_BRANCH):
        gate = jax.nn.sigmoid(_dot(h, wg_ref[:, i * d:(i + 1) * d]) + bg_ref[:, i * d:(i + 1) * d])
        term = gate * _dot(ys[i], wbr_ref[i])
        z = term if z is None else z + term
    y = _dot(z.astype(BF16), wo_ref[...])
    x1 = _ln(alpha * x + mod[:, 2 * d:3 * d] * y) * g1_ref[...] + b1_ref[...]
    x1_o[0] = x1

    h2 = _ln(x1) * (1.0 + mod[:, 4 * d:5 * d]) + mod[:, 3 * d:4 * d]
    hb16 = h2.astype(BF16)
    bits = pltpu.bitcast(hb16.astype(F32), jnp.uint32)
    hp_o[0] = (bits[:, d // 2:] & jnp.uint32(0xFFFF0000)) | (bits[:, :d // 2] >> 16)

    h2_hi, h2_lo = _split2(h2)
    r_hi = _dot(h2_hi, wr_ref[...])
    r_lo = _dot(h2_lo, wr_ref[...])
    logits = r_hi[:, 0:N_EXPERTS] + (r_hi[:, N_EXPERTS:2 * N_EXPERTS] + r_lo[:, 0:N_EXPERTS])
    e = jnp.exp(logits - jnp.max(logits, axis=-1, keepdims=True))
    aff_o[0] = e / jnp.sum(e, axis=-1, keepdims=True)


def _merge(x, mod3, ya, yb, yc, hf, hb, so, wg, bg, wbr, wo, g1, b1, wr, lctx, alpha):
    nb, tt, d = x.shape
    nt = tt // TM
    lt = lctx // TM
    tok = lambda n: pl.BlockSpec((1, TM, n), lambda b, t: (b, t, 0))
    const = lambda shape: pl.BlockSpec(shape, lambda b, t: (0,) * len(shape))
    return pl.pallas_call(
        functools.partial(_merge_kernel, alpha=alpha),
        grid=(nb, nt),
        in_specs=[
            tok(d),
            pl.BlockSpec((1, 1, 6 * d), _mod_index(lt, nb)),
            tok(512), tok(512), tok(512), tok(512), tok(512), tok(512),
            const((d, N_BRANCH * d)), const((1, N_BRANCH * d)), const((N_BRANCH, BRANCH_W, d)),
            const((d, d)), const((1, d)), const((1, d)), const((d, LANES)),
        ],
        out_specs=[tok(d), tok(d // 2), tok(N_EXPERTS)],
        out_shape=[jax.ShapeDtypeStruct((nb, tt, d), F32),
                   jax.ShapeDtypeStruct((nb, tt, d // 2), jnp.uint32),
                   jax.ShapeDtypeStruct((nb, tt, N_EXPERTS), F32)],
        compiler_params=_cparams(("parallel", "parallel"), 56),
        name="merge",
    )(x, mod3, ya, yb, yc, hf, hb, so, wg, bg.reshape(1, -1), wbr, wo,
      g1.reshape(1, d), b1.reshape(1, d), wr)


def _route_one(a_ref, ones_ref, tab_ref, idx_o, val_o, *, ntok, cap, tok_off, slot_off):
    a = a_ref[0]
    bits = pltpu.bitcast(a, jnp.int32)
    thr = jnp.zeros((N_EXPERTS, 1), jnp.int32)
    for bit in range(30, -1, -1):
        cand = thr | jnp.int32(1 << bit)
        cnt = jnp.sum(jnp.where(bits >= cand, 1.0, 0.0), axis=-1, keepdims=True)
        thr = jnp.where(cnt >= cap, cand, thr)
    gt = bits > thr
    eq = bits == thr
    need = cap - jnp.sum(jnp.where(gt, 1.0, 0.0), axis=-1, keepdims=True)

    rr = lax.broadcasted_iota(jnp.int32, (LANES, LANES), 0)
    cc = lax.broadcasted_iota(jnp.int32, (LANES, LANES), 1)
    upper = jnp.where(rr <= cc, 1.0, 0.0).astype(BF16)

    def cumsum_blocks(mask_f):
        run = jnp.zeros((N_EXPERTS, 1), F32)
        out = []
        for c in range(ntok // LANES):
            blk = mask_f[:, c * LANES:(c + 1) * LANES]
            out.append(_dot(blk.astype(BF16), upper) + run)
            run = run + jnp.sum(blk, axis=-1, keepdims=True)
        return out

    eq_f = jnp.where(eq, 1.0, 0.0)
    cum_eq = cumsum_blocks(eq_f)
    sel_parts = []
    for c in range(ntok // LANES):
        sl = slice(c * LANES, (c + 1) * LANES)
        sel_parts.append(jnp.where(gt[:, sl] | (eq[:, sl] & (cum_eq[c] <= need)), 1.0, 0.0))
    sel_f = jnp.concatenate(sel_parts, axis=1)
    cpos = cumsum_blocks(sel_f)
    nblk = ntok // LANES
    tab_ref[...] = jnp.zeros_like(tab_ref)
    for c in range(nblk):
        for e in range(N_EXPERTS):
            tab_ref[0, e, c:c + 1, :] = cpos[c][e:e + 1, :]
            tab_ref[1, e, c:c + 1, :] = a_ref[0, e:e + 1, c * LANES:(c + 1) * LANES]
    counts = _dot(sel_f.astype(BF16), ones_ref[...])
    blockend = _dot(counts.astype(BF16), upper)
    prevend = blockend - counts

    lane = lax.broadcasted_iota(jnp.int32, (1, LANES), 1).astype(F32)
    slot = lax.broadcasted_iota(jnp.int32, (cap, 1), 0).astype(F32)
    for e in range(N_EXPERTS):
        be = blockend[e:e + 1, :]
        pe = prevend[e:e + 1, :]
        pick = jnp.where(pe <= slot, jnp.where(slot < be, 1.0, 0.0), 0.0).astype(BF16)
        cnt_blk = sum(_dot(pick, part) for part in _split2(tab_ref[0, e]))
        aff_blk = sum(_dot(pick, part) for part in _split3(tab_ref[1, e]))
        within = jnp.sum(jnp.where(cnt_blk <= slot, 1.0, 0.0), axis=-1, keepdims=True)
        nfull = jnp.sum(jnp.where(be <= slot, 1.0, 0.0), axis=-1, keepdims=True)
        idx_o[0, e, slot_off:slot_off + cap, :] = (nfull * LANES + within + tok_off).astype(jnp.int32)
        val_o[0, e, slot_off:slot_off + cap, :] = jnp.sum(jnp.where(lane == within, aff_blk, 0.0),
                                                           axis=-1, keepdims=True)


def _route_kernel(al_ref, ac_ref, onesl_ref, onesc_ref, idx_o, val_o, tab_ref, *, s, lctx, cap_l, cap_c):
    _route_one(al_ref, onesl_ref, tab_ref, idx_o, val_o, ntok=s, cap=cap_l, tok_off=lctx, slot_off=0)
    _route_one(ac_ref, onesc_ref, tab_ref, idx_o, val_o, ntok=lctx, cap=cap_c, tok_off=0, slot_off=cap_l)


def _route(aff, lctx):
    nb, tt, ne = aff.shape
    s = tt - lctx
    cap_l = CAPACITY_FACTOR * s // ne
    cap_c = CAPACITY_FACTOR * lctx // ne
    capt = cap_l + cap_c
    assert s // LANES <= LANES and cap_l % 8 == 0 and cap_c % 8 == 0
    aff_t = jnp.swapaxes(aff, 1, 2)
    ones = lambda n: jnp.asarray(np.arange(n)[:, None] // LANES == np.arange(LANES)[None, :], BF16)
    out = lambda dt: jax.ShapeDtypeStruct((nb, ne, capt, 1), dt)
    return pl.pallas_call(
        functools.partial(_route_kernel, s=s, lctx=lctx, cap_l=cap_l, cap_c=cap_c),
        grid=(nb,),
        in_specs=[pl.BlockSpec((1, ne, s), lambda b: (b, 0, 0)),
                  pl.BlockSpec((1, ne, lctx), lambda b: (b, 0, 0)),
                  pl.BlockSpec((s, LANES), lambda b: (0, 0)),
                  pl.BlockSpec((lctx, LANES), lambda b: (0, 0))],
        out_specs=[pl.BlockSpec((1, ne, capt, 1), lambda b: (b, 0, 0, 0))] * 2,
        out_shape=[out(jnp.int32), out(F32)],
        scratch_shapes=[pltpu.VMEM((2, ne, LANES, LANES), F32)],
        compiler_params=_cparams(("parallel",), 40),
        name="route",
    )(aff_t[:, :, lctx:], aff_t[:, :, :lctx], ones(s), ones(lctx))


def _gather_kernel(idx_ref, x_ref, o_ref, *, capt):
    def body(c, carry):
        o_ref[0, 0, pl.ds(c, 1), :] = x_ref[0, pl.ds(idx_ref[0, 0, c], 1), :]
        return carry
    lax.fori_loop(0, capt, body, 0, unroll=8)


def _gather(idx_s, hp, capt):
    nb, tt, w = hp.shape
    ne = N_EXPERTS
    return pl.pallas_call(
        functools.partial(_gather_kernel, capt=capt),
        grid=(nb, ne),
        in_specs=[pl.BlockSpec((1, 1, capt), lambda b, e: (b * ne + e, 0, 0), memory_space=pltpu.SMEM),
                  pl.BlockSpec((1, tt, w), lambda b, e: (b, 0, 0))],
        out_specs=pl.BlockSpec((1, 1, capt, w), lambda b, e: (b, e, 0, 0)),
        out_shape=jax.ShapeDtypeStruct((nb, ne, capt, w), jnp.uint32),
        compiler_params=_cparams(("parallel", "arbitrary"), 40),
        name="moe_gather",
    )(idx_s, hp)


def _ffn_kernel(x_ref, w1_ref, w3_ref, w2_ref, val_ref, y_ref, w1b, w3b, w2b):
    @pl.when(pl.program_id(1) == 0)
    def _():
        w1b[...] = w1_ref[0, 0].astype(BF16)
        w3b[...] = w3_ref[0, 0].astype(BF16)
        w2b[...] = w2_ref[0, 0].astype(BF16)

    packed = x_ref[0, 0]
    lo = pltpu.bitcast(packed << 16, F32)
    hi = pltpu.bitcast(packed & jnp.uint32(0xFFFF0000), F32)
    xg = jnp.concatenate([lo, hi], axis=1).astype(BF16)
    a = _dot(xg, w1b[...])
    hid = (a * jax.nn.sigmoid(a) * _dot(xg, w3b[...])).astype(BF16)
    y_ref[0, 0] = _dot(hid, w2b[...]) * val_ref[0, 0]


def _ffn(xg, w1, w3, w2, vals, layer):
    nb, ne, capt, w = xg.shape
    d, ff = w1.shape[2], w1.shape[3]
    return pl.pallas_call(
        _ffn_kernel,
        grid=(ne, nb),
        in_specs=[pl.BlockSpec((1, 1, capt, w), lambda e, b: (b, e, 0, 0)),
                  pl.BlockSpec((1, 1, d, ff), lambda e, b: (layer, e, 0, 0)),
                  pl.BlockSpec((1, 1, d, ff), lambda e, b: (layer, e, 0, 0)),
                  pl.BlockSpec((1, 1, ff, d), lambda e, b: (layer, e, 0, 0)),
                  pl.BlockSpec((1, 1, capt, 1), lambda e, b: (b, e, 0, 0))],
        out_specs=pl.BlockSpec((1, 1, capt, d), lambda e, b: (b, e, 0, 0)),
        out_shape=jax.ShapeDtypeStruct((nb, ne, capt, d), F32),
        scratch_shapes=[pltpu.VMEM((d, ff), BF16), pltpu.VMEM((d, ff), BF16), pltpu.VMEM((ff, d), BF16)],
        compiler_params=_cparams(("parallel", "arbitrary"), 56),
        name="moe_ffn",
    )(xg, w1, w3, w2, vals)


SCATTER_GROUP = 4


def _scatter_kernel(idx_ref, y_ref, o_ref, *, capt):
    @pl.when(pl.program_id(1) == 0)
    def _():
        o_ref[...] = jnp.zeros_like(o_ref)

    def body(i, carry):
        c0 = i * SCATTER_GROUP
        rows = [pl.ds(idx_ref[0, 0, c0 + j], 1) for j in range(SCATTER_GROUP)]
        acc = [o_ref[0, r, :] for r in rows]
        new = [a + y_ref[0, 0, pl.ds(c0 + j, 1), :] for j, a in enumerate(acc)]
        for r, v in zip(rows, new):
            o_ref[0, r, :] = v
        return carry
    lax.fori_loop(0, capt // SCATTER_GROUP, body, 0)


def _scatter(idx_s, y, tt):
    nb, ne, capt, d = y.shape
    assert capt % SCATTER_GROUP == 0
    return pl.pallas_call(
        functools.partial(_scatter_kernel, capt=capt),
        grid=(nb, ne),
        in_specs=[pl.BlockSpec((1, 1, capt), lambda b, e: (b * ne + e, 0, 0), memory_space=pltpu.SMEM),
                  pl.BlockSpec((1, 1, capt, d), lambda b, e: (b, e, 0, 0))],
        out_specs=pl.BlockSpec((1, tt, d), lambda b, e: (b, 0, 0)),
        out_shape=jax.ShapeDtypeStruct((nb, tt, d), F32),
        compiler_params=_cparams(("parallel", "arbitrary"), 56),
        name="moe_scatter",
    )(idx_s, y)


def _post_kernel(x_ref, ml_ref, mod_ref, g_ref, b_ref, o_ref, *, alpha):
    d = D_MODEL
    g2 = mod_ref[0][:, 5 * d:6 * d]
    o_ref[0] = _ln(alpha * x_ref[0] + g2 * ml_ref[0]) * g_ref[...] + b_ref[...]


def _post(x1, ml, mod3, g, b, lctx, alpha):
    nb, tt, d = x1.shape
    lt = lctx // TM
    t0 = lt
    tok = pl.BlockSpec((1, TM, d), lambda b_, t: (b_, t + t0, 0))
    vec = pl.BlockSpec((1, d), lambda b_, t: (0, 0))
    return pl.pallas_call(
        functools.partial(_post_kernel, alpha=alpha),
        grid=(nb, tt // TM - t0),
        in_specs=[tok, tok, pl.BlockSpec((1, 1, 6 * d), lambda b_, t: (jnp.where(t + t0 < lt, nb, b_), 0, 0)),
                  vec, vec],
        out_specs=pl.BlockSpec((1, TM, d), lambda b_, t: (b_, t, 0)),
        out_shape=jax.ShapeDtypeStruct((nb, tt - t0 * TM, d), F32),
        compiler_params=_cparams(("parallel", "parallel")),
        name="moe_post",
    )(x1, ml, mod3, g.reshape(1, d), b.reshape(1, d))


def _rope_tables(s, lctx):
    half = HEAD_DIM // 2
    nf = half // 2
    inv = ROPE_BASE ** (-jnp.arange(nf, dtype=F32) / nf)
    t = jnp.arange(s)
    lane = np.arange(LANES)
    dd = lane % HEAD_DIM
    use_col = jnp.asarray(dd >= half)[None, :]
    pos = jnp.where(use_col, (t % GRID_W)[:, None], (t // GRID_W)[:, None]).astype(F32)
    ang = pos * inv[jnp.asarray(dd % nf)][None, :]
    cos, sin = jnp.cos(ang), jnp.sin(ang)
    first = jnp.asarray((dd % half) < nf)[None, :]
    sa = jnp.where(first, -sin, 0.0)
    sb = jnp.where(first, 0.0, sin)
    pad = lambda a, v: jnp.concatenate([jnp.full((lctx, LANES), v, F32), a], axis=0)
    return pad(cos, 1.0), pad(sa, 0.0), pad(sb, 0.0)


def kernel(x, c, ctx, c_ctx, w_mod, b_mod, w_in, attn_sink, na_rpb, conv_w, conv_b, conv_ln_g, conv_ln_b,
           mlstm_gate_b, w_branch, w_gate, b_gate, w_out, ln1_g, ln1_b, w_router, w_exp_gate, w_exp_up,
           w_exp_down, ln2_g, ln2_b):
    nb, s, d = x.shape
    lctx = ctx.shape[1]
    depth = w_mod.shape[0]
    assert d == D_MODEL and nb + 1 <= 8 and lctx % TM == 0 and s % TM == 0 and s % GRID_W == 0
    alpha = (2.0 * depth) ** 0.25
    tt = lctx + s
    cap_t = CAPACITY_FACTOR * s // N_EXPERTS + CAPACITY_FACTOR * lctx // N_EXPERTS

    c8 = jnp.concatenate([c, c_ctx[None, :], jnp.zeros((8 - nb - 1, d), F32)], axis=0)
    mod_all = _modulation(c8, w_mod, b_mod)
    tabs = _rope_tables(s, lctx)
    xs = jnp.concatenate([ctx, x], axis=1)
    nlat = s // BLK
    assert nlat >= NB_KBLK
    bias_b = _attn_b_bias(na_rpb, nlat)

    for l in range(depth):
        mod3 = mod_all[l].reshape(8, 1, 6 * d)
        w_bf = jnp.pad(w_in[l], ((0, 0), (0, PROJ_PAD - PROJ_W))).astype(BF16)
        if l == 0:
            outs = _inproj(xs, mod3, tabs, w_bf, mlstm_gate_b[l], lctx)
        else:
            xs, *outs = _inproj(None, mod3, tabs, w_bf, mlstm_gate_b[l], lctx, prev=prev)
        (qa, kva, qb, kb, vb, yc0, qd, kd, vd, so, gates) = outs
        ya = _attn_a(qa, kva, attn_sink[l], lctx)
        yb = _attn_b(qb, kb, vb, bias_b, l, lctx)
        yc = _conv(yc0, conv_w[l], conv_b[l], conv_ln_g[l], conv_ln_b[l], lctx)
        hf, hb = _mlstm(qd, kd, vd, gates, lctx)
        wr_hi, wr_lo = _split2(w_router[l])
        wr = jnp.pad(jnp.concatenate([wr_hi, wr_lo], axis=1), ((0, 0), (0, LANES - 2 * N_EXPERTS)))
        x1, hp, aff = _merge(xs, mod3, ya, yb, yc, hf, hb, so, w_gate[l].astype(BF16), b_gate[l],
                             w_branch[l].astype(BF16), w_out[l].astype(BF16), ln1_g[l], ln1_b[l], wr,
                             lctx, alpha)
        idx, vals = _route(aff, lctx)
        idx_s = idx.reshape(nb * N_EXPERTS, 1, cap_t)
        xg = _gather(idx_s, hp, cap_t)
        y = _ffn(xg, w_exp_gate, w_exp_up, w_exp_down, vals, l)
        ml = _scatter(idx_s, y, tt)
        prev = (x1, ml, mod3, ln2_g[l], ln2_b[l], alpha)
    return _post(x1, ml, mod3, ln2_g[depth - 1], ln2_b[depth - 1], lctx, alpha)
```

```python
import functools
import math

import numpy as np
import jax
import jax.numpy as jnp
from jax import lax
from jax.experimental import pallas as pl
from jax.experimental.pallas import tpu as pltpu

F32 = jnp.float32
BF16 = jnp.bfloat16

D_MODEL = 1024
GRID_W = 64
HEAD_DIM = 64
A_HEADS = 8
A_KV_HEADS = 2
A_WINDOW = 128
B_HEADS = 8
NA_ROWS = 8
NA_COLS = 16
C_CH = 512
C_KSIZE = 31
M_HEADS = 4
M_DIM = 128
N_BRANCH = 4
BRANCH_W = 512
N_EXPERTS = 16
EXPERT_FF = 1024
CAPACITY_FACTOR = 2
ROPE_BASE = 10000.0
LN_EPS = 1e-6
NEG_INF = -1e30

BLK = 128
TM = 256
LANES = 128
PROJ_W = 5392
PROJ_PAD = 5504
OFF_A, OFF_AKV, OFF_B, OFF_C, OFF_D, OFF_G = 0, 512, 768, 2304, 3328, 5376
MIB = 1 << 20


def _cparams(sem, vmem_mib=None):
    kw = dict(dimension_semantics=sem)
    if vmem_mib is not None:
        kw["vmem_limit_bytes"] = vmem_mib * MIB
    return pltpu.CompilerParams(**kw)


def _ln(x):
    mu = jnp.mean(x, axis=-1, keepdims=True)
    xc = x - mu
    var = jnp.mean(xc * xc, axis=-1, keepdims=True)
    return xc * lax.rsqrt(var + LN_EPS)


def _dot(a, b):
    return jnp.dot(a, b, preferred_element_type=F32)


def _dot_nt(a, b):
    return lax.dot_general(a, b, (((1,), (1,)), ((), ())), preferred_element_type=F32)


def _split2(x):
    hi = x.astype(BF16)
    lo = (x - hi.astype(F32)).astype(BF16)
    return hi, lo


def _split3(x):
    hi = x.astype(BF16)
    r = x - hi.astype(F32)
    mid = r.astype(BF16)
    lo = (r - mid.astype(F32)).astype(BF16)
    return hi, mid, lo


def _dot3(x, w):
    xh, xl = _split2(x)
    wh, wl = _split2(w)
    return _dot(xh, wh) + (_dot(xh, wl) + _dot(xl, wh))


def _mod_kernel(c_ref, w_ref, b_ref, o_ref):
    c = c_ref[...]
    s = c * jax.nn.sigmoid(c)
    o_ref[0] = _dot3(s, w_ref[0]) + b_ref[0]


def _modulation(c8, w_mod, b_mod):
    depth, d, d6 = w_mod.shape
    nj = d6 // d
    return pl.pallas_call(
        _mod_kernel,
        grid=(depth, nj),
        in_specs=[
            pl.BlockSpec((8, d), lambda l, j: (0, 0)),
            pl.BlockSpec((1, d, d), lambda l, j: (l, 0, j)),
            pl.BlockSpec((1, 1, d), lambda l, j: (l, 0, j)),
        ],
        out_specs=pl.BlockSpec((1, 8, d), lambda l, j: (l, 0, j)),
        out_shape=jax.ShapeDtypeStruct((depth, 8, d6), F32),
        compiler_params=_cparams(("arbitrary", "arbitrary"), 40),
        name="modulation",
    )(c8, w_mod, b_mod.reshape(depth, 1, d6))


def _rope(x, cos, sa, sb):
    parts = []
    for j in range(x.shape[1] // LANES):
        xj = x[:, j * LANES:(j + 1) * LANES]
        parts.append(xj * cos + pltpu.roll(xj, LANES - 16, 1) * sa + pltpu.roll(xj, 16, 1) * sb)
    return parts[0] if len(parts) == 1 else jnp.concatenate(parts, axis=1)


def _inproj_kernel(*refs, alpha):
    d = D_MODEL
    if alpha is None:
        x_ref, mod_ref, cos_ref, sa_ref, sb_ref, w3_ref, gb_ref = refs[:7]
        outs = refs[7:]
        x = x_ref[0]
    else:
        x1_ref, ml_ref, pmod_ref, pg_ref, pb_ref, mod_ref, cos_ref, sa_ref, sb_ref, w3_ref, gb_ref = refs[:11]
        x_o = refs[11]
        outs = refs[12:]
        x = _ln(alpha * x1_ref[0] + pmod_ref[0][:, 5 * d:6 * d] * ml_ref[0]) * pg_ref[...] + pb_ref[...]
        x_o[0] = x
    qa_o, kva_o, qb_o, kb_o, vb_o, yc_o, qd_o, kd_o, vd_o, so_o, g_o = outs
    w_ref = w3_ref.at[0]
    mod = mod_ref[0]
    h = (_ln(x) * (1.0 + mod[:, d:2 * d]) + mod[:, 0:d]).astype(BF16)
    cos, sa, sb = cos_ref[...], sa_ref[...], sb_ref[...]
    qscale = HEAD_DIM ** -0.5

    qa = _dot(h, w_ref[:, OFF_A:OFF_AKV])
    qa_o[0] = (_rope(qa, cos, sa, sb) * qscale).astype(BF16)
    kva = _dot(h, w_ref[:, OFF_AKV:OFF_B])
    kva_o[0, :, 0:LANES] = _rope(kva[:, 0:LANES], cos, sa, sb).astype(BF16)
    kva_o[0, :, LANES:2 * LANES] = kva[:, LANES:2 * LANES].astype(BF16)

    qb_o[0] = (_dot(h, w_ref[:, OFF_B:OFF_B + 512]) * qscale).astype(BF16)
    kb_o[0] = _dot(h, w_ref[:, OFF_B + 512:OFF_B + 1024]).astype(BF16)
    vb_o[0] = _dot(h, w_ref[:, OFF_B + 1024:OFF_C]).astype(BF16)

    ua = _dot(h, w_ref[:, OFF_C:OFF_C + C_CH])
    ug = _dot(h, w_ref[:, OFF_C + C_CH:OFF_D])
    yc_o[0] = ua * jax.nn.sigmoid(ug)

    qd_o[0] = _dot(h, w_ref[:, OFF_D:OFF_D + 512]).astype(BF16)
    kd_o[0] = (_dot(h, w_ref[:, OFF_D + 512:OFF_D + 1024]) * (M_DIM ** -0.5)).astype(BF16)
    vd_o[0] = _dot(h, w_ref[:, OFF_D + 1024:OFF_D + 1536]).astype(BF16)
    so_o[0] = jax.nn.sigmoid(_dot(h, w_ref[:, OFF_D + 1536:OFF_G])).astype(BF16)

    g = _dot(h, w_ref[:, OFF_G:PROJ_PAD])
    g_o[0] = g[:, 0:16] + gb_ref[...]


def _mod_index(lt, nb):
    return lambda b, t: (jnp.where(t < lt, nb, b), 0, 0)


def _inproj(x, mod3, tabs, w_bf, layer, gate_b, lctx, prev=None):
    nb, tt, d = (x if prev is None else prev[0]).shape
    nt = tt // TM
    lt = lctx // TM
    tok = lambda n: pl.BlockSpec((1, TM, n), lambda b, t: (b, t, 0))
    tab = pl.BlockSpec((TM, LANES), lambda b, t: (t, 0))
    vec = pl.BlockSpec((1, d), lambda b, t: (0, 0))
    modspec = pl.BlockSpec((1, 1, 6 * d), _mod_index(lt, nb))
    sds = lambda n, dt: jax.ShapeDtypeStruct((nb, tt, n), dt)
    in_specs = [modspec, tab, tab, tab,
                pl.BlockSpec((1, d, PROJ_PAD), lambda b, t: (layer, 0, 0)),
                pl.BlockSpec((1, 16), lambda b, t: (0, 0))]
    args = [mod3, tabs[0], tabs[1], tabs[2], w_bf, gate_b.reshape(1, 16)]
    out_specs = [tok(512), tok(256), tok(512), tok(512), tok(512), tok(512),
                 tok(512), tok(512), tok(512), tok(512), tok(16)]
    out_shape = [sds(512, BF16), sds(256, BF16), sds(512, BF16), sds(512, BF16), sds(512, BF16),
                 sds(512, F32), sds(512, BF16), sds(512, BF16), sds(512, BF16), sds(512, BF16),
                 sds(16, F32)]
    if prev is None:
        in_specs = [tok(d)] + in_specs
        args = [x] + args
        alpha = None
    else:
        x1, ml, pmod3, pg, pb, alpha = prev
        in_specs = [tok(d), tok(d), modspec, vec, vec] + in_specs
        args = [x1, ml, pmod3, pg.reshape(1, d), pb.reshape(1, d)] + args
        out_specs = [tok(d)] + out_specs
        out_shape = [sds(d, F32)] + out_shape
    return pl.pallas_call(
        functools.partial(_inproj_kernel, alpha=alpha),
        grid=(nb, nt),
        in_specs=in_specs,
        out_specs=out_specs,
        out_shape=out_shape,
        compiler_params=_cparams(("parallel", "parallel"), 48),
        name="inproj",
    )(*args)


def _attn_a_kernel(sink_ref, q_ref, kp_ref, kc_ref, kn_ref, kx_ref, mask_ref, o_ref):
    q = q_ref[0]
    kv = jnp.concatenate([kp_ref[0], kc_ref[0], kn_ref[0], kx_ref[0]], axis=0)
    mask = mask_ref[0]
    group = A_HEADS // A_KV_HEADS
    for g in range(A_KV_HEADS):
        k = kv[:, g * HEAD_DIM:(g + 1) * HEAD_DIM]
        v = kv[:, LANES + g * HEAD_DIM:LANES + (g + 1) * HEAD_DIM]
        heads = range(g * group, (g + 1) * group)
        qs = jnp.concatenate([q[:, hh * HEAD_DIM:(hh + 1) * HEAD_DIM] for hh in heads], axis=0)
        s = _dot_nt(qs, k)
        ps, ls = [], []
        for r, hh in enumerate(heads):
            sr = s[r * BLK:(r + 1) * BLK] + mask
            sink = sink_ref[hh]
            m = jnp.maximum(jnp.max(sr, axis=-1, keepdims=True), sink)
            p = jnp.exp(sr - m)
            ls.append(jnp.sum(p, axis=-1, keepdims=True) + jnp.exp(sink - m))
            ps.append(p.astype(BF16))
        o = _dot(jnp.concatenate(ps, axis=0), v)
        for r, hh in enumerate(heads):
            o_ref[0, :, hh * HEAD_DIM:(hh + 1) * HEAD_DIM] = (o[r * BLK:(r + 1) * BLK] / ls[r]).astype(BF16)


def _attn_a_mask(lctx):
    i = np.arange(BLK)[:, None]
    j = np.arange(BLK)[None, :]
    ok_prev = (j >= i)
    ok_next = (j <= i)
    yes = np.ones((BLK, BLK), bool)
    no = np.zeros((BLK, BLK), bool)
    ctx = np.ones((BLK, lctx), bool)
    variants = [
        np.concatenate([ok_prev, yes, ok_next, ctx], 1),
        np.concatenate([no, yes, ok_next, ctx], 1),
        np.concatenate([ok_prev, yes, no, ctx], 1),
        np.concatenate([no, no, no, ctx], 1),
    ]
    return jnp.asarray(np.where(np.stack(variants), 0.0, NEG_INF).astype(np.float32))


def _attn_a(qa, kva, sink, lctx):
    nb, tt, _ = qa.shape
    nblk = tt // BLK
    lb = lctx // BLK
    assert nblk - lb >= 2
    mask = _attn_a_mask(lctx)

    def variant(t):
        return jnp.where(t < lb, 3, jnp.where(t == lb, 1, jnp.where(t == nblk - 1, 2, 0)))

    kvb = lambda f: pl.BlockSpec((1, BLK, 256), lambda b, t: (b, f(t), 0))
    return pl.pallas_call(
        _attn_a_kernel,
        grid=(nb, nblk),
        in_specs=[
            pl.BlockSpec(memory_space=pltpu.SMEM),
            pl.BlockSpec((1, BLK, 512), lambda b, t: (b, t, 0)),
            kvb(lambda t: jnp.maximum(t - 1, 0)),
            kvb(lambda t: t),
            kvb(lambda t: jnp.minimum(t + 1, nblk - 1)),
            pl.BlockSpec((1, lctx, 256), lambda b, t: (b, 0, 0)),
            pl.BlockSpec((1, BLK, 3 * BLK + lctx), lambda b, t: (variant(t), 0, 0)),
        ],
        out_specs=pl.BlockSpec((1, BLK, 512), lambda b, t: (b, t, 0)),
        out_shape=jax.ShapeDtypeStruct((nb, tt, 512), BF16),
        compiler_params=_cparams(("parallel", "parallel")),
        name="attn_window",
    )(sink, qa, kva, kva, kva, kva, mask)


NB_KBLK = 5


def _attn_b_kernel(q_ref, k0, k1, k2, k3, k4, kx, v0, v1, v2, v3, v4, vx, bias_ref, o_ref):
    q = q_ref[0]
    k = jnp.concatenate([k0[0], k1[0], k2[0], k3[0], k4[0], kx[0]], axis=0)
    v = jnp.concatenate([v0[0], v1[0], v2[0], v3[0], v4[0], vx[0]], axis=0)
    nloc = NB_KBLK * BLK
    first = lax.broadcasted_iota(jnp.int32, (1, LANES), 1) < HEAD_DIM
    for pair in range(B_HEADS // 2):
        sl = slice(pair * LANES, (pair + 1) * LANES)
        q2, k2, v2 = q[:, sl], k[:, sl], v[:, sl]
        zero = jnp.zeros_like(q2)
        s2 = _dot_nt(jnp.concatenate([jnp.where(first, q2, zero), jnp.where(first, zero, q2)], axis=0), k2)
        ps, ls = [], []
        for j in range(2):
            s = s2[j * BLK:(j + 1) * BLK]
            s = jnp.concatenate([s[:, :nloc] + bias_ref[0, 0, 2 * pair + j], s[:, nloc:]], axis=1)
            p = jnp.exp(s - jnp.max(s, axis=-1, keepdims=True))
            ls.append(jnp.sum(p, axis=-1, keepdims=True))
            ps.append(p.astype(BF16))
        o2 = _dot(jnp.concatenate(ps, axis=0), v2)
        o_ref[0, :, sl] = jnp.where(first, o2[:BLK] / ls[0], o2[BLK:] / ls[1]).astype(BF16)


def _attn_b_bias(rpb, nlat):
    rows = 2 * nlat
    wh = min(NA_ROWS, rows)
    n = GRID_W
    qrows, krows = BLK // n, NB_KBLK * BLK // n
    nvar = NB_KBLK
    edge = n - NA_COLS
    g = jnp.concatenate([jnp.repeat(rpb[..., :1], edge, axis=-1), rpb.astype(F32),
                         jnp.repeat(rpb[..., -1:], edge + 1, axis=-1)], axis=-1)
    g = jnp.roll(g, -(n - 1), axis=-1)
    toep = jnp.tile(g, (1, 1, 1, n))[..., :n * (2 * n - 1)].reshape(g.shape[:3] + (n, 2 * n - 1))[..., :n]

    reps = [0, 1, 2, nlat - 2, nlat - 1]
    i = np.arange(BLK)
    kk = np.arange(NB_KBLK * BLK)
    ok, dr = [], []
    for dlt, j in enumerate(reps):
        base = j - dlt
        r = 2 * j + i // n
        qc = i % n
        kr = 2 * base + kk // n
        kc = kk % n
        start = np.clip(r - wh // 2, 0, rows - wh)
        row_ok = (kr[None, :] >= start[:, None]) & (kr[None, :] < start[:, None] + wh)
        c0 = np.clip(qc - NA_COLS // 2, 0, n - NA_COLS)
        col_ok = (kc[None, :] >= c0[:, None]) & (kc[None, :] < c0[:, None] + NA_COLS)
        ok.append(row_ok & col_ok)
        dr.append(np.clip(2 * (base - j) + np.arange(krows)[None, :] - np.arange(qrows)[:, None] + NA_ROWS - 1,
                          0, 2 * NA_ROWS - 2))
    depth, nh = rpb.shape[:2]
    bias = jnp.stack([jnp.concatenate([jnp.concatenate([toep[:, :, dr[v][a, b]] for b in range(krows)], axis=-1)
                                       for a in range(qrows)], axis=-2) for v in range(nvar)], axis=1)
    bias = jnp.where(jnp.asarray(np.stack(ok))[None, :, None], bias, NEG_INF)
    return jnp.concatenate([bias, jnp.full((depth, 1, nh, BLK, NB_KBLK * BLK), NEG_INF, F32)], axis=1)


def _attn_b(qb, kb, vb, bias, layer, lctx):
    nb, tt, _ = qb.shape
    nblk = tt // BLK
    lb = lctx // BLK
    nlat = nblk - lb

    def base(t):
        return jnp.clip(t - lb - 2, 0, nlat - NB_KBLK) + lb

    def variant(t):
        return jnp.where(t < lb, 5, t - base(t))

    loc = lambda i: pl.BlockSpec((1, BLK, 512), lambda b, t: (b, base(t) + i, 0))
    ctx = pl.BlockSpec((1, lctx, 512), lambda b, t: (b, 0, 0))
    return pl.pallas_call(
        _attn_b_kernel,
        grid=(nb, nblk),
        in_specs=[pl.BlockSpec((1, BLK, 512), lambda b, t: (b, t, 0))]
        + [loc(i) for i in range(NB_KBLK)] + [ctx]
        + [loc(i) for i in range(NB_KBLK)] + [ctx]
        + [pl.BlockSpec((1, 1, B_HEADS, BLK, NB_KBLK * BLK), lambda b, t: (layer, variant(t), 0, 0, 0))],
        out_specs=pl.BlockSpec((1, BLK, 512), lambda b, t: (b, t, 0)),
        out_shape=jax.ShapeDtypeStruct((nb, tt, 512), BF16),
        compiler_params=_cparams(("parallel", "parallel")),
        name="attn_neighbourhood",
    )(qb, *([kb] * (NB_KBLK + 1)), *([vb] * (NB_KBLK + 1)), bias)


HALO = 16


SUB = 8


NCB = C_CH // LANES


def _conv_kernel(prev_ref, cur_ref, next_ref, w_ref, b_ref, g_ref, bb_ref, o_ref, sh_ref, acc_ref, *, lb, nblk):
    t = pl.program_id(1)
    has_prev = jnp.logical_and(t != 0, t != lb)
    has_next = jnp.logical_and(t != lb - 1, t != nblk - 1)
    prev = jnp.where(has_prev, prev_ref[0], 0.0)
    nxt = jnp.where(has_next, next_ref[0], 0.0)
    cur = cur_ref[0]
    for cb in range(NCB):
        cs = slice(cb * LANES, (cb + 1) * LANES)
        sh_ref[0, cb, 0:HALO, :] = prev[:, cs]
        sh_ref[0, cb, HALO:HALO + BLK, :] = cur[:, cs]
        sh_ref[0, cb, HALO + BLK:2 * HALO + BLK, :] = nxt[:, cs]
    rows = BLK + 2 * HALO - SUB
    pad = C_KSIZE // 2

    def channel_block(cb, carry):
        for r in range(1, SUB):
            sh_ref[r, cb, 0:rows, :] = sh_ref[0, cb, pl.ds(r, rows), :]
        acc = jnp.zeros((BLK // SUB, SUB, LANES), F32)
        for kk in range(C_KSIZE):
            off = HALO - pad + kk
            rows_k = sh_ref[off % SUB, cb, off - off % SUB:off - off % SUB + BLK, :]
            acc = acc + rows_k.reshape(BLK // SUB, SUB, LANES) * w_ref[kk, cb][None]
        acc_ref[cb] = acc.reshape(BLK, LANES)
        return carry

    lax.fori_loop(0, NCB, channel_block, 0)
    y = jnp.concatenate([acc_ref[cb] for cb in range(NCB)], axis=1)
    y = _ln(y + b_ref[...]) * g_ref[...] + bb_ref[...]
    o_ref[0] = (y * jax.nn.sigmoid(y)).astype(BF16)


def _conv(yc, w, b, g, bb, lctx):
    nb, tt, ch = yc.shape
    nblk = tt // BLK
    lb = lctx // BLK
    per = BLK // HALO
    vec = pl.BlockSpec((1, ch), lambda b_, t: (0, 0))
    return pl.pallas_call(
        functools.partial(_conv_kernel, lb=lb, nblk=nblk),
        grid=(nb, nblk),
        in_specs=[
            pl.BlockSpec((1, HALO, ch), lambda b_, t: (b_, jnp.maximum(t * per - 1, 0), 0)),
            pl.BlockSpec((1, BLK, ch), lambda b_, t: (b_, t, 0)),
            pl.BlockSpec((1, HALO, ch), lambda b_, t: (b_, jnp.minimum((t + 1) * per, nblk * per - 1), 0)),
            pl.BlockSpec((C_KSIZE, NCB, SUB, LANES), lambda b_, t: (0, 0, 0, 0)),
            vec, vec, vec,
        ],
        out_specs=pl.BlockSpec((1, BLK, ch), lambda b_, t: (b_, t, 0)),
        out_shape=jax.ShapeDtypeStruct((nb, tt, ch), BF16),
        scratch_shapes=[pltpu.VMEM((SUB, NCB, BLK + 2 * HALO, LANES), F32), pltpu.VMEM((NCB, BLK, LANES), F32)],
        compiler_params=_cparams(("parallel", "parallel")),
        name="conformer_conv",
    )(yc, yc, yc, jnp.broadcast_to(w.reshape(C_KSIZE, NCB, 1, LANES), (C_KSIZE, NCB, SUB, LANES)),
      b.reshape(1, ch), g.reshape(1, ch),
      bb.reshape(1, ch))


def _mlstm_kernel(qf_ref, kf_ref, vf_ref, gcf_ref, grf_ref, qb_ref, kb_ref, vb_ref, gcb_ref, grb_ref,
                  hf_ref, hb_ref, c_st, n_st, m_st, *, mb):
    @pl.when(pl.program_id(1) == 0)
    def _():
        c_st[...] = jnp.zeros_like(c_st)
        n_st[...] = jnp.zeros_like(n_st)
        m_st[...] = jnp.zeros_like(m_st)

    rr = lax.broadcasted_iota(jnp.int32, (BLK, BLK), 0)
    cc = lax.broadcasted_iota(jnp.int32, (BLK, BLK), 1)
    ch = []
    dirs = ((qf_ref, kf_ref, vf_ref, gcf_ref, grf_ref, hf_ref), (qb_ref, kb_ref, vb_ref, gcb_ref, grb_ref, hb_ref))
    for bb, dirn in [(bb, dirn) for bb in range(mb) for dirn in range(2)]:
        q_ref, k_ref, v_ref, gc_ref, gr_ref, h_ref = dirs[dirn]
        before = (rr >= cc) if dirn == 0 else (rr <= cc)
        bmat = jnp.where(before, 1.0, 0.0).astype(BF16)
        gc = gc_ref[0, bb]
        gr = gr_ref[0, bb]
        lf_c = jax.nn.log_sigmoid(gc[:, M_HEADS:2 * M_HEADS])
        lf_r = jax.nn.log_sigmoid(gr[M_HEADS:2 * M_HEADS, :])
        fc = sum(_dot(bmat, part) for part in _split3(lf_c))
        fr = sum(_dot_nt(part, bmat) for part in _split3(lf_r))
        f_tot = jnp.sum(lf_r, axis=-1, keepdims=True)
        for hh in range(M_HEADS):
            sl = slice(hh * M_DIM, (hh + 1) * M_DIM)
            st = (bb * 2 + dirn) * M_HEADS + hh
            ch.append(dict(before=before, sl=sl, st=st, h_ref=h_ref, bb=bb,
                           q=q_ref[bb, :, sl], k=k_ref[bb, :, sl], v=v_ref[bb, :, sl],
                           f_c=fc[:, hh:hh + 1], f_r=fr[hh:hh + 1, :], f_tot=f_tot[hh:hh + 1, :],
                           i_c=gc[:, hh:hh + 1], i_r=gr[hh:hh + 1, :],
                           c_old=c_st[st], n_old=n_st[st], m_old=m_st[st][:, 0:1]))

    for c in ch:
        c["qk"] = _dot_nt(c["q"], c["k"])
        c["qc"] = _dot(c["q"], c["c_old"].astype(BF16))
    for c in ch:
        c["a"] = c["f_c"] + c["m_old"]
        c["logw"] = jnp.where(c["before"], c["f_c"] - c["f_r"] + c["i_r"], -jnp.inf)
        c["mt"] = jnp.maximum(c["a"], jnp.max(c["logw"], axis=-1, keepdims=True))
    for c in ch:
        g_r = c["f_tot"] - c["f_r"] + c["i_r"]
        c["m_new"] = jnp.maximum(c["f_tot"] + c["m_old"], jnp.max(g_r, axis=-1, keepdims=True))
        c["decay"] = jnp.exp(c["f_tot"] + c["m_old"] - c["m_new"])
        c["kw"] = c["k"].astype(F32) * jnp.exp(c["f_tot"] - c["f_c"] + c["i_c"] - c["m_new"])
    for c in ch:
        c["s"] = c["qk"] * jnp.exp(c["logw"] - c["mt"])
        c["w_inter"] = jnp.exp(c["a"] - c["mt"])
    for c in ch:
        c["sv"] = _dot(c["s"].astype(BF16), c["v"])
        c["kv"] = lax.dot_general(c["kw"].astype(BF16), c["v"], (((0,), (0,)), ((), ())), preferred_element_type=F32)
    for c in ch:
        num = c["w_inter"] * c["qc"] + c["sv"]
        den = (c["w_inter"] * jnp.sum(c["q"].astype(F32) * c["n_old"], axis=-1, keepdims=True)
               + jnp.sum(c["s"], axis=-1, keepdims=True))
        hout = num / jnp.maximum(jnp.abs(den), jnp.exp(-c["mt"]))
        c["h_ref"][c["bb"], :, c["sl"]] = hout.astype(BF16)
    for c in ch:
        st = c["st"]
        c_st[st] = c["decay"] * c["c_old"] + c["kv"]
        n_st[st] = c["decay"] * c["n_old"] + jnp.sum(c["kw"], axis=0, keepdims=True)
        m_st[st] = jnp.broadcast_to(c["m_new"], (1, LANES))


def _mlstm(qd, kd, vd, gates, lctx):
    nb, tt, _ = qd.shape
    nblk = tt // BLK
    lb = lctx // BLK
    gcol = jnp.stack([gates[..., 0:8], gates[..., 8:16]])
    grow = jnp.swapaxes(gcol, 2, 3)

    def bwd(i):
        return jnp.where(i < lb, lb - 1 - i, nblk - 1 + lb - i)

    fwd = lambda i: i
    mb = max(m for m in (4, 2, 1) if nb % m == 0)
    tok = lambda f: pl.BlockSpec((mb, BLK, 512), lambda b, i: (b, f(i), 0))
    gcs = lambda d, f: pl.BlockSpec((1, mb, BLK, 8), lambda b, i: (d, b, f(i), 0))
    grs = lambda d, f: pl.BlockSpec((1, mb, 8, BLK), lambda b, i: (d, b, 0, f(i)))
    out = jax.ShapeDtypeStruct((nb, tt, 512), BF16)
    nst = 2 * mb * M_HEADS
    return pl.pallas_call(
        functools.partial(_mlstm_kernel, mb=mb),
        grid=(nb // mb, nblk),
        in_specs=[tok(fwd), tok(fwd), tok(fwd), gcs(0, fwd), grs(0, fwd),
                  tok(bwd), tok(bwd), tok(bwd), gcs(1, bwd), grs(1, bwd)],
        out_specs=[tok(fwd), tok(bwd)],
        out_shape=[out, out],
        scratch_shapes=[pltpu.VMEM((nst, M_DIM, M_DIM), F32),
                        pltpu.VMEM((nst, 1, M_DIM), F32),
                        pltpu.VMEM((nst, 1, LANES), F32)],
        compiler_params=_cparams(("parallel", "arbitrary")),
        name="mlstm",
    )(qd, kd, vd, gcol, grow, qd, kd, vd, gcol, grow)


def _merge_kernel(x_ref, mod_ref, ya_ref, yb_ref, yc_ref, hf_ref, hb_ref, so_ref,
                  wg_ref, bg_ref, wbr_ref, wo_ref, g1_ref, b1_ref, wr_ref,
                  x1_o, hp_o, aff_o, *, alpha):
    d = D_MODEL
    x = x_ref[0]
    mod = mod_ref[0]
    h = (_ln(x) * (1.0 + mod[:, d:2 * d]) + mod[:, 0:d]).astype(BF16)
    yd = (so_ref[0].astype(F32) * (hf_ref[0].astype(F32) + hb_ref[0].astype(F32))).astype(BF16)
    ys = (ya_ref[0], yb_ref[0], yc_ref[0], yd)
    pre = [_dot(h, wg_ref[0, :, i * d:(i + 1) * d]) for i in range(N_BRANCH)]
    br = [_dot(ys[i], wbr_ref[0, i]) for i in range(N_BRANCH)]
    z = None
    for i in range(N_BRANCH):
        term = jax.nn.sigmoid(pre[i] + bg_ref[:, i * d:(i + 1) * d]) * br[i]
        z = term if z is None else z + term
    y = _dot(z.astype(BF16), wo_ref[0])
    x1 = _ln(alpha * x + mod[:, 2 * d:3 * d] * y) * g1_ref[...] + b1_ref[...]
    x1_o[0] = x1

    h2 = _ln(x1) * (1.0 + mod[:, 4 * d:5 * d]) + mod[:, 3 * d:4 * d]
    hb16 = h2.astype(BF16)
    bits = pltpu.bitcast(hb16.astype(F32), jnp.uint32)
    hp_o[0] = (bits[:, d // 2:] & jnp.uint32(0xFFFF0000)) | (bits[:, :d // 2] >> 16)

    h2_hi, h2_lo = _split2(h2)
    r_hi = _dot(h2_hi, wr_ref[...])
    r_lo = _dot(h2_lo, wr_ref[...])
    logits = r_hi[:, 0:N_EXPERTS] + (r_hi[:, N_EXPERTS:2 * N_EXPERTS] + r_lo[:, 0:N_EXPERTS])
    e = jnp.exp(logits - jnp.max(logits, axis=-1, keepdims=True))
    aff_o[0] = e / jnp.sum(e, axis=-1, keepdims=True)


def _merge(x, mod3, ya, yb, yc, hf, hb, so, wg, bg, wbr, wo, layer, g1, b1, wr, lctx, alpha):
    nb, tt, d = x.shape
    nt = tt // TM
    lt = lctx // TM
    tok = lambda n: pl.BlockSpec((1, TM, n), lambda b, t: (b, t, 0))
    const = lambda shape: pl.BlockSpec(shape, lambda b, t: (0,) * len(shape))
    stacked = lambda shape: pl.BlockSpec((1,) + shape, lambda b, t: (layer,) + (0,) * len(shape))
    return pl.pallas_call(
        functools.partial(_merge_kernel, alpha=alpha),
        grid=(nb, nt),
        in_specs=[
            tok(d),
            pl.BlockSpec((1, 1, 6 * d), _mod_index(lt, nb)),
            tok(512), tok(512), tok(512), tok(512), tok(512), tok(512),
            stacked((d, N_BRANCH * d)), const((1, N_BRANCH * d)), stacked((N_BRANCH, BRANCH_W, d)),
            stacked((d, d)), const((1, d)), const((1, d)), const((d, LANES)),
        ],
        out_specs=[tok(d), tok(d // 2), tok(N_EXPERTS)],
        out_shape=[jax.ShapeDtypeStruct((nb, tt, d), F32),
                   jax.ShapeDtypeStruct((nb, tt, d // 2), jnp.uint32),
                   jax.ShapeDtypeStruct((nb, tt, N_EXPERTS), F32)],
        compiler_params=_cparams(("parallel", "parallel"), 56),
        name="merge",
    )(x, mod3, ya, yb, yc, hf, hb, so, wg, bg.reshape(1, -1), wbr, wo,
      g1.reshape(1, d), b1.reshape(1, d), wr)


def _route_one(a_ref, ones_ref, tab_ref, idx_o, val_o, *, ntok, cap, tok_off, slot_off):
    a = a_ref[0]
    bits = pltpu.bitcast(a, jnp.int32)
    thr = jnp.zeros((N_EXPERTS, 1), jnp.int32)
    for bit in range(30, -1, -1):
        cand = thr | jnp.int32(1 << bit)
        cnt = jnp.sum(jnp.where(bits >= cand, 1.0, 0.0), axis=-1, keepdims=True)
        thr = jnp.where(cnt >= cap, cand, thr)
    gt = bits > thr
    eq = bits == thr
    need = cap - jnp.sum(jnp.where(gt, 1.0, 0.0), axis=-1, keepdims=True)

    rr = lax.broadcasted_iota(jnp.int32, (LANES, LANES), 0)
    cc = lax.broadcasted_iota(jnp.int32, (LANES, LANES), 1)
    upper = jnp.where(rr <= cc, 1.0, 0.0).astype(BF16)

    def cumsum_blocks(mask_f):
        run = jnp.zeros((N_EXPERTS, 1), F32)
        out = []
        for c in range(ntok // LANES):
            blk = mask_f[:, c * LANES:(c + 1) * LANES]
            out.append(_dot(blk.astype(BF16), upper) + run)
            run = run + jnp.sum(blk, axis=-1, keepdims=True)
        return out

    eq_f = jnp.where(eq, 1.0, 0.0)
    cum_eq = cumsum_blocks(eq_f)
    sel_parts = []
    for c in range(ntok // LANES):
        sl = slice(c * LANES, (c + 1) * LANES)
        sel_parts.append(jnp.where(gt[:, sl] | (eq[:, sl] & (cum_eq[c] <= need)), 1.0, 0.0))
    sel_f = jnp.concatenate(sel_parts, axis=1)
    cpos = cumsum_blocks(sel_f)
    nblk = ntok // LANES
    tab_ref[...] = jnp.zeros_like(tab_ref)
    for c in range(nblk):
        for e in range(N_EXPERTS):
            tab_ref[0, e, c:c + 1, :] = cpos[c][e:e + 1, :]
            tab_ref[1, e, c:c + 1, :] = a_ref[0, e:e + 1, c * LANES:(c + 1) * LANES]
    counts = _dot(sel_f.astype(BF16), ones_ref[...])
    blockend = _dot(counts.astype(BF16), upper)
    prevend = blockend - counts

    lane = lax.broadcasted_iota(jnp.int32, (1, LANES), 1).astype(F32)
    slot = lax.broadcasted_iota(jnp.int32, (cap, 1), 0).astype(F32)
    for e in range(N_EXPERTS):
        be = blockend[e:e + 1, :]
        pe = prevend[e:e + 1, :]
        pick = jnp.where(pe <= slot, jnp.where(slot < be, 1.0, 0.0), 0.0).astype(BF16)
        cnt_blk = sum(_dot(pick, part) for part in _split2(tab_ref[0, e]))
        aff_blk = sum(_dot(pick, part) for part in _split3(tab_ref[1, e]))
        within = jnp.sum(jnp.where(cnt_blk <= slot, 1.0, 0.0), axis=-1, keepdims=True)
        nfull = jnp.sum(jnp.where(be <= slot, 1.0, 0.0), axis=-1, keepdims=True)
        idx_o[0, e, slot_off:slot_off + cap, :] = (nfull * LANES + within + tok_off).astype(jnp.int32)
        val_o[0, e, slot_off:slot_off + cap, :] = jnp.sum(jnp.where(lane == within, aff_blk, 0.0),
                                                           axis=-1, keepdims=True)


def _route_kernel(al_ref, ac_ref, onesl_ref, onesc_ref, idx_o, val_o, tab_ref, *, s, lctx, cap_l, cap_c):
    _route_one(al_ref, onesl_ref, tab_ref, idx_o, val_o, ntok=s, cap=cap_l, tok_off=lctx, slot_off=0)
    _route_one(ac_ref, onesc_ref, tab_ref, idx_o, val_o, ntok=lctx, cap=cap_c, tok_off=0, slot_off=cap_l)


def _route(aff, lctx):
    nb, tt, ne = aff.shape
    s = tt - lctx
    cap_l = CAPACITY_FACTOR * s // ne
    cap_c = CAPACITY_FACTOR * lctx // ne
    capt = cap_l + cap_c
    assert s // LANES <= LANES and cap_l % 8 == 0 and cap_c % 8 == 0
    aff_t = jnp.swapaxes(aff, 1, 2)
    ones = lambda n: jnp.asarray(np.arange(n)[:, None] // LANES == np.arange(LANES)[None, :], BF16)
    out = lambda dt: jax.ShapeDtypeStruct((nb, ne, capt, 1), dt)
    return pl.pallas_call(
        functools.partial(_route_kernel, s=s, lctx=lctx, cap_l=cap_l, cap_c=cap_c),
        grid=(nb,),
        in_specs=[pl.BlockSpec((1, ne, s), lambda b: (b, 0, 0)),
                  pl.BlockSpec((1, ne, lctx), lambda b: (b, 0, 0)),
                  pl.BlockSpec((s, LANES), lambda b: (0, 0)),
                  pl.BlockSpec((lctx, LANES), lambda b: (0, 0))],
        out_specs=[pl.BlockSpec((1, ne, capt, 1), lambda b: (b, 0, 0, 0))] * 2,
        out_shape=[out(jnp.int32), out(F32)],
        scratch_shapes=[pltpu.VMEM((2, ne, LANES, LANES), F32)],
        compiler_params=_cparams(("parallel",), 40),
        name="route",
    )(aff_t[:, :, lctx:], aff_t[:, :, :lctx], ones(s), ones(lctx))


def _gather_kernel(idx_ref, x_ref, o_ref, *, capt):
    def body(i, carry):
        c0 = pl.multiple_of(i * SUB, SUB)
        rows = [x_ref[0, pl.ds(idx_ref[0, 0, c0 + j], 1), :] for j in range(SUB)]
        o_ref[0, 0, pl.ds(c0, SUB), :] = jnp.concatenate(rows, axis=0)
        return carry
    lax.fori_loop(0, capt // SUB, body, 0)


def _gather(idx_s, hp, capt):
    nb, tt, w = hp.shape
    assert capt % SUB == 0
    ne = N_EXPERTS
    return pl.pallas_call(
        functools.partial(_gather_kernel, capt=capt),
        grid=(nb, ne),
        in_specs=[pl.BlockSpec((1, 1, capt), lambda b, e: (b * ne + e, 0, 0), memory_space=pltpu.SMEM),
                  pl.BlockSpec((1, tt, w), lambda b, e: (b, 0, 0))],
        out_specs=pl.BlockSpec((1, 1, capt, w), lambda b, e: (b, e, 0, 0)),
        out_shape=jax.ShapeDtypeStruct((nb, ne, capt, w), jnp.uint32),
        compiler_params=_cparams(("parallel", "arbitrary"), 40),
        name="moe_gather",
    )(idx_s, hp)


def _ffn_kernel(x_ref, w1_ref, w3_ref, w2_ref, val_ref, y_ref, w1b, w3b, w2b):
    @pl.when(pl.program_id(1) == 0)
    def _():
        w1b[...] = w1_ref[0, 0].astype(BF16)
        w3b[...] = w3_ref[0, 0].astype(BF16)
        w2b[...] = w2_ref[0, 0].astype(BF16)

    packed = x_ref[0, 0]
    lo = pltpu.bitcast(packed << 16, F32)
    hi = pltpu.bitcast(packed & jnp.uint32(0xFFFF0000), F32)
    xg = jnp.concatenate([lo, hi], axis=1).astype(BF16)
    a = _dot(xg, w1b[...])
    hid = (a * jax.nn.sigmoid(a) * _dot(xg, w3b[...])).astype(BF16)
    y_ref[0, 0] = _dot(hid, w2b[...]) * val_ref[0, 0]


def _ffn(xg, w1, w3, w2, vals, layer):
    nb, ne, capt, w = xg.shape
    d, ff = w1.shape[2], w1.shape[3]
    return pl.pallas_call(
        _ffn_kernel,
        grid=(ne, nb),
        in_specs=[pl.BlockSpec((1, 1, capt, w), lambda e, b: (b, e, 0, 0)),
                  pl.BlockSpec((1, 1, d, ff), lambda e, b: (layer, e, 0, 0)),
                  pl.BlockSpec((1, 1, d, ff), lambda e, b: (layer, e, 0, 0)),
                  pl.BlockSpec((1, 1, ff, d), lambda e, b: (layer, e, 0, 0)),
                  pl.BlockSpec((1, 1, capt, 1), lambda e, b: (b, e, 0, 0))],
        out_specs=pl.BlockSpec((1, 1, capt, d), lambda e, b: (b, e, 0, 0)),
        out_shape=jax.ShapeDtypeStruct((nb, ne, capt, d), F32),
        scratch_shapes=[pltpu.VMEM((d, ff), BF16), pltpu.VMEM((d, ff), BF16), pltpu.VMEM((ff, d), BF16)],
        compiler_params=_cparams(("parallel", "arbitrary"), 56),
        name="moe_ffn",
    )(xg, w1, w3, w2, vals)


SCATTER_GROUP = 8


def _scatter_kernel(idx_ref, y_ref, o_ref, *, capt):
    @pl.when(pl.program_id(1) == 0)
    def _():
        o_ref[...] = jnp.zeros_like(o_ref)

    def body(i, carry):
        c0 = pl.multiple_of(i * SCATTER_GROUP, SCATTER_GROUP)
        rows = [pl.ds(idx_ref[0, 0, c0 + j], 1) for j in range(SCATTER_GROUP)]
        acc = [o_ref[0, r, :] for r in rows]
        ytile = y_ref[0, 0, pl.ds(c0, SCATTER_GROUP), :]
        new = [a + ytile[j:j + 1, :] for j, a in enumerate(acc)]
        for r, v in zip(rows, new):
            o_ref[0, r, :] = v
        return carry
    lax.fori_loop(0, capt // SCATTER_GROUP, body, 0)


def _scatter(idx_s, y, tt):
    nb, ne, capt, d = y.shape
    assert capt % SCATTER_GROUP == 0
    return pl.pallas_call(
        functools.partial(_scatter_kernel, capt=capt),
        grid=(nb, ne),
        in_specs=[pl.BlockSpec((1, 1, capt), lambda b, e: (b * ne + e, 0, 0), memory_space=pltpu.SMEM),
                  pl.BlockSpec((1, 1, capt, d), lambda b, e: (b, e, 0, 0))],
        out_specs=pl.BlockSpec((1, tt, d), lambda b, e: (b, 0, 0)),
        out_shape=jax.ShapeDtypeStruct((nb, tt, d), F32),
        compiler_params=_cparams(("parallel", "arbitrary"), 56),
        name="moe_scatter",
    )(idx_s, y)


def _post_kernel(x_ref, ml_ref, mod_ref, g_ref, b_ref, o_ref, *, alpha):
    d = D_MODEL
    g2 = mod_ref[0][:, 5 * d:6 * d]
    o_ref[0] = _ln(alpha * x_ref[0] + g2 * ml_ref[0]) * g_ref[...] + b_ref[...]


def _post(x1, ml, mod3, g, b, lctx, alpha):
    nb, tt, d = x1.shape
    lt = lctx // TM
    t0 = lt
    tok = pl.BlockSpec((1, TM, d), lambda b_, t: (b_, t + t0, 0))
    vec = pl.BlockSpec((1, d), lambda b_, t: (0, 0))
    return pl.pallas_call(
        functools.partial(_post_kernel, alpha=alpha),
        grid=(nb, tt // TM - t0),
        in_specs=[tok, tok, pl.BlockSpec((1, 1, 6 * d), lambda b_, t: (jnp.where(t + t0 < lt, nb, b_), 0, 0)),
                  vec, vec],
        out_specs=pl.BlockSpec((1, TM, d), lambda b_, t: (b_, t, 0)),
        out_shape=jax.ShapeDtypeStruct((nb, tt - t0 * TM, d), F32),
        compiler_params=_cparams(("parallel", "parallel")),
        name="moe_post",
    )(x1, ml, mod3, g.reshape(1, d), b.reshape(1, d))


def _rope_tables(s, lctx):
    half = HEAD_DIM // 2
    nf = half // 2
    inv = ROPE_BASE ** (-jnp.arange(nf, dtype=F32) / nf)
    t = jnp.arange(s)
    lane = np.arange(LANES)
    dd = lane % HEAD_DIM
    use_col = jnp.asarray(dd >= half)[None, :]
    pos = jnp.where(use_col, (t % GRID_W)[:, None], (t // GRID_W)[:, None]).astype(F32)
    ang = pos * inv[jnp.asarray(dd % nf)][None, :]
    cos, sin = jnp.cos(ang), jnp.sin(ang)
    first = jnp.asarray((dd % half) < nf)[None, :]
    sa = jnp.where(first, -sin, 0.0)
    sb = jnp.where(first, 0.0, sin)
    pad = lambda a, v: jnp.concatenate([jnp.full((lctx, LANES), v, F32), a], axis=0)
    return pad(cos, 1.0), pad(sa, 0.0), pad(sb, 0.0)


def kernel(x, c, ctx, c_ctx, w_mod, b_mod, w_in, attn_sink, na_rpb, conv_w, conv_b, conv_ln_g, conv_ln_b,
           mlstm_gate_b, w_branch, w_gate, b_gate, w_out, ln1_g, ln1_b, w_router, w_exp_gate, w_exp_up,
           w_exp_down, ln2_g, ln2_b):
    nb, s, d = x.shape
    lctx = ctx.shape[1]
    depth = w_mod.shape[0]
    assert d == D_MODEL and nb + 1 <= 8 and lctx % TM == 0 and s % TM == 0 and s % GRID_W == 0
    alpha = (2.0 * depth) ** 0.25
    tt = lctx + s
    cap_t = CAPACITY_FACTOR * s // N_EXPERTS + CAPACITY_FACTOR * lctx // N_EXPERTS

    c8 = jnp.concatenate([c, c_ctx[None, :], jnp.zeros((8 - nb - 1, d), F32)], axis=0)
    mod_all = _modulation(c8, w_mod, b_mod)
    tabs = _rope_tables(s, lctx)
    xs = jnp.concatenate([ctx, x], axis=1)
    nlat = s // BLK
    assert nlat >= NB_KBLK
    bias_b = _attn_b_bias(na_rpb, nlat)
    w_in_bf = jnp.pad(w_in, ((0, 0), (0, 0), (0, PROJ_PAD - PROJ_W))).astype(BF16)
    w_gate_bf, w_branch_bf, w_out_bf = w_gate.astype(BF16), w_branch.astype(BF16), w_out.astype(BF16)

    for l in range(depth):
        mod3 = mod_all[l].reshape(8, 1, 6 * d)
        if l == 0:
            outs = _inproj(xs, mod3, tabs, w_in_bf, l, mlstm_gate_b[l], lctx)
        else:
            xs, *outs = _inproj(None, mod3, tabs, w_in_bf, l, mlstm_gate_b[l], lctx, prev=prev)
        (qa, kva, qb, kb, vb, yc0, qd, kd, vd, so, gates) = outs
        ya = _attn_a(qa, kva, attn_sink[l], lctx)
        yb = _attn_b(qb, kb, vb, bias_b, l, lctx)
        yc = _conv(yc0, conv_w[l], conv_b[l], conv_ln_g[l], conv_ln_b[l], lctx)
        hf, hb = _mlstm(qd, kd, vd, gates, lctx)
        wr_hi, wr_lo = _split2(w_router[l])
        wr = jnp.pad(jnp.concatenate([wr_hi, wr_lo], axis=1), ((0, 0), (0, LANES - 2 * N_EXPERTS)))
        x1, hp, aff = _merge(xs, mod3, ya, yb, yc, hf, hb, so, w_gate_bf, b_gate[l], w_branch_bf, w_out_bf, l,
                             ln1_g[l], ln1_b[l], wr, lctx, alpha)
        idx, vals = _route(aff, lctx)
        idx_s = idx.reshape(nb * N_EXPERTS, 1, cap_t)
        xg = _gather(idx_s, hp, cap_t)
        y = _ffn(xg, w_exp_gate, w_exp_up, w_exp_down, vals, l)
        ml = _scatter(idx_s, y, tt)
        prev = (x1, ml, mod3, ln2_g[l], ln2_b[l], alpha)
    return _post(x1, ml, mod3, ln2_g[depth - 1], ln2_b[depth - 1], lctx, alpha)
```

```python
import functools
import math

import numpy as np
import jax
import jax.numpy as jnp
from jax import lax
from jax.experimental import pallas as pl
from jax.experimental.pallas import tpu as pltpu

F32 = jnp.float32
BF16 = jnp.bfloat16

D_MODEL = 1024
GRID_W = 64
HEAD_DIM = 64
A_HEADS = 8
A_KV_HEADS = 2
A_WINDOW = 128
B_HEADS = 8
NA_ROWS = 8
NA_COLS = 16
C_CH = 512
C_KSIZE = 31
M_HEADS = 4
M_DIM = 128
N_BRANCH = 4
BRANCH_W = 512
N_EXPERTS = 16
EXPERT_FF = 1024
CAPACITY_FACTOR = 2
ROPE_BASE = 10000.0
LN_EPS = 1e-6
NEG_INF = -1e30

BLK = 128
TM = 256
LANES = 128
PROJ_W = 5392
PROJ_PAD = 5504
OFF_A, OFF_AKV, OFF_B, OFF_C, OFF_D, OFF_G = 0, 512, 768, 2304, 3328, 5376
MIB = 1 << 20


def _cparams(sem, vmem_mib=None):
    kw = dict(dimension_semantics=sem)
    if vmem_mib is not None:
        kw["vmem_limit_bytes"] = vmem_mib * MIB
    return pltpu.CompilerParams(**kw)


def _ln(x):
    mu = jnp.mean(x, axis=-1, keepdims=True)
    xc = x - mu
    var = jnp.mean(xc * xc, axis=-1, keepdims=True)
    return xc * lax.rsqrt(var + LN_EPS)


def _dot(a, b):
    return jnp.dot(a, b, preferred_element_type=F32)


def _dot_nt(a, b):
    return lax.dot_general(a, b, (((1,), (1,)), ((), ())), preferred_element_type=F32)


def _split2(x):
    hi = x.astype(BF16)
    lo = (x - hi.astype(F32)).astype(BF16)
    return hi, lo


def _split3(x):
    hi = x.astype(BF16)
    r = x - hi.astype(F32)
    mid = r.astype(BF16)
    lo = (r - mid.astype(F32)).astype(BF16)
    return hi, mid, lo


def _dot3(x, w):
    xh, xl = _split2(x)
    wh, wl = _split2(w)
    return _dot(xh, wh) + (_dot(xh, wl) + _dot(xl, wh))


def _mod_kernel(c_ref, w_ref, b_ref, o_ref):
    c = c_ref[...]
    s = c * jax.nn.sigmoid(c)
    o_ref[0] = _dot3(s, w_ref[0]) + b_ref[0]


def _modulation(c8, w_mod, b_mod):
    depth, d, d6 = w_mod.shape
    nj = d6 // d
    return pl.pallas_call(
        _mod_kernel,
        grid=(depth, nj),
        in_specs=[
            pl.BlockSpec((8, d), lambda l, j: (0, 0)),
            pl.BlockSpec((1, d, d), lambda l, j: (l, 0, j)),
            pl.BlockSpec((1, 1, d), lambda l, j: (l, 0, j)),
        ],
        out_specs=pl.BlockSpec((1, 8, d), lambda l, j: (l, 0, j)),
        out_shape=jax.ShapeDtypeStruct((depth, 8, d6), F32),
        compiler_params=_cparams(("arbitrary", "arbitrary"), 40),
        name="modulation",
    )(c8, w_mod, b_mod.reshape(depth, 1, d6))


def _rope(x, cos, sa, sb):
    parts = []
    for j in range(x.shape[1] // LANES):
        xj = x[:, j * LANES:(j + 1) * LANES]
        parts.append(xj * cos + pltpu.roll(xj, LANES - 16, 1) * sa + pltpu.roll(xj, 16, 1) * sb)
    return parts[0] if len(parts) == 1 else jnp.concatenate(parts, axis=1)


def _inproj_kernel(*refs, alpha):
    d = D_MODEL
    if alpha is None:
        x_ref, mod_ref, cos_ref, sa_ref, sb_ref, w3_ref, gb_ref = refs[:7]
        outs = refs[7:]
        x = x_ref[0]
    else:
        x1_ref, ml_ref, pmod_ref, pg_ref, pb_ref, mod_ref, cos_ref, sa_ref, sb_ref, w3_ref, gb_ref = refs[:11]
        x_o = refs[11]
        outs = refs[12:]
        x = _ln(alpha * x1_ref[0] + pmod_ref[0][:, 5 * d:6 * d] * ml_ref[0]) * pg_ref[...] + pb_ref[...]
        x_o[0] = x
    qa_o, kva_o, qb_o, kb_o, vb_o, yc_o, qd_o, kd_o, vd_o, so_o, g_o = outs
    w_ref = w3_ref.at[0]
    mod = mod_ref[0]
    h = (_ln(x) * (1.0 + mod[:, d:2 * d]) + mod[:, 0:d]).astype(BF16)
    cos, sa, sb = cos_ref[...], sa_ref[...], sb_ref[...]
    qscale = HEAD_DIM ** -0.5

    qa = _dot(h, w_ref[:, OFF_A:OFF_AKV])
    qa_o[0] = (_rope(qa, cos, sa, sb) * qscale).astype(BF16)
    kva = _dot(h, w_ref[:, OFF_AKV:OFF_B])
    kva_o[0, :, 0:LANES] = _rope(kva[:, 0:LANES], cos, sa, sb).astype(BF16)
    kva_o[0, :, LANES:2 * LANES] = kva[:, LANES:2 * LANES].astype(BF16)

    qb_o[0] = (_dot(h, w_ref[:, OFF_B:OFF_B + 512]) * qscale).astype(BF16)
    kb_o[0] = _dot(h, w_ref[:, OFF_B + 512:OFF_B + 1024]).astype(BF16)
    vb_o[0] = _dot(h, w_ref[:, OFF_B + 1024:OFF_C]).astype(BF16)

    ua = _dot(h, w_ref[:, OFF_C:OFF_C + C_CH])
    ug = _dot(h, w_ref[:, OFF_C + C_CH:OFF_D])
    yc_o[0] = ua * jax.nn.sigmoid(ug)

    qd_o[0] = _dot(h, w_ref[:, OFF_D:OFF_D + 512]).astype(BF16)
    kd_o[0] = (_dot(h, w_ref[:, OFF_D + 512:OFF_D + 1024]) * (M_DIM ** -0.5)).astype(BF16)
    vd_o[0] = _dot(h, w_ref[:, OFF_D + 1024:OFF_D + 1536]).astype(BF16)
    so_o[0] = jax.nn.sigmoid(_dot(h, w_ref[:, OFF_D + 1536:OFF_G])).astype(BF16)

    g = _dot(h, w_ref[:, OFF_G:PROJ_PAD])
    g_o[0] = g[:, 0:16] + gb_ref[...]


def _mod_index(lt, nb):
    return lambda b, t: (jnp.where(t < lt, nb, b), 0, 0)


def _inproj(x, mod3, tabs, w_bf, layer, gate_b, lctx, prev=None):
    nb, tt, d = (x if prev is None else prev[0]).shape
    nt = tt // TM
    lt = lctx // TM
    tok = lambda n: pl.BlockSpec((1, TM, n), lambda b, t: (b, t, 0))
    tab = pl.BlockSpec((TM, LANES), lambda b, t: (t, 0))
    vec = pl.BlockSpec((1, d), lambda b, t: (0, 0))
    modspec = pl.BlockSpec((1, 1, 6 * d), _mod_index(lt, nb))
    sds = lambda n, dt: jax.ShapeDtypeStruct((nb, tt, n), dt)
    in_specs = [modspec, tab, tab, tab,
                pl.BlockSpec((1, d, PROJ_PAD), lambda b, t: (layer, 0, 0)),
                pl.BlockSpec((1, 16), lambda b, t: (0, 0))]
    args = [mod3, tabs[0], tabs[1], tabs[2], w_bf, gate_b.reshape(1, 16)]
    out_specs = [tok(512), tok(256), tok(512), tok(512), tok(512), tok(512),
                 tok(512), tok(512), tok(512), tok(512), tok(16)]
    out_shape = [sds(512, BF16), sds(256, BF16), sds(512, BF16), sds(512, BF16), sds(512, BF16),
                 sds(512, F32), sds(512, BF16), sds(512, BF16), sds(512, BF16), sds(512, BF16),
                 sds(16, F32)]
    if prev is None:
        in_specs = [tok(d)] + in_specs
        args = [x] + args
        alpha = None
    else:
        x1, ml, pmod3, pg, pb, alpha = prev
        in_specs = [tok(d), tok(d), modspec, vec, vec] + in_specs
        args = [x1, ml, pmod3, pg.reshape(1, d), pb.reshape(1, d)] + args
        out_specs = [tok(d)] + out_specs
        out_shape = [sds(d, F32)] + out_shape
    return pl.pallas_call(
        functools.partial(_inproj_kernel, alpha=alpha),
        grid=(nb, nt),
        in_specs=in_specs,
        out_specs=out_specs,
        out_shape=out_shape,
        compiler_params=_cparams(("parallel", "parallel"), 48),
        name="inproj",
    )(*args)


def _attn_a_kernel(sink_ref, q_ref, kp_ref, kc_ref, kn_ref, kx_ref, mask_ref, o_ref):
    q = q_ref[0]
    kv = jnp.concatenate([kp_ref[0], kc_ref[0], kn_ref[0], kx_ref[0]], axis=0)
    mask = mask_ref[0]
    group = A_HEADS // A_KV_HEADS
    for g in range(A_KV_HEADS):
        k = kv[:, g * HEAD_DIM:(g + 1) * HEAD_DIM]
        v = kv[:, LANES + g * HEAD_DIM:LANES + (g + 1) * HEAD_DIM]
        heads = range(g * group, (g + 1) * group)
        qs = jnp.concatenate([q[:, hh * HEAD_DIM:(hh + 1) * HEAD_DIM] for hh in heads], axis=0)
        s = _dot_nt(qs, k)
        ps, ls = [], []
        for r, hh in enumerate(heads):
            sr = s[r * BLK:(r + 1) * BLK] + mask
            sink = sink_ref[hh]
            m = jnp.maximum(jnp.max(sr, axis=-1, keepdims=True), sink)
            p = jnp.exp(sr - m)
            ls.append(jnp.sum(p, axis=-1, keepdims=True) + jnp.exp(sink - m))
            ps.append(p.astype(BF16))
        o = _dot(jnp.concatenate(ps, axis=0), v)
        for r, hh in enumerate(heads):
            o_ref[0, :, hh * HEAD_DIM:(hh + 1) * HEAD_DIM] = (o[r * BLK:(r + 1) * BLK] / ls[r]).astype(BF16)


def _attn_a_mask(lctx):
    i = np.arange(BLK)[:, None]
    j = np.arange(BLK)[None, :]
    ok_prev = (j >= i)
    ok_next = (j <= i)
    yes = np.ones((BLK, BLK), bool)
    no = np.zeros((BLK, BLK), bool)
    ctx = np.ones((BLK, lctx), bool)
    variants = [
        np.concatenate([ok_prev, yes, ok_next, ctx], 1),
        np.concatenate([no, yes, ok_next, ctx], 1),
        np.concatenate([ok_prev, yes, no, ctx], 1),
        np.concatenate([no, no, no, ctx], 1),
    ]
    return jnp.asarray(np.where(np.stack(variants), 0.0, NEG_INF).astype(np.float32))


def _attn_a(qa, kva, sink, lctx):
    nb, tt, _ = qa.shape
    nblk = tt // BLK
    lb = lctx // BLK
    assert nblk - lb >= 2
    mask = _attn_a_mask(lctx)

    def variant(t):
        return jnp.where(t < lb, 3, jnp.where(t == lb, 1, jnp.where(t == nblk - 1, 2, 0)))

    kvb = lambda f: pl.BlockSpec((1, BLK, 256), lambda b, t: (b, f(t), 0))
    return pl.pallas_call(
        _attn_a_kernel,
        grid=(nb, nblk),
        in_specs=[
            pl.BlockSpec(memory_space=pltpu.SMEM),
            pl.BlockSpec((1, BLK, 512), lambda b, t: (b, t, 0)),
            kvb(lambda t: jnp.maximum(t - 1, 0)),
            kvb(lambda t: t),
            kvb(lambda t: jnp.minimum(t + 1, nblk - 1)),
            pl.BlockSpec((1, lctx, 256), lambda b, t: (b, 0, 0)),
            pl.BlockSpec((1, BLK, 3 * BLK + lctx), lambda b, t: (variant(t), 0, 0)),
        ],
        out_specs=pl.BlockSpec((1, BLK, 512), lambda b, t: (b, t, 0)),
        out_shape=jax.ShapeDtypeStruct((nb, tt, 512), BF16),
        compiler_params=_cparams(("parallel", "parallel")),
        name="attn_window",
    )(sink, qa, kva, kva, kva, kva, mask)


NB_KBLK = 5


def _attn_b_kernel(q_ref, k0, k1, k2, k3, k4, kx, v0, v1, v2, v3, v4, vx, bias_ref, o_ref, *, lb):
    nloc = NB_KBLK * BLK
    first = lax.broadcasted_iota(jnp.int32, (1, LANES), 1) < HEAD_DIM

    def attend(k, v, with_bias):
        q = q_ref[0]
        for pair in range(B_HEADS // 2):
            sl = slice(pair * LANES, (pair + 1) * LANES)
            q2, k2, v2 = q[:, sl], k[:, sl], v[:, sl]
            zero = jnp.zeros_like(q2)
            s2 = _dot_nt(jnp.concatenate([jnp.where(first, q2, zero), jnp.where(first, zero, q2)], axis=0), k2)
            ps, ls = [], []
            for j in range(2):
                s = s2[j * BLK:(j + 1) * BLK]
                if with_bias:
                    s = jnp.concatenate([s[:, :nloc] + bias_ref[0, 0, 2 * pair + j], s[:, nloc:]], axis=1)
                p = jnp.exp(s - jnp.max(s, axis=-1, keepdims=True))
                ls.append(jnp.sum(p, axis=-1, keepdims=True))
                ps.append(p.astype(BF16))
            o2 = _dot(jnp.concatenate(ps, axis=0), v2)
            o_ref[0, :, sl] = jnp.where(first, o2[:BLK] / ls[0], o2[BLK:] / ls[1]).astype(BF16)

    is_ctx = pl.program_id(1) < lb

    @pl.when(is_ctx)
    def _():
        attend(kx[0], vx[0], False)

    @pl.when(jnp.logical_not(is_ctx))
    def _():
        attend(jnp.concatenate([k0[0], k1[0], k2[0], k3[0], k4[0], kx[0]], axis=0),
               jnp.concatenate([v0[0], v1[0], v2[0], v3[0], v4[0], vx[0]], axis=0), True)


def _attn_b_bias(rpb, nlat):
    rows = 2 * nlat
    wh = min(NA_ROWS, rows)
    n = GRID_W
    qrows, krows = BLK // n, NB_KBLK * BLK // n
    nvar = NB_KBLK
    edge = n - NA_COLS
    g = jnp.concatenate([jnp.repeat(rpb[..., :1], edge, axis=-1), rpb.astype(F32),
                         jnp.repeat(rpb[..., -1:], edge + 1, axis=-1)], axis=-1)
    g = jnp.roll(g, -(n - 1), axis=-1)
    toep = jnp.tile(g, (1, 1, 1, n))[..., :n * (2 * n - 1)].reshape(g.shape[:3] + (n, 2 * n - 1))[..., :n]

    reps = [0, 1, 2, nlat - 2, nlat - 1]
    i = np.arange(BLK)
    kk = np.arange(NB_KBLK * BLK)
    ok, dr = [], []
    for dlt, j in enumerate(reps):
        base = j - dlt
        r = 2 * j + i // n
        qc = i % n
        kr = 2 * base + kk // n
        kc = kk % n
        start = np.clip(r - wh // 2, 0, rows - wh)
        row_ok = (kr[None, :] >= start[:, None]) & (kr[None, :] < start[:, None] + wh)
        c0 = np.clip(qc - NA_COLS // 2, 0, n - NA_COLS)
        col_ok = (kc[None, :] >= c0[:, None]) & (kc[None, :] < c0[:, None] + NA_COLS)
        ok.append(row_ok & col_ok)
        dr.append(np.clip(2 * (base - j) + np.arange(krows)[None, :] - np.arange(qrows)[:, None] + NA_ROWS - 1,
                          0, 2 * NA_ROWS - 2))
    depth, nh = rpb.shape[:2]
    bias = jnp.stack([jnp.concatenate([jnp.concatenate([toep[:, :, dr[v][a, b]] for b in range(krows)], axis=-1)
                                       for a in range(qrows)], axis=-2) for v in range(nvar)], axis=1)
    return jnp.where(jnp.asarray(np.stack(ok))[None, :, None], bias, NEG_INF)


def _attn_b(qb, kb, vb, bias, layer, lctx):
    nb, tt, _ = qb.shape
    nblk = tt // BLK
    lb = lctx // BLK
    nlat = nblk - lb

    def base(t):
        return jnp.clip(t - lb - 2, 0, nlat - NB_KBLK) + lb

    def variant(t):
        return jnp.where(t < lb, 0, t - base(t))

    loc = lambda i: pl.BlockSpec((1, BLK, 512), lambda b, t: (b, base(t) + i, 0))
    ctx = pl.BlockSpec((1, lctx, 512), lambda b, t: (b, 0, 0))
    return pl.pallas_call(
        functools.partial(_attn_b_kernel, lb=lb),
        grid=(nb, nblk),
        in_specs=[pl.BlockSpec((1, BLK, 512), lambda b, t: (b, t, 0))]
        + [loc(i) for i in range(NB_KBLK)] + [ctx]
        + [loc(i) for i in range(NB_KBLK)] + [ctx]
        + [pl.BlockSpec((1, 1, B_HEADS, BLK, NB_KBLK * BLK), lambda b, t: (layer, variant(t), 0, 0, 0))],
        out_specs=pl.BlockSpec((1, BLK, 512), lambda b, t: (b, t, 0)),
        out_shape=jax.ShapeDtypeStruct((nb, tt, 512), BF16),
        compiler_params=_cparams(("parallel", "parallel")),
        name="attn_neighbourhood",
    )(qb, *([kb] * (NB_KBLK + 1)), *([vb] * (NB_KBLK + 1)), bias)


HALO = 16


SUB = 8


NCB = C_CH // LANES


def _conv_kernel(prev_ref, cur_ref, next_ref, w_ref, b_ref, g_ref, bb_ref, o_ref, sh_ref, acc_ref, *, lb, nblk):
    t = pl.program_id(1)
    has_prev = jnp.logical_and(t != 0, t != lb)
    has_next = jnp.logical_and(t != lb - 1, t != nblk - 1)
    prev = jnp.where(has_prev, prev_ref[0], 0.0)
    nxt = jnp.where(has_next, next_ref[0], 0.0)
    cur = cur_ref[0]
    for cb in range(NCB):
        cs = slice(cb * LANES, (cb + 1) * LANES)
        sh_ref[0, cb, 0:HALO, :] = prev[:, cs]
        sh_ref[0, cb, HALO:HALO + BLK, :] = cur[:, cs]
        sh_ref[0, cb, HALO + BLK:2 * HALO + BLK, :] = nxt[:, cs]
    rows = BLK + 2 * HALO - SUB
    pad = C_KSIZE // 2

    def channel_block(cb, carry):
        for r in range(1, SUB):
            sh_ref[r, cb, 0:rows, :] = sh_ref[0, cb, pl.ds(r, rows), :]
        acc = jnp.zeros((BLK // SUB, SUB, LANES), F32)
        for kk in range(C_KSIZE):
            off = HALO - pad + kk
            rows_k = sh_ref[off % SUB, cb, off - off % SUB:off - off % SUB + BLK, :]
            acc = acc + rows_k.reshape(BLK // SUB, SUB, LANES) * w_ref[kk, cb][None]
        acc_ref[cb] = acc.reshape(BLK, LANES)
        return carry

    lax.fori_loop(0, NCB, channel_block, 0)
    y = jnp.concatenate([acc_ref[cb] for cb in range(NCB)], axis=1)
    y = _ln(y + b_ref[...]) * g_ref[...] + bb_ref[...]
    o_ref[0] = (y * jax.nn.sigmoid(y)).astype(BF16)


def _conv(yc, w, b, g, bb, lctx):
    nb, tt, ch = yc.shape
    nblk = tt // BLK
    lb = lctx // BLK
    per = BLK // HALO
    vec = pl.BlockSpec((1, ch), lambda b_, t: (0, 0))
    return pl.pallas_call(
        functools.partial(_conv_kernel, lb=lb, nblk=nblk),
        grid=(nb, nblk),
        in_specs=[
            pl.BlockSpec((1, HALO, ch), lambda b_, t: (b_, jnp.maximum(t * per - 1, 0), 0)),
            pl.BlockSpec((1, BLK, ch), lambda b_, t: (b_, t, 0)),
            pl.BlockSpec((1, HALO, ch), lambda b_, t: (b_, jnp.minimum((t + 1) * per, nblk * per - 1), 0)),
            pl.BlockSpec((C_KSIZE, NCB, SUB, LANES), lambda b_, t: (0, 0, 0, 0)),
            vec, vec, vec,
        ],
        out_specs=pl.BlockSpec((1, BLK, ch), lambda b_, t: (b_, t, 0)),
        out_shape=jax.ShapeDtypeStruct((nb, tt, ch), BF16),
        scratch_shapes=[pltpu.VMEM((SUB, NCB, BLK + 2 * HALO, LANES), F32), pltpu.VMEM((NCB, BLK, LANES), F32)],
        compiler_params=_cparams(("parallel", "parallel")),
        name="conformer_conv",
    )(yc, yc, yc, jnp.broadcast_to(w.reshape(C_KSIZE, NCB, 1, LANES), (C_KSIZE, NCB, SUB, LANES)),
      b.reshape(1, ch), g.reshape(1, ch),
      bb.reshape(1, ch))


def _mlstm_kernel(qf_ref, kf_ref, vf_ref, gcf_ref, grf_ref, qb_ref, kb_ref, vb_ref, gcb_ref, grb_ref,
                  hf_ref, hb_ref, c_st, n_st, m_st, *, mb):
    @pl.when(pl.program_id(1) == 0)
    def _():
        c_st[...] = jnp.zeros_like(c_st)
        n_st[...] = jnp.zeros_like(n_st)
        m_st[...] = jnp.zeros_like(m_st)

    rr = lax.broadcasted_iota(jnp.int32, (BLK, BLK), 0)
    cc = lax.broadcasted_iota(jnp.int32, (BLK, BLK), 1)
    ch = []
    dirs = ((qf_ref, kf_ref, vf_ref, gcf_ref, grf_ref, hf_ref), (qb_ref, kb_ref, vb_ref, gcb_ref, grb_ref, hb_ref))
    for bb, dirn in [(bb, dirn) for bb in range(mb) for dirn in range(2)]:
        q_ref, k_ref, v_ref, gc_ref, gr_ref, h_ref = dirs[dirn]
        before = (rr >= cc) if dirn == 0 else (rr <= cc)
        bmat = jnp.where(before, 1.0, 0.0).astype(BF16)
        gc = gc_ref[0, bb]
        gr = gr_ref[0, bb]
        lf_c = jax.nn.log_sigmoid(gc[:, M_HEADS:2 * M_HEADS])
        lf_r = jax.nn.log_sigmoid(gr[M_HEADS:2 * M_HEADS, :])
        fc = sum(_dot(bmat, part) for part in _split3(lf_c))
        fr = sum(_dot_nt(part, bmat) for part in _split3(lf_r))
        f_tot = jnp.sum(lf_r, axis=-1, keepdims=True)
        for hh in range(M_HEADS):
            sl = slice(hh * M_DIM, (hh + 1) * M_DIM)
            st = (bb * 2 + dirn) * M_HEADS + hh
            ch.append(dict(before=before, sl=sl, st=st, h_ref=h_ref, bb=bb,
                           q=q_ref[bb, :, sl], k=k_ref[bb, :, sl], v=v_ref[bb, :, sl],
                           f_c=fc[:, hh:hh + 1], f_r=fr[hh:hh + 1, :], f_tot=f_tot[hh:hh + 1, :],
                           i_c=gc[:, hh:hh + 1], i_r=gr[hh:hh + 1, :],
                           c_old=c_st[st], n_old=n_st[st], m_old=m_st[st][:, 0:1]))

    for c in ch:
        c["qk"] = _dot_nt(c["q"], c["k"])
        c["qc"] = _dot(c["q"], c["c_old"].astype(BF16))
    for c in ch:
        c["a"] = c["f_c"] + c["m_old"]
        c["logw"] = jnp.where(c["before"], c["f_c"] - c["f_r"] + c["i_r"], -jnp.inf)
        c["mt"] = jnp.maximum(c["a"], jnp.max(c["logw"], axis=-1, keepdims=True))
    for c in ch:
        g_r = c["f_tot"] - c["f_r"] + c["i_r"]
        c["m_new"] = jnp.maximum(c["f_tot"] + c["m_old"], jnp.max(g_r, axis=-1, keepdims=True))
        c["decay"] = jnp.exp(c["f_tot"] + c["m_old"] - c["m_new"])
        c["kw"] = c["k"].astype(F32) * jnp.exp(c["f_tot"] - c["f_c"] + c["i_c"] - c["m_new"])
    for c in ch:
        c["s"] = c["qk"] * jnp.exp(c["logw"] - c["mt"])
        c["w_inter"] = jnp.exp(c["a"] - c["mt"])
    for c in ch:
        c["sv"] = _dot(c["s"].astype(BF16), c["v"])
        c["kv"] = lax.dot_general(c["kw"].astype(BF16), c["v"], (((0,), (0,)), ((), ())), preferred_element_type=F32)
    for c in ch:
        num = c["w_inter"] * c["qc"] + c["sv"]
        den = (c["w_inter"] * jnp.sum(c["q"].astype(F32) * c["n_old"], axis=-1, keepdims=True)
               + jnp.sum(c["s"], axis=-1, keepdims=True))
        hout = num / jnp.maximum(jnp.abs(den), jnp.exp(-c["mt"]))
        c["h_ref"][c["bb"], :, c["sl"]] = hout.astype(BF16)
    for c in ch:
        st = c["st"]
        c_st[st] = c["decay"] * c["c_old"] + c["kv"]
        n_st[st] = c["decay"] * c["n_old"] + jnp.sum(c["kw"], axis=0, keepdims=True)
        m_st[st] = jnp.broadcast_to(c["m_new"], (1, LANES))


def _mlstm(qd, kd, vd, gates, lctx):
    nb, tt, _ = qd.shape
    nblk = tt // BLK
    lb = lctx // BLK
    gcol = jnp.stack([gates[..., 0:8], gates[..., 8:16]])
    grow = jnp.swapaxes(gcol, 2, 3)

    def bwd(i):
        return jnp.where(i < lb, lb - 1 - i, nblk - 1 + lb - i)

    fwd = lambda i: i
    mb = max(m for m in (4, 2, 1) if nb % m == 0)
    tok = lambda f: pl.BlockSpec((mb, BLK, 512), lambda b, i: (b, f(i), 0))
    gcs = lambda d, f: pl.BlockSpec((1, mb, BLK, 8), lambda b, i: (d, b, f(i), 0))
    grs = lambda d, f: pl.BlockSpec((1, mb, 8, BLK), lambda b, i: (d, b, 0, f(i)))
    out = jax.ShapeDtypeStruct((nb, tt, 512), BF16)
    nst = 2 * mb * M_HEADS
    return pl.pallas_call(
        functools.partial(_mlstm_kernel, mb=mb),
        grid=(nb // mb, nblk),
        in_specs=[tok(fwd), tok(fwd), tok(fwd), gcs(0, fwd), grs(0, fwd),
                  tok(bwd), tok(bwd), tok(bwd), gcs(1, bwd), grs(1, bwd)],
        out_specs=[tok(fwd), tok(bwd)],
        out_shape=[out, out],
        scratch_shapes=[pltpu.VMEM((nst, M_DIM, M_DIM), F32),
                        pltpu.VMEM((nst, 1, M_DIM), F32),
                        pltpu.VMEM((nst, 1, LANES), F32)],
        compiler_params=_cparams(("parallel", "arbitrary")),
        name="mlstm",
    )(qd, kd, vd, gcol, grow, qd, kd, vd, gcol, grow)


def _merge_kernel(x_ref, mod_ref, ya_ref, yb_ref, yc_ref, hf_ref, hb_ref, so_ref,
                  wg_ref, bg_ref, wbr_ref, wo_ref, g1_ref, b1_ref, wr_ref,
                  x1_o, hp_o, aff_o, *, alpha):
    d = D_MODEL
    x = x_ref[0]
    mod = mod_ref[0]
    h = (_ln(x) * (1.0 + mod[:, d:2 * d]) + mod[:, 0:d]).astype(BF16)
    yd = (so_ref[0].astype(F32) * (hf_ref[0].astype(F32) + hb_ref[0].astype(F32))).astype(BF16)
    ys = (ya_ref[0], yb_ref[0], yc_ref[0], yd)
    pre = [_dot(h, wg_ref[0, :, i * d:(i + 1) * d]) for i in range(N_BRANCH)]
    br = [_dot(ys[i], wbr_ref[0, i]) for i in range(N_BRANCH)]
    z = None
    for i in range(N_BRANCH):
        term = jax.nn.sigmoid(pre[i] + bg_ref[:, i * d:(i + 1) * d]) * br[i]
        z = term if z is None else z + term
    y = _dot(z.astype(BF16), wo_ref[0])
    x1 = _ln(alpha * x + mod[:, 2 * d:3 * d] * y) * g1_ref[...] + b1_ref[...]
    x1_o[0] = x1

    h2 = _ln(x1) * (1.0 + mod[:, 4 * d:5 * d]) + mod[:, 3 * d:4 * d]
    hb16 = h2.astype(BF16)
    bits = pltpu.bitcast(hb16.astype(F32), jnp.uint32)
    hp_o[0] = (bits[:, d // 2:] & jnp.uint32(0xFFFF0000)) | (bits[:, :d // 2] >> 16)

    h2_hi, h2_lo = _split2(h2)
    r_hi = _dot(h2_hi, wr_ref[...])
    r_lo = _dot(h2_lo, wr_ref[...])
    logits = r_hi[:, 0:N_EXPERTS] + (r_hi[:, N_EXPERTS:2 * N_EXPERTS] + r_lo[:, 0:N_EXPERTS])
    e = jnp.exp(logits - jnp.max(logits, axis=-1, keepdims=True))
    aff_o[0] = e / jnp.sum(e, axis=-1, keepdims=True)


def _merge(x, mod3, ya, yb, yc, hf, hb, so, wg, bg, wbr, wo, layer, g1, b1, wr, lctx, alpha):
    nb, tt, d = x.shape
    nt = tt // TM
    lt = lctx // TM
    tok = lambda n: pl.BlockSpec((1, TM, n), lambda b, t: (b, t, 0))
    const = lambda shape: pl.BlockSpec(shape, lambda b, t: (0,) * len(shape))
    stacked = lambda shape: pl.BlockSpec((1,) + shape, lambda b, t: (layer,) + (0,) * len(shape))
    return pl.pallas_call(
        functools.partial(_merge_kernel, alpha=alpha),
        grid=(nb, nt),
        in_specs=[
            tok(d),
            pl.BlockSpec((1, 1, 6 * d), _mod_index(lt, nb)),
            tok(512), tok(512), tok(512), tok(512), tok(512), tok(512),
            stacked((d, N_BRANCH * d)), const((1, N_BRANCH * d)), stacked((N_BRANCH, BRANCH_W, d)),
            stacked((d, d)), const((1, d)), const((1, d)), const((d, LANES)),
        ],
        out_specs=[tok(d), tok(d // 2), tok(N_EXPERTS)],
        out_shape=[jax.ShapeDtypeStruct((nb, tt, d), F32),
                   jax.ShapeDtypeStruct((nb, tt, d // 2), jnp.uint32),
                   jax.ShapeDtypeStruct((nb, tt, N_EXPERTS), F32)],
        compiler_params=_cparams(("parallel", "parallel"), 56),
        name="merge",
    )(x, mod3, ya, yb, yc, hf, hb, so, wg, bg.reshape(1, -1), wbr, wo,
      g1.reshape(1, d), b1.reshape(1, d), wr)


def _route_one(a_ref, ones_ref, tab_ref, idx_o, val_o, *, ntok, cap, tok_off, slot_off):
    a = a_ref[0]
    bits = pltpu.bitcast(a, jnp.int32)
    thr = jnp.zeros((N_EXPERTS, 1), jnp.int32)
    for bit in range(30, -1, -1):
        cand = thr | jnp.int32(1 << bit)
        cnt = jnp.sum(jnp.where(bits >= cand, 1.0, 0.0), axis=-1, keepdims=True)
        thr = jnp.where(cnt >= cap, cand, thr)
    gt = bits > thr
    eq = bits == thr
    need = cap - jnp.sum(jnp.where(gt, 1.0, 0.0), axis=-1, keepdims=True)

    rr = lax.broadcasted_iota(jnp.int32, (LANES, LANES), 0)
    cc = lax.broadcasted_iota(jnp.int32, (LANES, LANES), 1)
    upper = jnp.where(rr <= cc, 1.0, 0.0).astype(BF16)

    def cumsum_blocks(mask_f):
        run = jnp.zeros((N_EXPERTS, 1), F32)
        out = []
        for c in range(ntok // LANES):
            blk = mask_f[:, c * LANES:(c + 1) * LANES]
            out.append(_dot(blk.astype(BF16), upper) + run)
            run = run + jnp.sum(blk, axis=-1, keepdims=True)
        return out

    eq_f = jnp.where(eq, 1.0, 0.0)
    cum_eq = cumsum_blocks(eq_f)
    sel_parts = []
    for c in range(ntok // LANES):
        sl = slice(c * LANES, (c + 1) * LANES)
        sel_parts.append(jnp.where(gt[:, sl] | (eq[:, sl] & (cum_eq[c] <= need)), 1.0, 0.0))
    sel_f = jnp.concatenate(sel_parts, axis=1)
    cpos = cumsum_blocks(sel_f)
    nblk = ntok // LANES
    tab_ref[...] = jnp.zeros_like(tab_ref)
    for c in range(nblk):
        for e in range(N_EXPERTS):
            tab_ref[0, e, c:c + 1, :] = cpos[c][e:e + 1, :]
            tab_ref[1, e, c:c + 1, :] = a_ref[0, e:e + 1, c * LANES:(c + 1) * LANES]
    counts = _dot(sel_f.astype(BF16), ones_ref[...])
    blockend = _dot(counts.astype(BF16), upper)
    prevend = blockend - counts

    lane = lax.broadcasted_iota(jnp.int32, (1, LANES), 1).astype(F32)
    slot = lax.broadcasted_iota(jnp.int32, (cap, 1), 0).astype(F32)
    for e in range(N_EXPERTS):
        be = blockend[e:e + 1, :]
        pe = prevend[e:e + 1, :]
        pick = jnp.where(pe <= slot, jnp.where(slot < be, 1.0, 0.0), 0.0).astype(BF16)
        cnt_blk = sum(_dot(pick, part) for part in _split2(tab_ref[0, e]))
        aff_blk = sum(_dot(pick, part) for part in _split3(tab_ref[1, e]))
        within = jnp.sum(jnp.where(cnt_blk <= slot, 1.0, 0.0), axis=-1, keepdims=True)
        nfull = jnp.sum(jnp.where(be <= slot, 1.0, 0.0), axis=-1, keepdims=True)
        idx_o[0, e, slot_off:slot_off + cap, :] = (nfull * LANES + within + tok_off).astype(jnp.int32)
        val_o[0, e, slot_off:slot_off + cap, :] = jnp.sum(jnp.where(lane == within, aff_blk, 0.0),
                                                           axis=-1, keepdims=True)


def _route_kernel(al_ref, ac_ref, onesl_ref, onesc_ref, idx_o, val_o, tab_ref, *, s, lctx, cap_l, cap_c):
    _route_one(al_ref, onesl_ref, tab_ref, idx_o, val_o, ntok=s, cap=cap_l, tok_off=lctx, slot_off=0)
    _route_one(ac_ref, onesc_ref, tab_ref, idx_o, val_o, ntok=lctx, cap=cap_c, tok_off=0, slot_off=cap_l)


def _route(aff, lctx):
    nb, tt, ne = aff.shape
    s = tt - lctx
    cap_l = CAPACITY_FACTOR * s // ne
    cap_c = CAPACITY_FACTOR * lctx // ne
    capt = cap_l + cap_c
    assert s // LANES <= LANES and cap_l % 8 == 0 and cap_c % 8 == 0
    aff_t = jnp.swapaxes(aff, 1, 2)
    ones = lambda n: jnp.asarray(np.arange(n)[:, None] // LANES == np.arange(LANES)[None, :], BF16)
    out = lambda dt: jax.ShapeDtypeStruct((nb, ne, capt, 1), dt)
    return pl.pallas_call(
        functools.partial(_route_kernel, s=s, lctx=lctx, cap_l=cap_l, cap_c=cap_c),
        grid=(nb,),
        in_specs=[pl.BlockSpec((1, ne, s), lambda b: (b, 0, 0)),
                  pl.BlockSpec((1, ne, lctx), lambda b: (b, 0, 0)),
                  pl.BlockSpec((s, LANES), lambda b: (0, 0)),
                  pl.BlockSpec((lctx, LANES), lambda b: (0, 0))],
        out_specs=[pl.BlockSpec((1, ne, capt, 1), lambda b: (b, 0, 0, 0))] * 2,
        out_shape=[out(jnp.int32), out(F32)],
        scratch_shapes=[pltpu.VMEM((2, ne, LANES, LANES), F32)],
        compiler_params=_cparams(("parallel",), 40),
        name="route",
    )(aff_t[:, :, lctx:], aff_t[:, :, :lctx], ones(s), ones(lctx))


def _gather_kernel(idx_ref, x_ref, o_ref, *, capt):
    def body(i, carry):
        c0 = pl.multiple_of(i * SUB, SUB)
        rows = [x_ref[0, pl.ds(idx_ref[0, 0, c0 + j], 1), :] for j in range(SUB)]
        o_ref[0, 0, pl.ds(c0, SUB), :] = jnp.concatenate(rows, axis=0)
        return carry
    lax.fori_loop(0, capt // SUB, body, 0)


def _gather(idx_s, hp, capt):
    nb, tt, w = hp.shape
    assert capt % SUB == 0
    ne = N_EXPERTS
    return pl.pallas_call(
        functools.partial(_gather_kernel, capt=capt),
        grid=(nb, ne),
        in_specs=[pl.BlockSpec((1, 1, capt), lambda b, e: (b * ne + e, 0, 0), memory_space=pltpu.SMEM),
                  pl.BlockSpec((1, tt, w), lambda b, e: (b, 0, 0))],
        out_specs=pl.BlockSpec((1, 1, capt, w), lambda b, e: (b, e, 0, 0)),
        out_shape=jax.ShapeDtypeStruct((nb, ne, capt, w), jnp.uint32),
        compiler_params=_cparams(("parallel", "arbitrary"), 40),
        name="moe_gather",
    )(idx_s, hp)


FF_CHUNK = 256


def _ffn_kernel(x_ref, w1_ref, w3_ref, w2_ref, val_ref, y_ref, w1b, w3b, w2b):
    @pl.when(pl.program_id(1) == 0)
    def _():
        w1b[...] = w1_ref[0, 0].astype(BF16)
        w3b[...] = w3_ref[0, 0].astype(BF16)
        w2b[...] = w2_ref[0, 0].astype(BF16)

    packed = x_ref[0, 0]
    lo = pltpu.bitcast(packed << 16, F32)
    hi = pltpu.bitcast(packed & jnp.uint32(0xFFFF0000), F32)
    xg = jnp.concatenate([lo, hi], axis=1).astype(BF16)
    y = None
    for c in range(EXPERT_FF // FF_CHUNK):
        cs = slice(c * FF_CHUNK, (c + 1) * FF_CHUNK)
        a = _dot(xg, w1b[:, cs])
        hid = (a * jax.nn.sigmoid(a) * _dot(xg, w3b[:, cs])).astype(BF16)
        part = _dot(hid, w2b[cs, :])
        y = part if y is None else y + part
    y_ref[0, 0] = y * val_ref[0, 0]


def _ffn(xg, w1, w3, w2, vals, layer):
    nb, ne, capt, w = xg.shape
    d, ff = w1.shape[2], w1.shape[3]
    return pl.pallas_call(
        _ffn_kernel,
        grid=(ne, nb),
        in_specs=[pl.BlockSpec((1, 1, capt, w), lambda e, b: (b, e, 0, 0)),
                  pl.BlockSpec((1, 1, d, ff), lambda e, b: (layer, e, 0, 0)),
                  pl.BlockSpec((1, 1, d, ff), lambda e, b: (layer, e, 0, 0)),
                  pl.BlockSpec((1, 1, ff, d), lambda e, b: (layer, e, 0, 0)),
                  pl.BlockSpec((1, 1, capt, 1), lambda e, b: (b, e, 0, 0))],
        out_specs=pl.BlockSpec((1, 1, capt, d), lambda e, b: (b, e, 0, 0)),
        out_shape=jax.ShapeDtypeStruct((nb, ne, capt, d), F32),
        scratch_shapes=[pltpu.VMEM((d, ff), BF16), pltpu.VMEM((d, ff), BF16), pltpu.VMEM((ff, d), BF16)],
        compiler_params=_cparams(("parallel", "arbitrary"), 56),
        name="moe_ffn",
    )(xg, w1, w3, w2, vals)


SCATTER_GROUP = 8


def _scatter_kernel(idx_ref, y_ref, o_ref, *, capt):
    @pl.when(pl.program_id(1) == 0)
    def _():
        o_ref[...] = jnp.zeros_like(o_ref)

    def body(i, carry):
        c0 = pl.multiple_of(i * SCATTER_GROUP, SCATTER_GROUP)
        rows = [pl.ds(idx_ref[0, 0, c0 + j], 1) for j in range(SCATTER_GROUP)]
        acc = [o_ref[0, r, :] for r in rows]
        ytile = y_ref[0, 0, pl.ds(c0, SCATTER_GROUP), :]
        new = [a + ytile[j:j + 1, :] for j, a in enumerate(acc)]
        for r, v in zip(rows, new):
            o_ref[0, r, :] = v
        return carry
    lax.fori_loop(0, capt // SCATTER_GROUP, body, 0)


def _scatter(idx_s, y, tt):
    nb, ne, capt, d = y.shape
    assert capt % SCATTER_GROUP == 0
    return pl.pallas_call(
        functools.partial(_scatter_kernel, capt=capt),
        grid=(nb, ne),
        in_specs=[pl.BlockSpec((1, 1, capt), lambda b, e: (b * ne + e, 0, 0), memory_space=pltpu.SMEM),
                  pl.BlockSpec((1, 1, capt, d), lambda b, e: (b, e, 0, 0))],
        out_specs=pl.BlockSpec((1, tt, d), lambda b, e: (b, 0, 0)),
        out_shape=jax.ShapeDtypeStruct((nb, tt, d), F32),
        compiler_params=_cparams(("parallel", "arbitrary"), 56),
        name="moe_scatter",
    )(idx_s, y)


def _post_kernel(x_ref, ml_ref, mod_ref, g_ref, b_ref, o_ref, *, alpha):
    d = D_MODEL
    g2 = mod_ref[0][:, 5 * d:6 * d]
    o_ref[0] = _ln(alpha * x_ref[0] + g2 * ml_ref[0]) * g_ref[...] + b_ref[...]


def _post(x1, ml, mod3, g, b, lctx, alpha):
    nb, tt, d = x1.shape
    lt = lctx // TM
    t0 = lt
    tok = pl.BlockSpec((1, TM, d), lambda b_, t: (b_, t + t0, 0))
    vec = pl.BlockSpec((1, d), lambda b_, t: (0, 0))
    return pl.pallas_call(
        functools.partial(_post_kernel, alpha=alpha),
        grid=(nb, tt // TM - t0),
        in_specs=[tok, tok, pl.BlockSpec((1, 1, 6 * d), lambda b_, t: (jnp.where(t + t0 < lt, nb, b_), 0, 0)),
                  vec, vec],
        out_specs=pl.BlockSpec((1, TM, d), lambda b_, t: (b_, t, 0)),
        out_shape=jax.ShapeDtypeStruct((nb, tt - t0 * TM, d), F32),
        compiler_params=_cparams(("parallel", "parallel")),
        name="moe_post",
    )(x1, ml, mod3, g.reshape(1, d), b.reshape(1, d))


def _rope_tables(s, lctx):
    half = HEAD_DIM // 2
    nf = half // 2
    inv = ROPE_BASE ** (-jnp.arange(nf, dtype=F32) / nf)
    t = jnp.arange(s)
    lane = np.arange(LANES)
    dd = lane % HEAD_DIM
    use_col = jnp.asarray(dd >= half)[None, :]
    pos = jnp.where(use_col, (t % GRID_W)[:, None], (t // GRID_W)[:, None]).astype(F32)
    ang = pos * inv[jnp.asarray(dd % nf)][None, :]
    cos, sin = jnp.cos(ang), jnp.sin(ang)
    first = jnp.asarray((dd % half) < nf)[None, :]
    sa = jnp.where(first, -sin, 0.0)
    sb = jnp.where(first, 0.0, sin)
    pad = lambda a, v: jnp.concatenate([jnp.full((lctx, LANES), v, F32), a], axis=0)
    return pad(cos, 1.0), pad(sa, 0.0), pad(sb, 0.0)


def kernel(x, c, ctx, c_ctx, w_mod, b_mod, w_in, attn_sink, na_rpb, conv_w, conv_b, conv_ln_g, conv_ln_b,
           mlstm_gate_b, w_branch, w_gate, b_gate, w_out, ln1_g, ln1_b, w_router, w_exp_gate, w_exp_up,
           w_exp_down, ln2_g, ln2_b):
    nb, s, d = x.shape
    lctx = ctx.shape[1]
    depth = w_mod.shape[0]
    assert d == D_MODEL and nb + 1 <= 8 and lctx % TM == 0 and s % TM == 0 and s % GRID_W == 0
    alpha = (2.0 * depth) ** 0.25
    tt = lctx + s
    cap_t = CAPACITY_FACTOR * s // N_EXPERTS + CAPACITY_FACTOR * lctx // N_EXPERTS

    c8 = jnp.concatenate([c, c_ctx[None, :], jnp.zeros((8 - nb - 1, d), F32)], axis=0)
    mod_all = _modulation(c8, w_mod, b_mod)
    tabs = _rope_tables(s, lctx)
    xs = jnp.concatenate([ctx, x], axis=1)
    nlat = s // BLK
    assert nlat >= NB_KBLK
    bias_b = _attn_b_bias(na_rpb, nlat)
    w_in_bf = jnp.pad(w_in, ((0, 0), (0, 0), (0, PROJ_PAD - PROJ_W))).astype(BF16)
    w_gate_bf, w_branch_bf, w_out_bf = w_gate.astype(BF16), w_branch.astype(BF16), w_out.astype(BF16)

    for l in range(depth):
        mod3 = mod_all[l].reshape(8, 1, 6 * d)
        if l == 0:
            outs = _inproj(xs, mod3, tabs, w_in_bf, l, mlstm_gate_b[l], lctx)
        else:
            xs, *outs = _inproj(None, mod3, tabs, w_in_bf, l, mlstm_gate_b[l], lctx, prev=prev)
        (qa, kva, qb, kb, vb, yc0, qd, kd, vd, so, gates) = outs
        ya = _attn_a(qa, kva, attn_sink[l], lctx)
        yb = _attn_b(qb, kb, vb, bias_b, l, lctx)
        yc = _conv(yc0, conv_w[l], conv_b[l], conv_ln_g[l], conv_ln_b[l], lctx)
        hf, hb = _mlstm(qd, kd, vd, gates, lctx)
        wr_hi, wr_lo = _split2(w_router[l])
        wr = jnp.pad(jnp.concatenate([wr_hi, wr_lo], axis=1), ((0, 0), (0, LANES - 2 * N_EXPERTS)))
        x1, hp, aff = _merge(xs, mod3, ya, yb, yc, hf, hb, so, w_gate_bf, b_gate[l], w_branch_bf, w_out_bf, l,
                             ln1_g[l], ln1_b[l], wr, lctx, alpha)
        idx, vals = _route(aff, lctx)
        idx_s = idx.reshape(nb * N_EXPERTS, 1, cap_t)
        xg = _gather(idx_s, hp, cap_t)
        y = _ffn(xg, w_exp_gate, w_exp_up, w_exp_down, vals, l)
        ml = _scatter(idx_s, y, tt)
        prev = (x1, ml, mod3, ln2_g[l], ln2_b[l], alpha)
    return _post(x1, ml, mod3, ln2_g[depth - 1], ln2_b[depth - 1], lctx, alpha)
```

```python
import functools
import math

import numpy as np
import jax
import jax.numpy as jnp
from jax import lax
from jax.experimental import pallas as pl
from jax.experimental.pallas import tpu as pltpu

F32 = jnp.float32
BF16 = jnp.bfloat16

D_MODEL = 1024
GRID_W = 64
HEAD_DIM = 64
A_HEADS = 8
A_KV_HEADS = 2
A_WINDOW = 128
B_HEADS = 8
NA_ROWS = 8
NA_COLS = 16
C_CH = 512
C_KSIZE = 31
M_HEADS = 4
M_DIM = 128
N_BRANCH = 4
BRANCH_W = 512
N_EXPERTS = 16
EXPERT_FF = 1024
CAPACITY_FACTOR = 2
ROPE_BASE = 10000.0
LN_EPS = 1e-6
NEG_INF = -1e30

BLK = 128
TM = 256
LANES = 128
PROJ_W = 5392
PROJ_PAD = 5504
OFF_A, OFF_AKV, OFF_B, OFF_C, OFF_D, OFF_G = 0, 512, 768, 2304, 3328, 5376
MIB = 1 << 20


def _cparams(sem, vmem_mib=None):
    kw = dict(dimension_semantics=sem)
    if vmem_mib is not None:
        kw["vmem_limit_bytes"] = vmem_mib * MIB
    return pltpu.CompilerParams(**kw)


def _ln(x):
    mu = jnp.mean(x, axis=-1, keepdims=True)
    xc = x - mu
    var = jnp.mean(xc * xc, axis=-1, keepdims=True)
    return xc * lax.rsqrt(var + LN_EPS)


def _dot(a, b):
    return jnp.dot(a, b, preferred_element_type=F32)


def _dot_nt(a, b):
    return lax.dot_general(a, b, (((1,), (1,)), ((), ())), preferred_element_type=F32)


def _split2(x):
    hi = x.astype(BF16)
    lo = (x - hi.astype(F32)).astype(BF16)
    return hi, lo


def _split3(x):
    hi = x.astype(BF16)
    r = x - hi.astype(F32)
    mid = r.astype(BF16)
    lo = (r - mid.astype(F32)).astype(BF16)
    return hi, mid, lo


def _dot3(x, w):
    xh, xl = _split2(x)
    wh, wl = _split2(w)
    return _dot(xh, wh) + (_dot(xh, wl) + _dot(xl, wh))


def _mod_kernel(c_ref, w_ref, b_ref, o_ref):
    c = c_ref[...]
    s = c * jax.nn.sigmoid(c)
    o_ref[0] = _dot3(s, w_ref[0]) + b_ref[0]


def _modulation(c8, w_mod, b_mod):
    depth, d, d6 = w_mod.shape
    nj = d6 // d
    return pl.pallas_call(
        _mod_kernel,
        grid=(depth, nj),
        in_specs=[
            pl.BlockSpec((8, d), lambda l, j: (0, 0)),
            pl.BlockSpec((1, d, d), lambda l, j: (l, 0, j)),
            pl.BlockSpec((1, 1, d), lambda l, j: (l, 0, j)),
        ],
        out_specs=pl.BlockSpec((1, 8, d), lambda l, j: (l, 0, j)),
        out_shape=jax.ShapeDtypeStruct((depth, 8, d6), F32),
        compiler_params=_cparams(("arbitrary", "arbitrary"), 40),
        name="modulation",
    )(c8, w_mod, b_mod.reshape(depth, 1, d6))


def _rope(x, cos, sa, sb):
    parts = []
    for j in range(x.shape[1] // LANES):
        xj = x[:, j * LANES:(j + 1) * LANES]
        parts.append(xj * cos + pltpu.roll(xj, LANES - 16, 1) * sa + pltpu.roll(xj, 16, 1) * sb)
    return parts[0] if len(parts) == 1 else jnp.concatenate(parts, axis=1)


def _inproj_kernel(*refs, alpha, ns):
    d = D_MODEL
    R = range(ns)
    if alpha is None:
        x_ref, mod_ref, cos_ref, sa_ref, sb_ref, w3_ref, gb_ref = refs[:7]
        outs = refs[7:]
        xs = [x_ref[r] for r in R]
    else:
        x1_ref, ml_ref, pmod_ref, pg_ref, pb_ref, mod_ref, cos_ref, sa_ref, sb_ref, w3_ref, gb_ref = refs[:11]
        x_o = refs[11]
        outs = refs[12:]
        xs = [_ln(alpha * x1_ref[r] + pmod_ref[r][:, 5 * d:6 * d] * ml_ref[r]) * pg_ref[...] + pb_ref[...] for r in R]
        for r in R:
            x_o[r] = xs[r]
    qa_o, kva_o, qb_o, kb_o, vb_o, yc_o, qd_o, kd_o, vd_o, so_o, g_o = outs
    w_ref = w3_ref.at[0]
    hs = [(_ln(xs[r]) * (1.0 + mod_ref[r][:, d:2 * d]) + mod_ref[r][:, 0:d]).astype(BF16) for r in R]
    cos, sa, sb = cos_ref[...], sa_ref[...], sb_ref[...]
    qscale = HEAD_DIM ** -0.5

    for r in R:
        qa_o[r] = (_rope(_dot(hs[r], w_ref[:, OFF_A:OFF_AKV]), cos, sa, sb) * qscale).astype(BF16)
    for r in R:
        kva = _dot(hs[r], w_ref[:, OFF_AKV:OFF_B])
        kva_o[r, :, 0:LANES] = _rope(kva[:, 0:LANES], cos, sa, sb).astype(BF16)
        kva_o[r, :, LANES:2 * LANES] = kva[:, LANES:2 * LANES].astype(BF16)
    for r in R:
        qb_o[r] = (_dot(hs[r], w_ref[:, OFF_B:OFF_B + 512]) * qscale).astype(BF16)
    for r in R:
        kb_o[r] = _dot(hs[r], w_ref[:, OFF_B + 512:OFF_B + 1024]).astype(BF16)
    for r in R:
        vb_o[r] = _dot(hs[r], w_ref[:, OFF_B + 1024:OFF_C]).astype(BF16)
    for r in R:
        ua = _dot(hs[r], w_ref[:, OFF_C:OFF_C + C_CH])
        ug = _dot(hs[r], w_ref[:, OFF_C + C_CH:OFF_D])
        yc_o[r] = ua * jax.nn.sigmoid(ug)
    for r in R:
        qd_o[r] = _dot(hs[r], w_ref[:, OFF_D:OFF_D + 512]).astype(BF16)
    for r in R:
        kd_o[r] = (_dot(hs[r], w_ref[:, OFF_D + 512:OFF_D + 1024]) * (M_DIM ** -0.5)).astype(BF16)
    for r in R:
        vd_o[r] = _dot(hs[r], w_ref[:, OFF_D + 1024:OFF_D + 1536]).astype(BF16)
    for r in R:
        so_o[r] = jax.nn.sigmoid(_dot(hs[r], w_ref[:, OFF_D + 1536:OFF_G])).astype(BF16)
    for r in R:
        g_o[r] = _dot(hs[r], w_ref[:, OFF_G:PROJ_PAD])[:, 0:16] + gb_ref[...]


def _mod_groups(mod_l, nb, ns):
    d6 = mod_l.shape[-1]
    ng = nb // ns
    modg = jnp.concatenate([mod_l[:nb].reshape(ng, ns, 1, d6), jnp.broadcast_to(mod_l[nb], (1, ns, 1, d6))], axis=0)
    return modg.reshape((ng + 1) * ns, 1, d6)


def _inproj(x, mod_l, tabs, w_bf, layer, gate_b, lctx, prev=None):
    nb, tt, d = (x if prev is None else prev[0]).shape
    nt = tt // TM
    lt = lctx // TM
    ns = 2 if nb % 2 == 0 else 1
    ng = nb // ns
    tok = lambda n: pl.BlockSpec((ns, TM, n), lambda g, t: (g, t, 0))
    tab = pl.BlockSpec((TM, LANES), lambda g, t: (t, 0))
    vec = pl.BlockSpec((1, d), lambda g, t: (0, 0))
    modspec = pl.BlockSpec((ns, 1, 6 * d), lambda g, t: (jnp.where(t < lt, ng, g), 0, 0))
    sds = lambda n, dt: jax.ShapeDtypeStruct((nb, tt, n), dt)
    in_specs = [modspec, tab, tab, tab,
                pl.BlockSpec((1, d, PROJ_PAD), lambda g, t: (layer, 0, 0)),
                pl.BlockSpec((1, 16), lambda g, t: (0, 0))]
    args = [_mod_groups(mod_l, nb, ns), tabs[0], tabs[1], tabs[2], w_bf, gate_b.reshape(1, 16)]
    out_specs = [tok(512), tok(256), tok(512), tok(512), tok(512), tok(512),
                 tok(512), tok(512), tok(512), tok(512), tok(16)]
    out_shape = [sds(512, BF16), sds(256, BF16), sds(512, BF16), sds(512, BF16), sds(512, BF16),
                 sds(512, F32), sds(512, BF16), sds(512, BF16), sds(512, BF16), sds(512, BF16),
                 sds(16, F32)]
    if prev is None:
        in_specs = [tok(d)] + in_specs
        args = [x] + args
        alpha = None
    else:
        x1, ml, pmod_l, pg, pb, alpha = prev
        in_specs = [tok(d), tok(d), modspec, vec, vec] + in_specs
        args = [x1, ml, _mod_groups(pmod_l, nb, ns), pg.reshape(1, d), pb.reshape(1, d)] + args
        out_specs = [tok(d)] + out_specs
        out_shape = [sds(d, F32)] + out_shape
    return pl.pallas_call(
        functools.partial(_inproj_kernel, alpha=alpha, ns=ns),
        grid=(ng, nt),
        in_specs=in_specs,
        out_specs=out_specs,
        out_shape=out_shape,
        compiler_params=_cparams(("parallel", "parallel"), 56),
        name="inproj",
    )(*args)


def _attn_a_kernel(sink_ref, q_ref, kp_ref, kc_ref, kn_ref, kx_ref, mask_ref, o_ref):
    q = q_ref[0]
    kv = jnp.concatenate([kp_ref[0], kc_ref[0], kn_ref[0], kx_ref[0]], axis=0)
    mask = mask_ref[0]
    group = A_HEADS // A_KV_HEADS
    for g in range(A_KV_HEADS):
        k = kv[:, g * HEAD_DIM:(g + 1) * HEAD_DIM]
        v = kv[:, LANES + g * HEAD_DIM:LANES + (g + 1) * HEAD_DIM]
        heads = range(g * group, (g + 1) * group)
        qs = jnp.concatenate([q[:, hh * HEAD_DIM:(hh + 1) * HEAD_DIM] for hh in heads], axis=0)
        s = _dot_nt(qs, k)
        ps, ls = [], []
        for r, hh in enumerate(heads):
            sr = s[r * BLK:(r + 1) * BLK] + mask
            sink = sink_ref[hh]
            m = jnp.maximum(jnp.max(sr, axis=-1, keepdims=True), sink)
            p = jnp.exp(sr - m)
            ls.append(jnp.sum(p, axis=-1, keepdims=True) + jnp.exp(sink - m))
            ps.append(p.astype(BF16))
        o = _dot(jnp.concatenate(ps, axis=0), v)
        for r, hh in enumerate(heads):
            o_ref[0, :, hh * HEAD_DIM:(hh + 1) * HEAD_DIM] = (o[r * BLK:(r + 1) * BLK] / ls[r]).astype(BF16)


def _attn_a_mask(lctx):
    i = np.arange(BLK)[:, None]
    j = np.arange(BLK)[None, :]
    ok_prev = (j >= i)
    ok_next = (j <= i)
    yes = np.ones((BLK, BLK), bool)
    no = np.zeros((BLK, BLK), bool)
    ctx = np.ones((BLK, lctx), bool)
    variants = [
        np.concatenate([ok_prev, yes, ok_next, ctx], 1),
        np.concatenate([no, yes, ok_next, ctx], 1),
        np.concatenate([ok_prev, yes, no, ctx], 1),
        np.concatenate([no, no, no, ctx], 1),
    ]
    return jnp.asarray(np.where(np.stack(variants), 0.0, NEG_INF).astype(np.float32))


def _attn_a(qa, kva, sink, lctx):
    nb, tt, _ = qa.shape
    nblk = tt // BLK
    lb = lctx // BLK
    assert nblk - lb >= 2
    mask = _attn_a_mask(lctx)

    def variant(t):
        return jnp.where(t < lb, 3, jnp.where(t == lb, 1, jnp.where(t == nblk - 1, 2, 0)))

    kvb = lambda f: pl.BlockSpec((1, BLK, 256), lambda b, t: (b, f(t), 0))
    return pl.pallas_call(
        _attn_a_kernel,
        grid=(nb, nblk),
        in_specs=[
            pl.BlockSpec(memory_space=pltpu.SMEM),
            pl.BlockSpec((1, BLK, 512), lambda b, t: (b, t, 0)),
            kvb(lambda t: jnp.maximum(t - 1, 0)),
            kvb(lambda t: t),
            kvb(lambda t: jnp.minimum(t + 1, nblk - 1)),
            pl.BlockSpec((1, lctx, 256), lambda b, t: (b, 0, 0)),
            pl.BlockSpec((1, BLK, 3 * BLK + lctx), lambda b, t: (variant(t), 0, 0)),
        ],
        out_specs=pl.BlockSpec((1, BLK, 512), lambda b, t: (b, t, 0)),
        out_shape=jax.ShapeDtypeStruct((nb, tt, 512), BF16),
        compiler_params=_cparams(("parallel", "parallel")),
        name="attn_window",
    )(sink, qa, kva, kva, kva, kva, mask)


NB_KBLK = 5


def _attn_b_kernel(q_ref, k0, k1, k2, k3, k4, kx, v0, v1, v2, v3, v4, vx, bias_ref, o_ref, *, lb):
    nloc = NB_KBLK * BLK
    first = lax.broadcasted_iota(jnp.int32, (1, LANES), 1) < HEAD_DIM

    def attend(k, v, with_bias):
        q = q_ref[0]
        for pair in range(B_HEADS // 2):
            sl = slice(pair * LANES, (pair + 1) * LANES)
            q2, k2, v2 = q[:, sl], k[:, sl], v[:, sl]
            zero = jnp.zeros_like(q2)
            s2 = _dot_nt(jnp.concatenate([jnp.where(first, q2, zero), jnp.where(first, zero, q2)], axis=0), k2)
            ps, ls = [], []
            for j in range(2):
                s = s2[j * BLK:(j + 1) * BLK]
                if with_bias:
                    s = jnp.concatenate([s[:, :nloc] + bias_ref[0, 0, 2 * pair + j], s[:, nloc:]], axis=1)
                p = jnp.exp(s - jnp.max(s, axis=-1, keepdims=True))
                ls.append(jnp.sum(p, axis=-1, keepdims=True))
                ps.append(p.astype(BF16))
            o2 = _dot(jnp.concatenate(ps, axis=0), v2)
            o_ref[0, :, sl] = jnp.where(first, o2[:BLK] / ls[0], o2[BLK:] / ls[1]).astype(BF16)

    is_ctx = pl.program_id(1) < lb

    @pl.when(is_ctx)
    def _():
        attend(kx[0], vx[0], False)

    @pl.when(jnp.logical_not(is_ctx))
    def _():
        attend(jnp.concatenate([k0[0], k1[0], k2[0], k3[0], k4[0], kx[0]], axis=0),
               jnp.concatenate([v0[0], v1[0], v2[0], v3[0], v4[0], vx[0]], axis=0), True)


def _attn_b_bias(rpb, nlat):
    rows = 2 * nlat
    wh = min(NA_ROWS, rows)
    n = GRID_W
    qrows, krows = BLK // n, NB_KBLK * BLK // n
    nvar = NB_KBLK
    edge = n - NA_COLS
    g = jnp.concatenate([jnp.repeat(rpb[..., :1], edge, axis=-1), rpb.astype(F32),
                         jnp.repeat(rpb[..., -1:], edge + 1, axis=-1)], axis=-1)
    g = jnp.roll(g, -(n - 1), axis=-1)
    toep = jnp.tile(g, (1, 1, 1, n))[..., :n * (2 * n - 1)].reshape(g.shape[:3] + (n, 2 * n - 1))[..., :n]

    reps = [0, 1, 2, nlat - 2, nlat - 1]
    i = np.arange(BLK)
    kk = np.arange(NB_KBLK * BLK)
    ok, dr = [], []
    for dlt, j in enumerate(reps):
        base = j - dlt
        r = 2 * j + i // n
        qc = i % n
        kr = 2 * base + kk // n
        kc = kk % n
        start = np.clip(r - wh // 2, 0, rows - wh)
        row_ok = (kr[None, :] >= start[:, None]) & (kr[None, :] < start[:, None] + wh)
        c0 = np.clip(qc - NA_COLS // 2, 0, n - NA_COLS)
        col_ok = (kc[None, :] >= c0[:, None]) & (kc[None, :] < c0[:, None] + NA_COLS)
        ok.append(row_ok & col_ok)
        dr.append(np.clip(2 * (base - j) + np.arange(krows)[None, :] - np.arange(qrows)[:, None] + NA_ROWS - 1,
                          0, 2 * NA_ROWS - 2))
    depth, nh = rpb.shape[:2]
    bias = jnp.stack([jnp.concatenate([jnp.concatenate([toep[:, :, dr[v][a, b]] for b in range(krows)], axis=-1)
                                       for a in range(qrows)], axis=-2) for v in range(nvar)], axis=1)
    return jnp.where(jnp.asarray(np.stack(ok))[None, :, None], bias, NEG_INF)


def _attn_b(qb, kb, vb, bias, layer, lctx):
    nb, tt, _ = qb.shape
    nblk = tt // BLK
    lb = lctx // BLK
    nlat = nblk - lb

    def base(t):
        return jnp.clip(t - lb - 2, 0, nlat - NB_KBLK) + lb

    def variant(t):
        return jnp.where(t < lb, 0, t - base(t))

    loc = lambda i: pl.BlockSpec((1, BLK, 512), lambda b, t: (b, base(t) + i, 0))
    ctx = pl.BlockSpec((1, lctx, 512), lambda b, t: (b, 0, 0))
    return pl.pallas_call(
        functools.partial(_attn_b_kernel, lb=lb),
        grid=(nb, nblk),
        in_specs=[pl.BlockSpec((1, BLK, 512), lambda b, t: (b, t, 0))]
        + [loc(i) for i in range(NB_KBLK)] + [ctx]
        + [loc(i) for i in range(NB_KBLK)] + [ctx]
        + [pl.BlockSpec((1, 1, B_HEADS, BLK, NB_KBLK * BLK), lambda b, t: (layer, variant(t), 0, 0, 0))],
        out_specs=pl.BlockSpec((1, BLK, 512), lambda b, t: (b, t, 0)),
        out_shape=jax.ShapeDtypeStruct((nb, tt, 512), BF16),
        compiler_params=_cparams(("parallel", "parallel")),
        name="attn_neighbourhood",
    )(qb, *([kb] * (NB_KBLK + 1)), *([vb] * (NB_KBLK + 1)), bias)


HALO = 16


SUB = 8


NCB = C_CH // LANES


def _conv_kernel(prev_ref, cur_ref, next_ref, w_ref, b_ref, g_ref, bb_ref, o_ref, sh_ref, acc_ref, *, lb, nblk):
    t = pl.program_id(1)
    has_prev = jnp.logical_and(t != 0, t != lb)
    has_next = jnp.logical_and(t != lb - 1, t != nblk - 1)
    prev = jnp.where(has_prev, prev_ref[0], 0.0)
    nxt = jnp.where(has_next, next_ref[0], 0.0)
    cur = cur_ref[0]
    for cb in range(NCB):
        cs = slice(cb * LANES, (cb + 1) * LANES)
        sh_ref[0, cb, 0:HALO, :] = prev[:, cs]
        sh_ref[0, cb, HALO:HALO + BLK, :] = cur[:, cs]
        sh_ref[0, cb, HALO + BLK:2 * HALO + BLK, :] = nxt[:, cs]
    rows = BLK + 2 * HALO - SUB
    pad = C_KSIZE // 2

    def channel_block(cb, carry):
        for r in range(1, SUB):
            sh_ref[r, cb, 0:rows, :] = sh_ref[0, cb, pl.ds(r, rows), :]
        acc = jnp.zeros((BLK // SUB, SUB, LANES), F32)
        for kk in range(C_KSIZE):
            off = HALO - pad + kk
            rows_k = sh_ref[off % SUB, cb, off - off % SUB:off - off % SUB + BLK, :]
            acc = acc + rows_k.reshape(BLK // SUB, SUB, LANES) * w_ref[kk, cb][None]
        acc_ref[cb] = acc.reshape(BLK, LANES)
        return carry

    lax.fori_loop(0, NCB, channel_block, 0)
    y = jnp.concatenate([acc_ref[cb] for cb in range(NCB)], axis=1)
    y = _ln(y + b_ref[...]) * g_ref[...] + bb_ref[...]
    o_ref[0] = (y * jax.nn.sigmoid(y)).astype(BF16)


def _conv(yc, w, b, g, bb, lctx):
    nb, tt, ch = yc.shape
    nblk = tt // BLK
    lb = lctx // BLK
    per = BLK // HALO
    vec = pl.BlockSpec((1, ch), lambda b_, t: (0, 0))
    return pl.pallas_call(
        functools.partial(_conv_kernel, lb=lb, nblk=nblk),
        grid=(nb, nblk),
        in_specs=[
            pl.BlockSpec((1, HALO, ch), lambda b_, t: (b_, jnp.maximum(t * per - 1, 0), 0)),
            pl.BlockSpec((1, BLK, ch), lambda b_, t: (b_, t, 0)),
            pl.BlockSpec((1, HALO, ch), lambda b_, t: (b_, jnp.minimum((t + 1) * per, nblk * per - 1), 0)),
            pl.BlockSpec((C_KSIZE, NCB, SUB, LANES), lambda b_, t: (0, 0, 0, 0)),
            vec, vec, vec,
        ],
        out_specs=pl.BlockSpec((1, BLK, ch), lambda b_, t: (b_, t, 0)),
        out_shape=jax.ShapeDtypeStruct((nb, tt, ch), BF16),
        scratch_shapes=[pltpu.VMEM((SUB, NCB, BLK + 2 * HALO, LANES), F32), pltpu.VMEM((NCB, BLK, LANES), F32)],
        compiler_params=_cparams(("parallel", "parallel")),
        name="conformer_conv",
    )(yc, yc, yc, jnp.broadcast_to(w.reshape(C_KSIZE, NCB, 1, LANES), (C_KSIZE, NCB, SUB, LANES)),
      b.reshape(1, ch), g.reshape(1, ch),
      bb.reshape(1, ch))


def _mlstm_kernel(qf_ref, kf_ref, vf_ref, gcf_ref, grf_ref, qb_ref, kb_ref, vb_ref, gcb_ref, grb_ref,
                  hf_ref, hb_ref, c_st, n_st, m_st, *, mb):
    @pl.when(pl.program_id(1) == 0)
    def _():
        c_st[...] = jnp.zeros_like(c_st)
        n_st[...] = jnp.zeros_like(n_st)
        m_st[...] = jnp.zeros_like(m_st)

    rr = lax.broadcasted_iota(jnp.int32, (BLK, BLK), 0)
    cc = lax.broadcasted_iota(jnp.int32, (BLK, BLK), 1)
    ch = []
    dirs = ((qf_ref, kf_ref, vf_ref, gcf_ref, grf_ref, hf_ref), (qb_ref, kb_ref, vb_ref, gcb_ref, grb_ref, hb_ref))
    for bb, dirn in [(bb, dirn) for bb in range(mb) for dirn in range(2)]:
        q_ref, k_ref, v_ref, gc_ref, gr_ref, h_ref = dirs[dirn]
        before = (rr >= cc) if dirn == 0 else (rr <= cc)
        bmat = jnp.where(before, 1.0, 0.0).astype(BF16)
        gc = gc_ref[0, bb]
        gr = gr_ref[0, bb]
        lf_c = jax.nn.log_sigmoid(gc[:, M_HEADS:2 * M_HEADS])
        lf_r = jax.nn.log_sigmoid(gr[M_HEADS:2 * M_HEADS, :])
        fc = sum(_dot(bmat, part) for part in _split3(lf_c))
        fr = sum(_dot_nt(part, bmat) for part in _split3(lf_r))
        f_tot = jnp.sum(lf_r, axis=-1, keepdims=True)
        for hh in range(M_HEADS):
            sl = slice(hh * M_DIM, (hh + 1) * M_DIM)
            st = (bb * 2 + dirn) * M_HEADS + hh
            ch.append(dict(before=before, sl=sl, st=st, h_ref=h_ref, bb=bb,
                           q=q_ref[bb, :, sl], k=k_ref[bb, :, sl], v=v_ref[bb, :, sl],
                           f_c=fc[:, hh:hh + 1], f_r=fr[hh:hh + 1, :], f_tot=f_tot[hh:hh + 1, :],
                           i_c=gc[:, hh:hh + 1], i_r=gr[hh:hh + 1, :],
                           c_old=c_st[st], n_old=n_st[st], m_old=m_st[st][:, 0:1]))

    for c in ch:
        c["qk"] = _dot_nt(c["q"], c["k"])
        c["qc"] = _dot(c["q"], c["c_old"].astype(BF16))
    for c in ch:
        c["a"] = c["f_c"] + c["m_old"]
        c["logw"] = jnp.where(c["before"], c["f_c"] - c["f_r"] + c["i_r"], -jnp.inf)
        c["mt"] = jnp.maximum(c["a"], jnp.max(c["logw"], axis=-1, keepdims=True))
    for c in ch:
        g_r = c["f_tot"] - c["f_r"] + c["i_r"]
        c["m_new"] = jnp.maximum(c["f_tot"] + c["m_old"], jnp.max(g_r, axis=-1, keepdims=True))
        c["decay"] = jnp.exp(c["f_tot"] + c["m_old"] - c["m_new"])
        c["kw"] = c["k"].astype(F32) * jnp.exp(c["f_tot"] - c["f_c"] + c["i_c"] - c["m_new"])
    for c in ch:
        c["s"] = c["qk"] * jnp.exp(c["logw"] - c["mt"])
        c["w_inter"] = jnp.exp(c["a"] - c["mt"])
    for c in ch:
        c["sv"] = _dot(c["s"].astype(BF16), c["v"])
        c["kv"] = lax.dot_general(c["kw"].astype(BF16), c["v"], (((0,), (0,)), ((), ())), preferred_element_type=F32)
    for c in ch:
        num = c["w_inter"] * c["qc"] + c["sv"]
        den = (c["w_inter"] * jnp.sum(c["q"].astype(F32) * c["n_old"], axis=-1, keepdims=True)
               + jnp.sum(c["s"], axis=-1, keepdims=True))
        hout = num / jnp.maximum(jnp.abs(den), jnp.exp(-c["mt"]))
        c["h_ref"][c["bb"], :, c["sl"]] = hout.astype(BF16)
    for c in ch:
        st = c["st"]
        c_st[st] = c["decay"] * c["c_old"] + c["kv"]
        n_st[st] = c["decay"] * c["n_old"] + jnp.sum(c["kw"], axis=0, keepdims=True)
        m_st[st] = jnp.broadcast_to(c["m_new"], (1, LANES))


def _mlstm(qd, kd, vd, gates, lctx):
    nb, tt, _ = qd.shape
    nblk = tt // BLK
    lb = lctx // BLK
    gcol = jnp.stack([gates[..., 0:8], gates[..., 8:16]])
    grow = jnp.swapaxes(gcol, 2, 3)

    def bwd(i):
        return jnp.where(i < lb, lb - 1 - i, nblk - 1 + lb - i)

    fwd = lambda i: i
    mb = max(m for m in (4, 2, 1) if nb % m == 0)
    tok = lambda f: pl.BlockSpec((mb, BLK, 512), lambda b, i: (b, f(i), 0))
    gcs = lambda d, f: pl.BlockSpec((1, mb, BLK, 8), lambda b, i: (d, b, f(i), 0))
    grs = lambda d, f: pl.BlockSpec((1, mb, 8, BLK), lambda b, i: (d, b, 0, f(i)))
    out = jax.ShapeDtypeStruct((nb, tt, 512), BF16)
    nst = 2 * mb * M_HEADS
    return pl.pallas_call(
        functools.partial(_mlstm_kernel, mb=mb),
        grid=(nb // mb, nblk),
        in_specs=[tok(fwd), tok(fwd), tok(fwd), gcs(0, fwd), grs(0, fwd),
                  tok(bwd), tok(bwd), tok(bwd), gcs(1, bwd), grs(1, bwd)],
        out_specs=[tok(fwd), tok(bwd)],
        out_shape=[out, out],
        scratch_shapes=[pltpu.VMEM((nst, M_DIM, M_DIM), F32),
                        pltpu.VMEM((nst, 1, M_DIM), F32),
                        pltpu.VMEM((nst, 1, LANES), F32)],
        compiler_params=_cparams(("parallel", "arbitrary")),
        name="mlstm",
    )(qd, kd, vd, gcol, grow, qd, kd, vd, gcol, grow)


def _merge_kernel(x_ref, mod_ref, ya_ref, yb_ref, yc_ref, hf_ref, hb_ref, so_ref,
                  wg_ref, bg_ref, wbr_ref, wo_ref, g1_ref, b1_ref, wr_ref,
                  x1_o, hp_o, aff_o, *, alpha, ns):
    d = D_MODEL
    R = range(ns)
    mods = [mod_ref[r] for r in R]
    xs = [x_ref[r] for r in R]
    hs = [(_ln(xs[r]) * (1.0 + mods[r][:, d:2 * d]) + mods[r][:, 0:d]).astype(BF16) for r in R]
    yds = [(so_ref[r].astype(F32) * (hf_ref[r].astype(F32) + hb_ref[r].astype(F32))).astype(BF16) for r in R]
    yss = [(ya_ref[r], yb_ref[r], yc_ref[r], yds[r]) for r in R]
    pres = [[_dot(hs[r], wg_ref[0, :, i * d:(i + 1) * d]) for i in range(N_BRANCH)] for r in R]
    brs = [[_dot(yss[r][i], wbr_ref[0, i]) for i in range(N_BRANCH)] for r in R]
    zs = []
    for r in R:
        z = None
        for i in range(N_BRANCH):
            term = jax.nn.sigmoid(pres[r][i] + bg_ref[:, i * d:(i + 1) * d]) * brs[r][i]
            z = term if z is None else z + term
        zs.append(z)
    ys = [_dot(zs[r].astype(BF16), wo_ref[0]) for r in R]
    x1s = [_ln(alpha * xs[r] + mods[r][:, 2 * d:3 * d] * ys[r]) * g1_ref[...] + b1_ref[...] for r in R]
    for r in R:
        x1_o[r] = x1s[r]
    h2s = [_ln(x1s[r]) * (1.0 + mods[r][:, 4 * d:5 * d]) + mods[r][:, 3 * d:4 * d] for r in R]
    for r in R:
        h2 = h2s[r]
        hb16 = h2.astype(BF16)
        bits = pltpu.bitcast(hb16.astype(F32), jnp.uint32)
        hp_o[r] = (bits[:, d // 2:] & jnp.uint32(0xFFFF0000)) | (bits[:, :d // 2] >> 16)
        h2_hi, h2_lo = _split2(h2)
        r_hi = _dot(h2_hi, wr_ref[...])
        r_lo = _dot(h2_lo, wr_ref[...])
        logits = r_hi[:, 0:N_EXPERTS] + (r_hi[:, N_EXPERTS:2 * N_EXPERTS] + r_lo[:, 0:N_EXPERTS])
        e = jnp.exp(logits - jnp.max(logits, axis=-1, keepdims=True))
        aff_o[r] = e / jnp.sum(e, axis=-1, keepdims=True)


def _merge(x, mod_all_l, ya, yb, yc, hf, hb, so, wg, bg, wbr, wo, layer, g1, b1, wr, lctx, alpha):
    nb, tt, d = x.shape
    nt = tt // TM
    lt = lctx // TM
    ns = 2 if nb % 2 == 0 else 1
    ng = nb // ns
    modg = _mod_groups(mod_all_l, nb, ns)
    tok = lambda n: pl.BlockSpec((ns, TM, n), lambda g, t: (g, t, 0))
    const = lambda shape: pl.BlockSpec(shape, lambda g, t: (0,) * len(shape))
    stacked = lambda shape: pl.BlockSpec((1,) + shape, lambda g, t: (layer,) + (0,) * len(shape))
    return pl.pallas_call(
        functools.partial(_merge_kernel, alpha=alpha, ns=ns),
        grid=(ng, nt),
        in_specs=[
            tok(d),
            pl.BlockSpec((ns, 1, 6 * d), lambda g, t: (jnp.where(t < lt, ng, g), 0, 0)),
            tok(512), tok(512), tok(512), tok(512), tok(512), tok(512),
            stacked((d, N_BRANCH * d)), const((1, N_BRANCH * d)), stacked((N_BRANCH, BRANCH_W, d)),
            stacked((d, d)), const((1, d)), const((1, d)), const((d, LANES)),
        ],
        out_specs=[tok(d), tok(d // 2), tok(N_EXPERTS)],
        out_shape=[jax.ShapeDtypeStruct((nb, tt, d), F32),
                   jax.ShapeDtypeStruct((nb, tt, d // 2), jnp.uint32),
                   jax.ShapeDtypeStruct((nb, tt, N_EXPERTS), F32)],
        compiler_params=_cparams(("parallel", "parallel"), 56),
        name="merge",
    )(x, modg, ya, yb, yc, hf, hb, so, wg, bg.reshape(1, -1), wbr, wo,
      g1.reshape(1, d), b1.reshape(1, d), wr)


def _route_one(a_ref, ones_ref, tab_ref, idx_o, val_o, *, ntok, cap, tok_off, slot_off):
    a = a_ref[0]
    bits = pltpu.bitcast(a, jnp.int32)
    thr = jnp.zeros((N_EXPERTS, 1), jnp.int32)
    for bit in range(30, -1, -1):
        cand = thr | jnp.int32(1 << bit)
        cnt = jnp.sum(jnp.where(bits >= cand, 1.0, 0.0), axis=-1, keepdims=True)
        thr = jnp.where(cnt >= cap, cand, thr)
    gt = bits > thr
    eq = bits == thr
    need = cap - jnp.sum(jnp.where(gt, 1.0, 0.0), axis=-1, keepdims=True)

    rr = lax.broadcasted_iota(jnp.int32, (LANES, LANES), 0)
    cc = lax.broadcasted_iota(jnp.int32, (LANES, LANES), 1)
    upper = jnp.where(rr <= cc, 1.0, 0.0).astype(BF16)

    def cumsum_blocks(mask_f):
        run = jnp.zeros((N_EXPERTS, 1), F32)
        out = []
        for c in range(ntok // LANES):
            blk = mask_f[:, c * LANES:(c + 1) * LANES]
            out.append(_dot(blk.astype(BF16), upper) + run)
            run = run + jnp.sum(blk, axis=-1, keepdims=True)
        return out

    eq_f = jnp.where(eq, 1.0, 0.0)
    cum_eq = cumsum_blocks(eq_f)
    sel_parts = []
    for c in range(ntok // LANES):
        sl = slice(c * LANES, (c + 1) * LANES)
        sel_parts.append(jnp.where(gt[:, sl] | (eq[:, sl] & (cum_eq[c] <= need)), 1.0, 0.0))
    sel_f = jnp.concatenate(sel_parts, axis=1)
    cpos = cumsum_blocks(sel_f)
    nblk = ntok // LANES
    tab_ref[...] = jnp.zeros_like(tab_ref)
    for c in range(nblk):
        for e in range(N_EXPERTS):
            tab_ref[0, e, c:c + 1, :] = cpos[c][e:e + 1, :]
            tab_ref[1, e, c:c + 1, :] = a_ref[0, e:e + 1, c * LANES:(c + 1) * LANES]
    counts = _dot(sel_f.astype(BF16), ones_ref[...])
    blockend = _dot(counts.astype(BF16), upper)
    prevend = blockend - counts

    lane = lax.broadcasted_iota(jnp.int32, (1, LANES), 1).astype(F32)
    slot = lax.broadcasted_iota(jnp.int32, (cap, 1), 0).astype(F32)
    for e in range(N_EXPERTS):
        be = blockend[e:e + 1, :]
        pe = prevend[e:e + 1, :]
        pick = jnp.where(pe <= slot, jnp.where(slot < be, 1.0, 0.0), 0.0).astype(BF16)
        cnt_blk = sum(_dot(pick, part) for part in _split2(tab_ref[0, e]))
        aff_blk = sum(_dot(pick, part) for part in _split3(tab_ref[1, e]))
        within = jnp.sum(jnp.where(cnt_blk <= slot, 1.0, 0.0), axis=-1, keepdims=True)
        nfull = jnp.sum(jnp.where(be <= slot, 1.0, 0.0), axis=-1, keepdims=True)
        idx_o[0, e, slot_off:slot_off + cap, :] = (nfull * LANES + within + tok_off).astype(jnp.int32)
        val_o[0, e, slot_off:slot_off + cap, :] = jnp.sum(jnp.where(lane == within, aff_blk, 0.0),
                                                           axis=-1, keepdims=True)


def _route_kernel(al_ref, ac_ref, onesl_ref, onesc_ref, idx_o, val_o, tab_ref, *, s, lctx, cap_l, cap_c):
    _route_one(al_ref, onesl_ref, tab_ref, idx_o, val_o, ntok=s, cap=cap_l, tok_off=lctx, slot_off=0)
    _route_one(ac_ref, onesc_ref, tab_ref, idx_o, val_o, ntok=lctx, cap=cap_c, tok_off=0, slot_off=cap_l)


def _route(aff, lctx):
    nb, tt, ne = aff.shape
    s = tt - lctx
    cap_l = CAPACITY_FACTOR * s // ne
    cap_c = CAPACITY_FACTOR * lctx // ne
    capt = cap_l + cap_c
    assert s // LANES <= LANES and cap_l % 8 == 0 and cap_c % 8 == 0
    aff_t = jnp.swapaxes(aff, 1, 2)
    ones = lambda n: jnp.asarray(np.arange(n)[:, None] // LANES == np.arange(LANES)[None, :], BF16)
    out = lambda dt: jax.ShapeDtypeStruct((nb, ne, capt, 1), dt)
    return pl.pallas_call(
        functools.partial(_route_kernel, s=s, lctx=lctx, cap_l=cap_l, cap_c=cap_c),
        grid=(nb,),
        in_specs=[pl.BlockSpec((1, ne, s), lambda b: (b, 0, 0)),
                  pl.BlockSpec((1, ne, lctx), lambda b: (b, 0, 0)),
                  pl.BlockSpec((s, LANES), lambda b: (0, 0)),
                  pl.BlockSpec((lctx, LANES), lambda b: (0, 0))],
        out_specs=[pl.BlockSpec((1, ne, capt, 1), lambda b: (b, 0, 0, 0))] * 2,
        out_shape=[out(jnp.int32), out(F32)],
        scratch_shapes=[pltpu.VMEM((2, ne, LANES, LANES), F32)],
        compiler_params=_cparams(("parallel",), 40),
        name="route",
    )(aff_t[:, :, lctx:], aff_t[:, :, :lctx], ones(s), ones(lctx))


def _gather_kernel(idx_ref, x_ref, o_ref, *, capt):
    def body(i, carry):
        c0 = pl.multiple_of(i * SUB, SUB)
        rows = [x_ref[0, pl.ds(idx_ref[0, 0, c0 + j], 1), :] for j in range(SUB)]
        o_ref[0, 0, pl.ds(c0, SUB), :] = jnp.concatenate(rows, axis=0)
        return carry
    lax.fori_loop(0, capt // SUB, body, 0)


def _gather(idx_s, hp, capt):
    nb, tt, w = hp.shape
    assert capt % SUB == 0
    ne = N_EXPERTS
    return pl.pallas_call(
        functools.partial(_gather_kernel, capt=capt),
        grid=(nb, ne),
        in_specs=[pl.BlockSpec((1, 1, capt), lambda b, e: (b * ne + e, 0, 0), memory_space=pltpu.SMEM),
                  pl.BlockSpec((1, tt, w), lambda b, e: (b, 0, 0))],
        out_specs=pl.BlockSpec((1, 1, capt, w), lambda b, e: (b, e, 0, 0)),
        out_shape=jax.ShapeDtypeStruct((nb, ne, capt, w), jnp.uint32),
        compiler_params=_cparams(("parallel", "arbitrary"), 40),
        name="moe_gather",
    )(idx_s, hp)


FF_CHUNK = 256


def _ffn_kernel(x_ref, w1_ref, w3_ref, w2_ref, val_ref, y_ref, w1b, w3b, w2b):
    @pl.when(pl.program_id(1) == 0)
    def _():
        w1b[...] = w1_ref[0, 0].astype(BF16)
        w3b[...] = w3_ref[0, 0].astype(BF16)
        w2b[...] = w2_ref[0, 0].astype(BF16)

    packed = x_ref[0, 0]
    lo = pltpu.bitcast(packed << 16, F32)
    hi = pltpu.bitcast(packed & jnp.uint32(0xFFFF0000), F32)
    xg = jnp.concatenate([lo, hi], axis=1).astype(BF16)
    y = None
    for c in range(EXPERT_FF // FF_CHUNK):
        cs = slice(c * FF_CHUNK, (c + 1) * FF_CHUNK)
        a = _dot(xg, w1b[:, cs])
        hid = (a * jax.nn.sigmoid(a) * _dot(xg, w3b[:, cs])).astype(BF16)
        part = _dot(hid, w2b[cs, :])
        y = part if y is None else y + part
    y_ref[0, 0] = y * val_ref[0, 0]


def _ffn(xg, w1, w3, w2, vals, layer):
    nb, ne, capt, w = xg.shape
    d, ff = w1.shape[2], w1.shape[3]
    return pl.pallas_call(
        _ffn_kernel,
        grid=(ne, nb),
        in_specs=[pl.BlockSpec((1, 1, capt, w), lambda e, b: (b, e, 0, 0)),
                  pl.BlockSpec((1, 1, d, ff), lambda e, b: (layer, e, 0, 0)),
                  pl.BlockSpec((1, 1, d, ff), lambda e, b: (layer, e, 0, 0)),
                  pl.BlockSpec((1, 1, ff, d), lambda e, b: (layer, e, 0, 0)),
                  pl.BlockSpec((1, 1, capt, 1), lambda e, b: (b, e, 0, 0))],
        out_specs=pl.BlockSpec((1, 1, capt, d), lambda e, b: (b, e, 0, 0)),
        out_shape=jax.ShapeDtypeStruct((nb, ne, capt, d), F32),
        scratch_shapes=[pltpu.VMEM((d, ff), BF16), pltpu.VMEM((d, ff), BF16), pltpu.VMEM((ff, d), BF16)],
        compiler_params=_cparams(("parallel", "arbitrary"), 56),
        name="moe_ffn",
    )(xg, w1, w3, w2, vals)


SCATTER_GROUP = 8


def _scatter_kernel(idx_ref, y_ref, o_ref, *, capt):
    @pl.when(pl.program_id(1) == 0)
    def _():
        o_ref[...] = jnp.zeros_like(o_ref)

    def body(i, carry):
        c0 = pl.multiple_of(i * SCATTER_GROUP, SCATTER_GROUP)
        rows = [pl.ds(idx_ref[0, 0, c0 + j], 1) for j in range(SCATTER_GROUP)]
        acc = [o_ref[0, r, :] for r in rows]
        ytile = y_ref[0, 0, pl.ds(c0, SCATTER_GROUP), :]
        new = [a + ytile[j:j + 1, :] for j, a in enumerate(acc)]
        for r, v in zip(rows, new):
            o_ref[0, r, :] = v
        return carry
    lax.fori_loop(0, capt // SCATTER_GROUP, body, 0)


def _scatter(idx_s, y, tt):
    nb, ne, capt, d = y.shape
    assert capt % SCATTER_GROUP == 0
    return pl.pallas_call(
        functools.partial(_scatter_kernel, capt=capt),
        grid=(nb, ne),
        in_specs=[pl.BlockSpec((1, 1, capt), lambda b, e: (b * ne + e, 0, 0), memory_space=pltpu.SMEM),
                  pl.BlockSpec((1, 1, capt, d), lambda b, e: (b, e, 0, 0))],
        out_specs=pl.BlockSpec((1, tt, d), lambda b, e: (b, 0, 0)),
        out_shape=jax.ShapeDtypeStruct((nb, tt, d), F32),
        compiler_params=_cparams(("parallel", "arbitrary"), 56),
        name="moe_scatter",
    )(idx_s, y)


def _post_kernel(x_ref, ml_ref, mod_ref, g_ref, b_ref, o_ref, *, alpha):
    d = D_MODEL
    g2 = mod_ref[0][:, 5 * d:6 * d]
    o_ref[0] = _ln(alpha * x_ref[0] + g2 * ml_ref[0]) * g_ref[...] + b_ref[...]


def _post(x1, ml, mod3, g, b, lctx, alpha):
    nb, tt, d = x1.shape
    lt = lctx // TM
    t0 = lt
    tok = pl.BlockSpec((1, TM, d), lambda b_, t: (b_, t + t0, 0))
    vec = pl.BlockSpec((1, d), lambda b_, t: (0, 0))
    return pl.pallas_call(
        functools.partial(_post_kernel, alpha=alpha),
        grid=(nb, tt // TM - t0),
        in_specs=[tok, tok, pl.BlockSpec((1, 1, 6 * d), lambda b_, t: (jnp.where(t + t0 < lt, nb, b_), 0, 0)),
                  vec, vec],
        out_specs=pl.BlockSpec((1, TM, d), lambda b_, t: (b_, t, 0)),
        out_shape=jax.ShapeDtypeStruct((nb, tt - t0 * TM, d), F32),
        compiler_params=_cparams(("parallel", "parallel")),
        name="moe_post",
    )(x1, ml, mod3, g.reshape(1, d), b.reshape(1, d))


def _rope_tables(s, lctx):
    half = HEAD_DIM // 2
    nf = half // 2
    inv = ROPE_BASE ** (-jnp.arange(nf, dtype=F32) / nf)
    t = jnp.arange(s)
    lane = np.arange(LANES)
    dd = lane % HEAD_DIM
    use_col = jnp.asarray(dd >= half)[None, :]
    pos = jnp.where(use_col, (t % GRID_W)[:, None], (t // GRID_W)[:, None]).astype(F32)
    ang = pos * inv[jnp.asarray(dd % nf)][None, :]
    cos, sin = jnp.cos(ang), jnp.sin(ang)
    first = jnp.asarray((dd % half) < nf)[None, :]
    sa = jnp.where(first, -sin, 0.0)
    sb = jnp.where(first, 0.0, sin)
    pad = lambda a, v: jnp.concatenate([jnp.full((lctx, LANES), v, F32), a], axis=0)
    return pad(cos, 1.0), pad(sa, 0.0), pad(sb, 0.0)


def kernel(x, c, ctx, c_ctx, w_mod, b_mod, w_in, attn_sink, na_rpb, conv_w, conv_b, conv_ln_g, conv_ln_b,
           mlstm_gate_b, w_branch, w_gate, b_gate, w_out, ln1_g, ln1_b, w_router, w_exp_gate, w_exp_up,
           w_exp_down, ln2_g, ln2_b):
    nb, s, d = x.shape
    lctx = ctx.shape[1]
    depth = w_mod.shape[0]
    assert d == D_MODEL and nb + 1 <= 8 and lctx % TM == 0 and s % TM == 0 and s % GRID_W == 0
    alpha = (2.0 * depth) ** 0.25
    tt = lctx + s
    cap_t = CAPACITY_FACTOR * s // N_EXPERTS + CAPACITY_FACTOR * lctx // N_EXPERTS

    c8 = jnp.concatenate([c, c_ctx[None, :], jnp.zeros((8 - nb - 1, d), F32)], axis=0)
    mod_all = _modulation(c8, w_mod, b_mod)
    tabs = _rope_tables(s, lctx)
    xs = jnp.concatenate([ctx, x], axis=1)
    nlat = s // BLK
    assert nlat >= NB_KBLK
    bias_b = _attn_b_bias(na_rpb, nlat)
    w_in_bf = jnp.pad(w_in, ((0, 0), (0, 0), (0, PROJ_PAD - PROJ_W))).astype(BF16)
    w_gate_bf, w_branch_bf, w_out_bf = w_gate.astype(BF16), w_branch.astype(BF16), w_out.astype(BF16)

    for l in range(depth):
        mod3 = mod_all[l].reshape(8, 1, 6 * d)
        if l == 0:
            outs = _inproj(xs, mod_all[l], tabs, w_in_bf, l, mlstm_gate_b[l], lctx)
        else:
            xs, *outs = _inproj(None, mod_all[l], tabs, w_in_bf, l, mlstm_gate_b[l], lctx, prev=prev)
        (qa, kva, qb, kb, vb, yc0, qd, kd, vd, so, gates) = outs
        ya = _attn_a(qa, kva, attn_sink[l], lctx)
        yb = _attn_b(qb, kb, vb, bias_b, l, lctx)
        yc = _conv(yc0, conv_w[l], conv_b[l], conv_ln_g[l], conv_ln_b[l], lctx)
        hf, hb = _mlstm(qd, kd, vd, gates, lctx)
        wr_hi, wr_lo = _split2(w_router[l])
        wr = jnp.pad(jnp.concatenate([wr_hi, wr_lo], axis=1), ((0, 0), (0, LANES - 2 * N_EXPERTS)))
        x1, hp, aff = _merge(xs, mod_all[l], ya, yb, yc, hf, hb, so, w_gate_bf, b_gate[l], w_branch_bf, w_out_bf, l,
                             ln1_g[l], ln1_b[l], wr, lctx, alpha)
        idx, vals = _route(aff, lctx)
        idx_s = idx.reshape(nb * N_EXPERTS, 1, cap_t)
        xg = _gather(idx_s, hp, cap_t)
        y = _ffn(xg, w_exp_gate, w_exp_up, w_exp_down, vals, l)
        ml = _scatter(idx_s, y, tt)
        prev = (x1, ml, mod_all[l], ln2_g[l], ln2_b[l], alpha)
    return _post(x1, ml, mod3, ln2_g[depth - 1], ln2_b[depth - 1], lctx, alpha)
```

```python
import functools
import math

import numpy as np
import jax
import jax.numpy as jnp
from jax import lax
from jax.experimental import pallas as pl
from jax.experimental.pallas import tpu as pltpu

F32 = jnp.float32
BF16 = jnp.bfloat16

D_MODEL = 1024
GRID_W = 64
HEAD_DIM = 64
A_HEADS = 8
A_KV_HEADS = 2
A_WINDOW = 128
B_HEADS = 8
NA_ROWS = 8
NA_COLS = 16
C_CH = 512
C_KSIZE = 31
M_HEADS = 4
M_DIM = 128
N_BRANCH = 4
BRANCH_W = 512
N_EXPERTS = 16
EXPERT_FF = 1024
CAPACITY_FACTOR = 2
ROPE_BASE = 10000.0
LN_EPS = 1e-6
NEG_INF = -1e30

BLK = 128
TM = 256
LANES = 128
PROJ_W = 5392
PROJ_PAD = 5504
OFF_A, OFF_AKV, OFF_B, OFF_C, OFF_D, OFF_G = 0, 512, 768, 2304, 3328, 5376
MIB = 1 << 20


def _cparams(sem, vmem_mib=None):
    kw = dict(dimension_semantics=sem)
    if vmem_mib is not None:
        kw["vmem_limit_bytes"] = vmem_mib * MIB
    return pltpu.CompilerParams(**kw)


def _ln(x):
    mu = jnp.mean(x, axis=-1, keepdims=True)
    xc = x - mu
    var = jnp.mean(xc * xc, axis=-1, keepdims=True)
    return xc * lax.rsqrt(var + LN_EPS)


def _dot(a, b):
    return jnp.dot(a, b, preferred_element_type=F32)


def _dot_nt(a, b):
    return lax.dot_general(a, b, (((1,), (1,)), ((), ())), preferred_element_type=F32)


def _split2(x):
    hi = x.astype(BF16)
    lo = (x - hi.astype(F32)).astype(BF16)
    return hi, lo


def _split3(x):
    hi = x.astype(BF16)
    r = x - hi.astype(F32)
    mid = r.astype(BF16)
    lo = (r - mid.astype(F32)).astype(BF16)
    return hi, mid, lo


def _dot3(x, w):
    xh, xl = _split2(x)
    wh, wl = _split2(w)
    return _dot(xh, wh) + (_dot(xh, wl) + _dot(xl, wh))


def _mod_kernel(c_ref, w_ref, b_ref, o_ref):
    c = c_ref[...]
    s = c * jax.nn.sigmoid(c)
    o_ref[0] = _dot3(s, w_ref[0]) + b_ref[0]


def _modulation(c8, w_mod, b_mod):
    depth, d, d6 = w_mod.shape
    nj = d6 // d
    return pl.pallas_call(
        _mod_kernel,
        grid=(depth, nj),
        in_specs=[
            pl.BlockSpec((8, d), lambda l, j: (0, 0)),
            pl.BlockSpec((1, d, d), lambda l, j: (l, 0, j)),
            pl.BlockSpec((1, 1, d), lambda l, j: (l, 0, j)),
        ],
        out_specs=pl.BlockSpec((1, 8, d), lambda l, j: (l, 0, j)),
        out_shape=jax.ShapeDtypeStruct((depth, 8, d6), F32),
        compiler_params=_cparams(("arbitrary", "arbitrary"), 40),
        name="modulation",
    )(c8, w_mod, b_mod.reshape(depth, 1, d6))


def _rope(x, cos, sa, sb):
    parts = []
    for j in range(x.shape[1] // LANES):
        xj = x[:, j * LANES:(j + 1) * LANES]
        parts.append(xj * cos + pltpu.roll(xj, LANES - 16, 1) * sa + pltpu.roll(xj, 16, 1) * sb)
    return parts[0] if len(parts) == 1 else jnp.concatenate(parts, axis=1)


def _inproj_kernel(*refs, alpha, ns):
    d = D_MODEL
    R = range(ns)
    if alpha is None:
        x_ref, mod_ref, cos_ref, sa_ref, sb_ref, w3_ref, gb_ref = refs[:7]
        outs = refs[7:]
        xs = [x_ref[r] for r in R]
    else:
        x1_ref, ml_ref, pmod_ref, pg_ref, pb_ref, mod_ref, cos_ref, sa_ref, sb_ref, w3_ref, gb_ref = refs[:11]
        x_o = refs[11]
        outs = refs[12:]
        xs = [_ln(alpha * x1_ref[r] + pmod_ref[r][:, 5 * d:6 * d] * ml_ref[r]) * pg_ref[...] + pb_ref[...] for r in R]
        for r in R:
            x_o[r] = xs[r]
    qa_o, kva_o, qb_o, kb_o, vb_o, yc_o, qd_o, kd_o, vd_o, so_o, g_o = outs
    w_ref = w3_ref.at[0]
    hs = [(_ln(xs[r]) * (1.0 + mod_ref[r][:, d:2 * d]) + mod_ref[r][:, 0:d]).astype(BF16) for r in R]
    cos, sa, sb = cos_ref[...], sa_ref[...], sb_ref[...]
    qscale = HEAD_DIM ** -0.5

    for r in R:
        qa_o[r] = (_rope(_dot(hs[r], w_ref[:, OFF_A:OFF_AKV]), cos, sa, sb) * qscale).astype(BF16)
    for r in R:
        kva = _dot(hs[r], w_ref[:, OFF_AKV:OFF_B])
        kva_o[r, :, 0:LANES] = _rope(kva[:, 0:LANES], cos, sa, sb).astype(BF16)
        kva_o[r, :, LANES:2 * LANES] = kva[:, LANES:2 * LANES].astype(BF16)
    for r in R:
        qb_o[r] = (_dot(hs[r], w_ref[:, OFF_B:OFF_B + 512]) * qscale).astype(BF16)
    for r in R:
        kb_o[r] = _dot(hs[r], w_ref[:, OFF_B + 512:OFF_B + 1024]).astype(BF16)
    for r in R:
        vb_o[r] = _dot(hs[r], w_ref[:, OFF_B + 1024:OFF_C]).astype(BF16)
    for r in R:
        ua = _dot(hs[r], w_ref[:, OFF_C:OFF_C + C_CH])
        ug = _dot(hs[r], w_ref[:, OFF_C + C_CH:OFF_D])
        yc_o[r] = ua * jax.nn.sigmoid(ug)
    for r in R:
        qd_o[r] = _dot(hs[r], w_ref[:, OFF_D:OFF_D + 512]).astype(BF16)
    for r in R:
        kd_o[r] = (_dot(hs[r], w_ref[:, OFF_D + 512:OFF_D + 1024]) * (M_DIM ** -0.5)).astype(BF16)
    for r in R:
        vd_o[r] = _dot(hs[r], w_ref[:, OFF_D + 1024:OFF_D + 1536]).astype(BF16)
    for r in R:
        so_o[r] = jax.nn.sigmoid(_dot(hs[r], w_ref[:, OFF_D + 1536:OFF_G])).astype(BF16)
    for r in R:
        g_o[r] = _dot(hs[r], w_ref[:, OFF_G:PROJ_PAD])[:, 0:16] + gb_ref[...]


def _mod_groups(mod_l, nb, ns):
    d6 = mod_l.shape[-1]
    ng = nb // ns
    modg = jnp.concatenate([mod_l[:nb].reshape(ng, ns, 1, d6), jnp.broadcast_to(mod_l[nb], (1, ns, 1, d6))], axis=0)
    return modg.reshape((ng + 1) * ns, 1, d6)


def _inproj(x, mod_l, tabs, w_bf, layer, gate_b, lctx, prev=None):
    nb, tt, d = (x if prev is None else prev[0]).shape
    nt = tt // TM
    lt = lctx // TM
    ns = 2 if nb % 2 == 0 else 1
    ng = nb // ns
    tok = lambda n: pl.BlockSpec((ns, TM, n), lambda g, t: (g, t, 0))
    tab = pl.BlockSpec((TM, LANES), lambda g, t: (t, 0))
    vec = pl.BlockSpec((1, d), lambda g, t: (0, 0))
    modspec = pl.BlockSpec((ns, 1, 6 * d), lambda g, t: (jnp.where(t < lt, ng, g), 0, 0))
    sds = lambda n, dt: jax.ShapeDtypeStruct((nb, tt, n), dt)
    in_specs = [modspec, tab, tab, tab,
                pl.BlockSpec((1, d, PROJ_PAD), lambda g, t: (layer, 0, 0)),
                pl.BlockSpec((1, 16), lambda g, t: (0, 0))]
    args = [_mod_groups(mod_l, nb, ns), tabs[0], tabs[1], tabs[2], w_bf, gate_b.reshape(1, 16)]
    out_specs = [tok(512), tok(256), tok(512), tok(512), tok(512), tok(512),
                 tok(512), tok(512), tok(512), tok(512), tok(16)]
    out_shape = [sds(512, BF16), sds(256, BF16), sds(512, BF16), sds(512, BF16), sds(512, BF16),
                 sds(512, F32), sds(512, BF16), sds(512, BF16), sds(512, BF16), sds(512, BF16),
                 sds(16, F32)]
    if prev is None:
        in_specs = [tok(d)] + in_specs
        args = [x] + args
        alpha = None
    else:
        x1, ml, pmod_l, pg, pb, alpha = prev
        in_specs = [tok(d), tok(d), modspec, vec, vec] + in_specs
        args = [x1, ml, _mod_groups(pmod_l, nb, ns), pg.reshape(1, d), pb.reshape(1, d)] + args
        out_specs = [tok(d)] + out_specs
        out_shape = [sds(d, F32)] + out_shape
    return pl.pallas_call(
        functools.partial(_inproj_kernel, alpha=alpha, ns=ns),
        grid=(ng, nt),
        in_specs=in_specs,
        out_specs=out_specs,
        out_shape=out_shape,
        compiler_params=_cparams(("parallel", "parallel"), 56),
        name="inproj",
    )(*args)


def _attn_a_kernel(sink_ref, q_ref, kp_ref, kc_ref, kn_ref, kx_ref, mask_ref, o_ref, *, ns):
    mask = mask_ref[0]
    group = A_HEADS // A_KV_HEADS
    work = [(r, g) for g in range(A_KV_HEADS) for r in range(ns)]
    kvs = [jnp.concatenate([kp_ref[r], kc_ref[r], kn_ref[r], kx_ref[r]], axis=0) for r in range(ns)]
    ss = {}
    for r, g in work:
        k = kvs[r][:, g * HEAD_DIM:(g + 1) * HEAD_DIM]
        heads = range(g * group, (g + 1) * group)
        qs = jnp.concatenate([q_ref[r, :, hh * HEAD_DIM:(hh + 1) * HEAD_DIM] for hh in heads], axis=0)
        ss[r, g] = _dot_nt(qs, k)
    for r, g in work:
        v = kvs[r][:, LANES + g * HEAD_DIM:LANES + (g + 1) * HEAD_DIM]
        heads = range(g * group, (g + 1) * group)
        ps, ls = [], []
        for j, hh in enumerate(heads):
            sr = ss[r, g][j * BLK:(j + 1) * BLK] + mask
            sink = sink_ref[hh]
            m = jnp.maximum(jnp.max(sr, axis=-1, keepdims=True), sink)
            p = jnp.exp(sr - m)
            ls.append(jnp.sum(p, axis=-1, keepdims=True) + jnp.exp(sink - m))
            ps.append(p.astype(BF16))
        o = _dot(jnp.concatenate(ps, axis=0), v)
        for j, hh in enumerate(heads):
            o_ref[r, :, hh * HEAD_DIM:(hh + 1) * HEAD_DIM] = (o[j * BLK:(j + 1) * BLK] / ls[j]).astype(BF16)


def _attn_a_mask(lctx):
    i = np.arange(BLK)[:, None]
    j = np.arange(BLK)[None, :]
    ok_prev = (j >= i)
    ok_next = (j <= i)
    yes = np.ones((BLK, BLK), bool)
    no = np.zeros((BLK, BLK), bool)
    ctx = np.ones((BLK, lctx), bool)
    variants = [
        np.concatenate([ok_prev, yes, ok_next, ctx], 1),
        np.concatenate([no, yes, ok_next, ctx], 1),
        np.concatenate([ok_prev, yes, no, ctx], 1),
        np.concatenate([no, no, no, ctx], 1),
    ]
    return jnp.asarray(np.where(np.stack(variants), 0.0, NEG_INF).astype(np.float32))


def _attn_a(qa, kva, sink, lctx):
    nb, tt, _ = qa.shape
    nblk = tt // BLK
    lb = lctx // BLK
    assert nblk - lb >= 2
    mask = _attn_a_mask(lctx)

    def variant(t):
        return jnp.where(t < lb, 3, jnp.where(t == lb, 1, jnp.where(t == nblk - 1, 2, 0)))

    ns = 2 if nb % 2 == 0 else 1
    kvb = lambda f: pl.BlockSpec((ns, BLK, 256), lambda b, t: (b, f(t), 0))
    return pl.pallas_call(
        functools.partial(_attn_a_kernel, ns=ns),
        grid=(nb // ns, nblk),
        in_specs=[
            pl.BlockSpec(memory_space=pltpu.SMEM),
            pl.BlockSpec((ns, BLK, 512), lambda b, t: (b, t, 0)),
            kvb(lambda t: jnp.maximum(t - 1, 0)),
            kvb(lambda t: t),
            kvb(lambda t: jnp.minimum(t + 1, nblk - 1)),
            pl.BlockSpec((ns, lctx, 256), lambda b, t: (b, 0, 0)),
            pl.BlockSpec((1, BLK, 3 * BLK + lctx), lambda b, t: (variant(t), 0, 0)),
        ],
        out_specs=pl.BlockSpec((ns, BLK, 512), lambda b, t: (b, t, 0)),
        out_shape=jax.ShapeDtypeStruct((nb, tt, 512), BF16),
        compiler_params=_cparams(("parallel", "parallel")),
        name="attn_window",
    )(sink, qa, kva, kva, kva, kva, mask)


NB_KBLK = 5


def _attn_b_kernel(q_ref, k0, k1, k2, k3, k4, kx, v0, v1, v2, v3, v4, vx, bias_ref, o_ref, *, lb, ns):
    nloc = NB_KBLK * BLK
    first = lax.broadcasted_iota(jnp.int32, (1, LANES), 1) < HEAD_DIM

    def attend(ks, vs, with_bias):
        work = [(r, pair) for pair in range(B_HEADS // 2) for r in range(ns)]
        s2s = {}
        for r, pair in work:
            sl = slice(pair * LANES, (pair + 1) * LANES)
            q2 = q_ref[r, :, sl]
            zero = jnp.zeros_like(q2)
            s2s[r, pair] = _dot_nt(
                jnp.concatenate([jnp.where(first, q2, zero), jnp.where(first, zero, q2)], axis=0), ks[r][:, sl])
        for r, pair in work:
            sl = slice(pair * LANES, (pair + 1) * LANES)
            ps, ls = [], []
            for j in range(2):
                s = s2s[r, pair][j * BLK:(j + 1) * BLK]
                if with_bias:
                    s = jnp.concatenate([s[:, :nloc] + bias_ref[0, 0, 2 * pair + j], s[:, nloc:]], axis=1)
                p = jnp.exp(s - jnp.max(s, axis=-1, keepdims=True))
                ls.append(jnp.sum(p, axis=-1, keepdims=True))
                ps.append(p.astype(BF16))
            o2 = _dot(jnp.concatenate(ps, axis=0), vs[r][:, sl])
            o_ref[r, :, sl] = jnp.where(first, o2[:BLK] / ls[0], o2[BLK:] / ls[1]).astype(BF16)

    is_ctx = pl.program_id(1) < lb

    @pl.when(is_ctx)
    def _():
        attend([kx[r] for r in range(ns)], [vx[r] for r in range(ns)], False)

    @pl.when(jnp.logical_not(is_ctx))
    def _():
        attend([jnp.concatenate([k0[r], k1[r], k2[r], k3[r], k4[r], kx[r]], axis=0) for r in range(ns)],
               [jnp.concatenate([v0[r], v1[r], v2[r], v3[r], v4[r], vx[r]], axis=0) for r in range(ns)], True)


def _attn_b_bias(rpb, nlat):
    rows = 2 * nlat
    wh = min(NA_ROWS, rows)
    n = GRID_W
    qrows, krows = BLK // n, NB_KBLK * BLK // n
    nvar = NB_KBLK
    edge = n - NA_COLS
    g = jnp.concatenate([jnp.repeat(rpb[..., :1], edge, axis=-1), rpb.astype(F32),
                         jnp.repeat(rpb[..., -1:], edge + 1, axis=-1)], axis=-1)
    g = jnp.roll(g, -(n - 1), axis=-1)
    toep = jnp.tile(g, (1, 1, 1, n))[..., :n * (2 * n - 1)].reshape(g.shape[:3] + (n, 2 * n - 1))[..., :n]

    reps = [0, 1, 2, nlat - 2, nlat - 1]
    i = np.arange(BLK)
    kk = np.arange(NB_KBLK * BLK)
    ok, dr = [], []
    for dlt, j in enumerate(reps):
        base = j - dlt
        r = 2 * j + i // n
        qc = i % n
        kr = 2 * base + kk // n
        kc = kk % n
        start = np.clip(r - wh // 2, 0, rows - wh)
        row_ok = (kr[None, :] >= start[:, None]) & (kr[None, :] < start[:, None] + wh)
        c0 = np.clip(qc - NA_COLS // 2, 0, n - NA_COLS)
        col_ok = (kc[None, :] >= c0[:, None]) & (kc[None, :] < c0[:, None] + NA_COLS)
        ok.append(row_ok & col_ok)
        dr.append(np.clip(2 * (base - j) + np.arange(krows)[None, :] - np.arange(qrows)[:, None] + NA_ROWS - 1,
                          0, 2 * NA_ROWS - 2))
    depth, nh = rpb.shape[:2]
    bias = jnp.stack([jnp.concatenate([jnp.concatenate([toep[:, :, dr[v][a, b]] for b in range(krows)], axis=-1)
                                       for a in range(qrows)], axis=-2) for v in range(nvar)], axis=1)
    return jnp.where(jnp.asarray(np.stack(ok))[None, :, None], bias, NEG_INF)


def _attn_b(qb, kb, vb, bias, layer, lctx):
    nb, tt, _ = qb.shape
    nblk = tt // BLK
    lb = lctx // BLK
    nlat = nblk - lb

    def base(t):
        return jnp.clip(t - lb - 2, 0, nlat - NB_KBLK) + lb

    def variant(t):
        return jnp.where(t < lb, 0, t - base(t))

    ns = 2 if nb % 2 == 0 else 1
    loc = lambda i: pl.BlockSpec((ns, BLK, 512), lambda b, t: (b, base(t) + i, 0))
    ctx = pl.BlockSpec((ns, lctx, 512), lambda b, t: (b, 0, 0))
    return pl.pallas_call(
        functools.partial(_attn_b_kernel, lb=lb, ns=ns),
        grid=(nb // ns, nblk),
        in_specs=[pl.BlockSpec((ns, BLK, 512), lambda b, t: (b, t, 0))]
        + [loc(i) for i in range(NB_KBLK)] + [ctx]
        + [loc(i) for i in range(NB_KBLK)] + [ctx]
        + [pl.BlockSpec((1, 1, B_HEADS, BLK, NB_KBLK * BLK), lambda b, t: (layer, variant(t), 0, 0, 0))],
        out_specs=pl.BlockSpec((ns, BLK, 512), lambda b, t: (b, t, 0)),
        out_shape=jax.ShapeDtypeStruct((nb, tt, 512), BF16),
        compiler_params=_cparams(("parallel", "parallel"), 40),
        name="attn_neighbourhood",
    )(qb, *([kb] * (NB_KBLK + 1)), *([vb] * (NB_KBLK + 1)), bias)


HALO = 16


SUB = 8


NCB = C_CH // LANES


def _conv_kernel(prev_ref, cur_ref, next_ref, w_ref, b_ref, g_ref, bb_ref, o_ref, sh_ref, acc_ref, *, lb, nblk):
    t = pl.program_id(1)
    has_prev = jnp.logical_and(t != 0, t != lb)
    has_next = jnp.logical_and(t != lb - 1, t != nblk - 1)
    prev = jnp.where(has_prev, prev_ref[0], 0.0)
    nxt = jnp.where(has_next, next_ref[0], 0.0)
    cur = cur_ref[0]
    for cb in range(NCB):
        cs = slice(cb * LANES, (cb + 1) * LANES)
        sh_ref[0, cb, 0:HALO, :] = prev[:, cs]
        sh_ref[0, cb, HALO:HALO + BLK, :] = cur[:, cs]
        sh_ref[0, cb, HALO + BLK:2 * HALO + BLK, :] = nxt[:, cs]
    rows = BLK + 2 * HALO - SUB
    pad = C_KSIZE // 2

    def channel_block(cb, carry):
        for r in range(1, SUB):
            sh_ref[r, cb, 0:rows, :] = sh_ref[0, cb, pl.ds(r, rows), :]
        acc = jnp.zeros((BLK // SUB, SUB, LANES), F32)
        for kk in range(C_KSIZE):
            off = HALO - pad + kk
            rows_k = sh_ref[off % SUB, cb, off - off % SUB:off - off % SUB + BLK, :]
            acc = acc + rows_k.reshape(BLK // SUB, SUB, LANES) * w_ref[kk, cb][None]
        acc_ref[cb] = acc.reshape(BLK, LANES)
        return carry

    lax.fori_loop(0, NCB, channel_block, 0)
    y = jnp.concatenate([acc_ref[cb] for cb in range(NCB)], axis=1)
    y = _ln(y + b_ref[...]) * g_ref[...] + bb_ref[...]
    o_ref[0] = (y * jax.nn.sigmoid(y)).astype(BF16)


def _conv(yc, w, b, g, bb, lctx):
    nb, tt, ch = yc.shape
    nblk = tt // BLK
    lb = lctx // BLK
    per = BLK // HALO
    vec = pl.BlockSpec((1, ch), lambda b_, t: (0, 0))
    return pl.pallas_call(
        functools.partial(_conv_kernel, lb=lb, nblk=nblk),
        grid=(nb, nblk),
        in_specs=[
            pl.BlockSpec((1, HALO, ch), lambda b_, t: (b_, jnp.maximum(t * per - 1, 0), 0)),
            pl.BlockSpec((1, BLK, ch), lambda b_, t: (b_, t, 0)),
            pl.BlockSpec((1, HALO, ch), lambda b_, t: (b_, jnp.minimum((t + 1) * per, nblk * per - 1), 0)),
            pl.BlockSpec((C_KSIZE, NCB, SUB, LANES), lambda b_, t: (0, 0, 0, 0)),
            vec, vec, vec,
        ],
        out_specs=pl.BlockSpec((1, BLK, ch), lambda b_, t: (b_, t, 0)),
        out_shape=jax.ShapeDtypeStruct((nb, tt, ch), BF16),
        scratch_shapes=[pltpu.VMEM((SUB, NCB, BLK + 2 * HALO, LANES), F32), pltpu.VMEM((NCB, BLK, LANES), F32)],
        compiler_params=_cparams(("parallel", "parallel")),
        name="conformer_conv",
    )(yc, yc, yc, jnp.broadcast_to(w.reshape(C_KSIZE, NCB, 1, LANES), (C_KSIZE, NCB, SUB, LANES)),
      b.reshape(1, ch), g.reshape(1, ch),
      bb.reshape(1, ch))


def _mlstm_kernel(qf_ref, kf_ref, vf_ref, gcf_ref, grf_ref, qb_ref, kb_ref, vb_ref, gcb_ref, grb_ref,
                  hf_ref, hb_ref, c_st, n_st, m_st, *, mb):
    @pl.when(pl.program_id(1) == 0)
    def _():
        c_st[...] = jnp.zeros_like(c_st)
        n_st[...] = jnp.zeros_like(n_st)
        m_st[...] = jnp.zeros_like(m_st)

    rr = lax.broadcasted_iota(jnp.int32, (BLK, BLK), 0)
    cc = lax.broadcasted_iota(jnp.int32, (BLK, BLK), 1)
    ch = []
    dirs = ((qf_ref, kf_ref, vf_ref, gcf_ref, grf_ref, hf_ref), (qb_ref, kb_ref, vb_ref, gcb_ref, grb_ref, hb_ref))
    for bb, dirn in [(bb, dirn) for bb in range(mb) for dirn in range(2)]:
        q_ref, k_ref, v_ref, gc_ref, gr_ref, h_ref = dirs[dirn]
        before = (rr >= cc) if dirn == 0 else (rr <= cc)
        bmat = jnp.where(before, 1.0, 0.0).astype(BF16)
        gc = gc_ref[0, bb]
        gr = gr_ref[0, bb]
        lf_c = jax.nn.log_sigmoid(gc[:, M_HEADS:2 * M_HEADS])
        lf_r = jax.nn.log_sigmoid(gr[M_HEADS:2 * M_HEADS, :])
        fc = sum(_dot(bmat, part) for part in _split3(lf_c))
        fr = sum(_dot_nt(part, bmat) for part in _split3(lf_r))
        f_tot = jnp.sum(lf_r, axis=-1, keepdims=True)
        for hh in range(M_HEADS):
            sl = slice(hh * M_DIM, (hh + 1) * M_DIM)
            st = (bb * 2 + dirn) * M_HEADS + hh
            ch.append(dict(before=before, sl=sl, st=st, h_ref=h_ref, bb=bb,
                           q=q_ref[bb, :, sl], k=k_ref[bb, :, sl], v=v_ref[bb, :, sl],
                           f_c=fc[:, hh:hh + 1], f_r=fr[hh:hh + 1, :], f_tot=f_tot[hh:hh + 1, :],
                           i_c=gc[:, hh:hh + 1], i_r=gr[hh:hh + 1, :],
                           c_old=c_st[st], n_old=n_st[st], m_old=m_st[st][:, 0:1]))

    for c in ch:
        c["qk"] = _dot_nt(c["q"], c["k"])
        c["qc"] = _dot(c["q"], c["c_old"].astype(BF16))
    for c in ch:
        c["a"] = c["f_c"] + c["m_old"]
        c["logw"] = jnp.where(c["before"], c["f_c"] - c["f_r"] + c["i_r"], -jnp.inf)
        c["mt"] = jnp.maximum(c["a"], jnp.max(c["logw"], axis=-1, keepdims=True))
    for c in ch:
        g_r = c["f_tot"] - c["f_r"] + c["i_r"]
        c["m_new"] = jnp.maximum(c["f_tot"] + c["m_old"], jnp.max(g_r, axis=-1, keepdims=True))
        c["decay"] = jnp.exp(c["f_tot"] + c["m_old"] - c["m_new"])
        c["kw"] = c["k"].astype(F32) * jnp.exp(c["f_tot"] - c["f_c"] + c["i_c"] - c["m_new"])
    for c in ch:
        c["s"] = c["qk"] * jnp.exp(c["logw"] - c["mt"])
        c["w_inter"] = jnp.exp(c["a"] - c["mt"])
    for c in ch:
        c["sv"] = _dot(c["s"].astype(BF16), c["v"])
        c["kv"] = lax.dot_general(c["kw"].astype(BF16), c["v"], (((0,), (0,)), ((), ())), preferred_element_type=F32)
    for c in ch:
        num = c["w_inter"] * c["qc"] + c["sv"]
        den = (c["w_inter"] * jnp.sum(c["q"].astype(F32) * c["n_old"], axis=-1, keepdims=True)
               + jnp.sum(c["s"], axis=-1, keepdims=True))
        hout = num / jnp.maximum(jnp.abs(den), jnp.exp(-c["mt"]))
        c["h_ref"][c["bb"], :, c["sl"]] = hout.astype(BF16)
    for c in ch:
        st = c["st"]
        c_st[st] = c["decay"] * c["c_old"] + c["kv"]
        n_st[st] = c["decay"] * c["n_old"] + jnp.sum(c["kw"], axis=0, keepdims=True)
        m_st[st] = jnp.broadcast_to(c["m_new"], (1, LANES))


def _mlstm(qd, kd, vd, gates, lctx):
    nb, tt, _ = qd.shape
    nblk = tt // BLK
    lb = lctx // BLK
    gcol = jnp.stack([gates[..., 0:8], gates[..., 8:16]])
    grow = jnp.swapaxes(gcol, 2, 3)

    def bwd(i):
        return jnp.where(i < lb, lb - 1 - i, nblk - 1 + lb - i)

    fwd = lambda i: i
    mb = max(m for m in (4, 2, 1) if nb % m == 0)
    tok = lambda f: pl.BlockSpec((mb, BLK, 512), lambda b, i: (b, f(i), 0))
    gcs = lambda d, f: pl.BlockSpec((1, mb, BLK, 8), lambda b, i: (d, b, f(i), 0))
    grs = lambda d, f: pl.BlockSpec((1, mb, 8, BLK), lambda b, i: (d, b, 0, f(i)))
    out = jax.ShapeDtypeStruct((nb, tt, 512), BF16)
    nst = 2 * mb * M_HEADS
    return pl.pallas_call(
        functools.partial(_mlstm_kernel, mb=mb),
        grid=(nb // mb, nblk),
        in_specs=[tok(fwd), tok(fwd), tok(fwd), gcs(0, fwd), grs(0, fwd),
                  tok(bwd), tok(bwd), tok(bwd), gcs(1, bwd), grs(1, bwd)],
        out_specs=[tok(fwd), tok(bwd)],
        out_shape=[out, out],
        scratch_shapes=[pltpu.VMEM((nst, M_DIM, M_DIM), F32),
                        pltpu.VMEM((nst, 1, M_DIM), F32),
                        pltpu.VMEM((nst, 1, LANES), F32)],
        compiler_params=_cparams(("parallel", "arbitrary")),
        name="mlstm",
    )(qd, kd, vd, gcol, grow, qd, kd, vd, gcol, grow)


def _merge_kernel(x_ref, mod_ref, ya_ref, yb_ref, yc_ref, hf_ref, hb_ref, so_ref,
                  wg_ref, bg_ref, wbr_ref, wo_ref, g1_ref, b1_ref, wr_ref,
                  x1_o, hp_o, aff_o, *, alpha, ns):
    d = D_MODEL
    R = range(ns)
    mods = [mod_ref[r] for r in R]
    xs = [x_ref[r] for r in R]
    hs = [(_ln(xs[r]) * (1.0 + mods[r][:, d:2 * d]) + mods[r][:, 0:d]).astype(BF16) for r in R]
    yds = [(so_ref[r].astype(F32) * (hf_ref[r].astype(F32) + hb_ref[r].astype(F32))).astype(BF16) for r in R]
    yss = [(ya_ref[r], yb_ref[r], yc_ref[r], yds[r]) for r in R]
    pres = [[_dot(hs[r], wg_ref[0, :, i * d:(i + 1) * d]) for i in range(N_BRANCH)] for r in R]
    brs = [[_dot(yss[r][i], wbr_ref[0, i]) for i in range(N_BRANCH)] for r in R]
    zs = []
    for r in R:
        z = None
        for i in range(N_BRANCH):
            term = jax.nn.sigmoid(pres[r][i] + bg_ref[:, i * d:(i + 1) * d]) * brs[r][i]
            z = term if z is None else z + term
        zs.append(z)
    ys = [_dot(zs[r].astype(BF16), wo_ref[0]) for r in R]
    x1s = [_ln(alpha * xs[r] + mods[r][:, 2 * d:3 * d] * ys[r]) * g1_ref[...] + b1_ref[...] for r in R]
    for r in R:
        x1_o[r] = x1s[r]
    h2s = [_ln(x1s[r]) * (1.0 + mods[r][:, 4 * d:5 * d]) + mods[r][:, 3 * d:4 * d] for r in R]
    for r in R:
        h2 = h2s[r]
        hb16 = h2.astype(BF16)
        bits = pltpu.bitcast(hb16.astype(F32), jnp.uint32)
        hp_o[r] = (bits[:, d // 2:] & jnp.uint32(0xFFFF0000)) | (bits[:, :d // 2] >> 16)
        h2_hi, h2_lo = _split2(h2)
        r_hi = _dot(h2_hi, wr_ref[...])
        r_lo = _dot(h2_lo, wr_ref[...])
        logits = r_hi[:, 0:N_EXPERTS] + (r_hi[:, N_EXPERTS:2 * N_EXPERTS] + r_lo[:, 0:N_EXPERTS])
        e = jnp.exp(logits - jnp.max(logits, axis=-1, keepdims=True))
        aff_o[r] = e / jnp.sum(e, axis=-1, keepdims=True)


def _merge(x, mod_all_l, ya, yb, yc, hf, hb, so, wg, bg, wbr, wo, layer, g1, b1, wr, lctx, alpha):
    nb, tt, d = x.shape
    nt = tt // TM
    lt = lctx // TM
    ns = 2 if nb % 2 == 0 else 1
    ng = nb // ns
    modg = _mod_groups(mod_all_l, nb, ns)
    tok = lambda n: pl.BlockSpec((ns, TM, n), lambda g, t: (g, t, 0))
    const = lambda shape: pl.BlockSpec(shape, lambda g, t: (0,) * len(shape))
    stacked = lambda shape: pl.BlockSpec((1,) + shape, lambda g, t: (layer,) + (0,) * len(shape))
    return pl.pallas_call(
        functools.partial(_merge_kernel, alpha=alpha, ns=ns),
        grid=(ng, nt),
        in_specs=[
            tok(d),
            pl.BlockSpec((ns, 1, 6 * d), lambda g, t: (jnp.where(t < lt, ng, g), 0, 0)),
            tok(512), tok(512), tok(512), tok(512), tok(512), tok(512),
            stacked((d, N_BRANCH * d)), const((1, N_BRANCH * d)), stacked((N_BRANCH, BRANCH_W, d)),
            stacked((d, d)), const((1, d)), const((1, d)), const((d, LANES)),
        ],
        out_specs=[tok(d), tok(d // 2), tok(N_EXPERTS)],
        out_shape=[jax.ShapeDtypeStruct((nb, tt, d), F32),
                   jax.ShapeDtypeStruct((nb, tt, d // 2), jnp.uint32),
                   jax.ShapeDtypeStruct((nb, tt, N_EXPERTS), F32)],
        compiler_params=_cparams(("parallel", "parallel"), 56),
        name="merge",
    )(x, modg, ya, yb, yc, hf, hb, so, wg, bg.reshape(1, -1), wbr, wo,
      g1.reshape(1, d), b1.reshape(1, d), wr)


def _route_one(a_ref, ones_ref, tab_ref, idx_o, val_o, *, ntok, cap, tok_off, slot_off):
    a = a_ref[0]
    bits = pltpu.bitcast(a, jnp.int32)
    thr = jnp.zeros((N_EXPERTS, 1), jnp.int32)
    for bit in range(30, -1, -1):
        cand = thr | jnp.int32(1 << bit)
        cnt = jnp.sum(jnp.where(bits >= cand, 1.0, 0.0), axis=-1, keepdims=True)
        thr = jnp.where(cnt >= cap, cand, thr)
    gt = bits > thr
    eq = bits == thr
    need = cap - jnp.sum(jnp.where(gt, 1.0, 0.0), axis=-1, keepdims=True)

    rr = lax.broadcasted_iota(jnp.int32, (LANES, LANES), 0)
    cc = lax.broadcasted_iota(jnp.int32, (LANES, LANES), 1)
    upper = jnp.where(rr <= cc, 1.0, 0.0).astype(BF16)

    def cumsum_blocks(mask_f):
        run = jnp.zeros((N_EXPERTS, 1), F32)
        out = []
        for c in range(ntok // LANES):
            blk = mask_f[:, c * LANES:(c + 1) * LANES]
            out.append(_dot(blk.astype(BF16), upper) + run)
            run = run + jnp.sum(blk, axis=-1, keepdims=True)
        return out

    eq_f = jnp.where(eq, 1.0, 0.0)
    cum_eq = cumsum_blocks(eq_f)
    sel_parts = []
    for c in range(ntok // LANES):
        sl = slice(c * LANES, (c + 1) * LANES)
        sel_parts.append(jnp.where(gt[:, sl] | (eq[:, sl] & (cum_eq[c] <= need)), 1.0, 0.0))
    sel_f = jnp.concatenate(sel_parts, axis=1)
    cpos = cumsum_blocks(sel_f)
    nblk = ntok // LANES
    tab_ref[...] = jnp.zeros_like(tab_ref)
    for c in range(nblk):
        for e in range(N_EXPERTS):
            tab_ref[0, e, c:c + 1, :] = cpos[c][e:e + 1, :]
            tab_ref[1, e, c:c + 1, :] = a_ref[0, e:e + 1, c * LANES:(c + 1) * LANES]
    counts = _dot(sel_f.astype(BF16), ones_ref[...])
    blockend = _dot(counts.astype(BF16), upper)
    prevend = blockend - counts

    lane = lax.broadcasted_iota(jnp.int32, (1, LANES), 1).astype(F32)
    slot = lax.broadcasted_iota(jnp.int32, (cap, 1), 0).astype(F32)
    for e in range(N_EXPERTS):
        be = blockend[e:e + 1, :]
        pe = prevend[e:e + 1, :]
        pick = jnp.where(pe <= slot, jnp.where(slot < be, 1.0, 0.0), 0.0).astype(BF16)
        cnt_blk = sum(_dot(pick, part) for part in _split2(tab_ref[0, e]))
        aff_blk = sum(_dot(pick, part) for part in _split3(tab_ref[1, e]))
        within = jnp.sum(jnp.where(cnt_blk <= slot, 1.0, 0.0), axis=-1, keepdims=True)
        nfull = jnp.sum(jnp.where(be <= slot, 1.0, 0.0), axis=-1, keepdims=True)
        idx_o[0, e, slot_off:slot_off + cap, :] = (nfull * LANES + within + tok_off).astype(jnp.int32)
        val_o[0, e, slot_off:slot_off + cap, :] = jnp.sum(jnp.where(lane == within, aff_blk, 0.0),
                                                           axis=-1, keepdims=True)


def _route_kernel(al_ref, ac_ref, onesl_ref, onesc_ref, idx_o, val_o, tab_ref, *, s, lctx, cap_l, cap_c):
    _route_one(al_ref, onesl_ref, tab_ref, idx_o, val_o, ntok=s, cap=cap_l, tok_off=lctx, slot_off=0)
    _route_one(ac_ref, onesc_ref, tab_ref, idx_o, val_o, ntok=lctx, cap=cap_c, tok_off=0, slot_off=cap_l)


def _route(aff, lctx):
    nb, tt, ne = aff.shape
    s = tt - lctx
    cap_l = CAPACITY_FACTOR * s // ne
    cap_c = CAPACITY_FACTOR * lctx // ne
    capt = cap_l + cap_c
    assert s // LANES <= LANES and cap_l % 8 == 0 and cap_c % 8 == 0
    aff_t = jnp.swapaxes(aff, 1, 2)
    ones = lambda n: jnp.asarray(np.arange(n)[:, None] // LANES == np.arange(LANES)[None, :], BF16)
    out = lambda dt: jax.ShapeDtypeStruct((nb, ne, capt, 1), dt)
    return pl.pallas_call(
        functools.partial(_route_kernel, s=s, lctx=lctx, cap_l=cap_l, cap_c=cap_c),
        grid=(nb,),
        in_specs=[pl.BlockSpec((1, ne, s), lambda b: (b, 0, 0)),
                  pl.BlockSpec((1, ne, lctx), lambda b: (b, 0, 0)),
                  pl.BlockSpec((s, LANES), lambda b: (0, 0)),
                  pl.BlockSpec((lctx, LANES), lambda b: (0, 0))],
        out_specs=[pl.BlockSpec((1, ne, capt, 1), lambda b: (b, 0, 0, 0))] * 2,
        out_shape=[out(jnp.int32), out(F32)],
        scratch_shapes=[pltpu.VMEM((2, ne, LANES, LANES), F32)],
        compiler_params=_cparams(("parallel",), 40),
        name="route",
    )(aff_t[:, :, lctx:], aff_t[:, :, :lctx], ones(s), ones(lctx))


def _gather_kernel(idx_ref, x_ref, o_ref, *, capt):
    def body(i, carry):
        c0 = pl.multiple_of(i * SUB, SUB)
        rows = [x_ref[0, pl.ds(idx_ref[0, 0, c0 + j], 1), :] for j in range(SUB)]
        o_ref[0, 0, pl.ds(c0, SUB), :] = jnp.concatenate(rows, axis=0)
        return carry
    lax.fori_loop(0, capt // SUB, body, 0)


def _gather(idx_s, hp, capt):
    nb, tt, w = hp.shape
    assert capt % SUB == 0
    ne = N_EXPERTS
    return pl.pallas_call(
        functools.partial(_gather_kernel, capt=capt),
        grid=(nb, ne),
        in_specs=[pl.BlockSpec((1, 1, capt), lambda b, e: (b * ne + e, 0, 0), memory_space=pltpu.SMEM),
                  pl.BlockSpec((1, tt, w), lambda b, e: (b, 0, 0))],
        out_specs=pl.BlockSpec((1, 1, capt, w), lambda b, e: (b, e, 0, 0)),
        out_shape=jax.ShapeDtypeStruct((nb, ne, capt, w), jnp.uint32),
        compiler_params=_cparams(("parallel", "arbitrary"), 40),
        name="moe_gather",
    )(idx_s, hp)


FF_CHUNK = 256


def _ffn_kernel(x_ref, w1_ref, w3_ref, w2_ref, val_ref, y_ref, w1b, w3b, w2b):
    @pl.when(pl.program_id(1) == 0)
    def _():
        w1b[...] = w1_ref[0, 0].astype(BF16)
        w3b[...] = w3_ref[0, 0].astype(BF16)
        w2b[...] = w2_ref[0, 0].astype(BF16)

    packed = x_ref[0, 0]
    lo = pltpu.bitcast(packed << 16, F32)
    hi = pltpu.bitcast(packed & jnp.uint32(0xFFFF0000), F32)
    xg = jnp.concatenate([lo, hi], axis=1).astype(BF16)
    y = None
    for c in range(EXPERT_FF // FF_CHUNK):
        cs = slice(c * FF_CHUNK, (c + 1) * FF_CHUNK)
        a = _dot(xg, w1b[:, cs])
        hid = (a * jax.nn.sigmoid(a) * _dot(xg, w3b[:, cs])).astype(BF16)
        part = _dot(hid, w2b[cs, :])
        y = part if y is None else y + part
    y_ref[0, 0] = y * val_ref[0, 0]


def _ffn(xg, w1, w3, w2, vals, layer):
    nb, ne, capt, w = xg.shape
    d, ff = w1.shape[2], w1.shape[3]
    return pl.pallas_call(
        _ffn_kernel,
        grid=(ne, nb),
        in_specs=[pl.BlockSpec((1, 1, capt, w), lambda e, b: (b, e, 0, 0)),
                  pl.BlockSpec((1, 1, d, ff), lambda e, b: (layer, e, 0, 0)),
                  pl.BlockSpec((1, 1, d, ff), lambda e, b: (layer, e, 0, 0)),
                  pl.BlockSpec((1, 1, ff, d), lambda e, b: (layer, e, 0, 0)),
                  pl.BlockSpec((1, 1, capt, 1), lambda e, b: (b, e, 0, 0))],
        out_specs=pl.BlockSpec((1, 1, capt, d), lambda e, b: (b, e, 0, 0)),
        out_shape=jax.ShapeDtypeStruct((nb, ne, capt, d), F32),
        scratch_shapes=[pltpu.VMEM((d, ff), BF16), pltpu.VMEM((d, ff), BF16), pltpu.VMEM((ff, d), BF16)],
        compiler_params=_cparams(("parallel", "arbitrary"), 56),
        name="moe_ffn",
    )(xg, w1, w3, w2, vals)


SCATTER_GROUP = 8


def _scatter_kernel(idx_ref, y_ref, o_ref, *, capt):
    @pl.when(pl.program_id(1) == 0)
    def _():
        o_ref[...] = jnp.zeros_like(o_ref)

    def body(i, carry):
        c0 = pl.multiple_of(i * SCATTER_GROUP, SCATTER_GROUP)
        rows = [pl.ds(idx_ref[0, 0, c0 + j], 1) for j in range(SCATTER_GROUP)]
        acc = [o_ref[0, r, :] for r in rows]
        ytile = y_ref[0, 0, pl.ds(c0, SCATTER_GROUP), :]
        new = [a + ytile[j:j + 1, :] for j, a in enumerate(acc)]
        for r, v in zip(rows, new):
            o_ref[0, r, :] = v
        return carry
    lax.fori_loop(0, capt // SCATTER_GROUP, body, 0)


def _scatter(idx_s, y, tt):
    nb, ne, capt, d = y.shape
    assert capt % SCATTER_GROUP == 0
    return pl.pallas_call(
        functools.partial(_scatter_kernel, capt=capt),
        grid=(nb, ne),
        in_specs=[pl.BlockSpec((1, 1, capt), lambda b, e: (b * ne + e, 0, 0), memory_space=pltpu.SMEM),
                  pl.BlockSpec((1, 1, capt, d), lambda b, e: (b, e, 0, 0))],
        out_specs=pl.BlockSpec((1, tt, d), lambda b, e: (b, 0, 0)),
        out_shape=jax.ShapeDtypeStruct((nb, tt, d), F32),
        compiler_params=_cparams(("parallel", "arbitrary"), 56),
        name="moe_scatter",
    )(idx_s, y)


def _post_kernel(x_ref, ml_ref, mod_ref, g_ref, b_ref, o_ref, *, alpha):
    d = D_MODEL
    g2 = mod_ref[0][:, 5 * d:6 * d]
    o_ref[0] = _ln(alpha * x_ref[0] + g2 * ml_ref[0]) * g_ref[...] + b_ref[...]


def _post(x1, ml, mod3, g, b, lctx, alpha):
    nb, tt, d = x1.shape
    lt = lctx // TM
    t0 = lt
    tok = pl.BlockSpec((1, TM, d), lambda b_, t: (b_, t + t0, 0))
    vec = pl.BlockSpec((1, d), lambda b_, t: (0, 0))
    return pl.pallas_call(
        functools.partial(_post_kernel, alpha=alpha),
        grid=(nb, tt // TM - t0),
        in_specs=[tok, tok, pl.BlockSpec((1, 1, 6 * d), lambda b_, t: (jnp.where(t + t0 < lt, nb, b_), 0, 0)),
                  vec, vec],
        out_specs=pl.BlockSpec((1, TM, d), lambda b_, t: (b_, t, 0)),
        out_shape=jax.ShapeDtypeStruct((nb, tt - t0 * TM, d), F32),
        compiler_params=_cparams(("parallel", "parallel")),
        name="moe_post",
    )(x1, ml, mod3, g.reshape(1, d), b.reshape(1, d))


def _rope_tables(s, lctx):
    half = HEAD_DIM // 2
    nf = half // 2
    inv = ROPE_BASE ** (-jnp.arange(nf, dtype=F32) / nf)
    t = jnp.arange(s)
    lane = np.arange(LANES)
    dd = lane % HEAD_DIM
    use_col = jnp.asarray(dd >= half)[None, :]
    pos = jnp.where(use_col, (t % GRID_W)[:, None], (t // GRID_W)[:, None]).astype(F32)
    ang = pos * inv[jnp.asarray(dd % nf)][None, :]
    cos, sin = jnp.cos(ang), jnp.sin(ang)
    first = jnp.asarray((dd % half) < nf)[None, :]
    sa = jnp.where(first, -sin, 0.0)
    sb = jnp.where(first, 0.0, sin)
    pad = lambda a, v: jnp.concatenate([jnp.full((lctx, LANES), v, F32), a], axis=0)
    return pad(cos, 1.0), pad(sa, 0.0), pad(sb, 0.0)


def kernel(x, c, ctx, c_ctx, w_mod, b_mod, w_in, attn_sink, na_rpb, conv_w, conv_b, conv_ln_g, conv_ln_b,
           mlstm_gate_b, w_branch, w_gate, b_gate, w_out, ln1_g, ln1_b, w_router, w_exp_gate, w_exp_up,
           w_exp_down, ln2_g, ln2_b):
    nb, s, d = x.shape
    lctx = ctx.shape[1]
    depth = w_mod.shape[0]
    assert d == D_MODEL and nb + 1 <= 8 and lctx % TM == 0 and s % TM == 0 and s % GRID_W == 0
    alpha = (2.0 * depth) ** 0.25
    tt = lctx + s
    cap_t = CAPACITY_FACTOR * s // N_EXPERTS + CAPACITY_FACTOR * lctx // N_EXPERTS

    c8 = jnp.concatenate([c, c_ctx[None, :], jnp.zeros((8 - nb - 1, d), F32)], axis=0)
    mod_all = _modulation(c8, w_mod, b_mod)
    tabs = _rope_tables(s, lctx)
    xs = jnp.concatenate([ctx, x], axis=1)
    nlat = s // BLK
    assert nlat >= NB_KBLK
    bias_b = _attn_b_bias(na_rpb, nlat)
    w_in_bf = jnp.pad(w_in, ((0, 0), (0, 0), (0, PROJ_PAD - PROJ_W))).astype(BF16)
    w_gate_bf, w_branch_bf, w_out_bf = w_gate.astype(BF16), w_branch.astype(BF16), w_out.astype(BF16)

    for l in range(depth):
        mod3 = mod_all[l].reshape(8, 1, 6 * d)
        if l == 0:
            outs = _inproj(xs, mod_all[l], tabs, w_in_bf, l, mlstm_gate_b[l], lctx)
        else:
            xs, *outs = _inproj(None, mod_all[l], tabs, w_in_bf, l, mlstm_gate_b[l], lctx, prev=prev)
        (qa, kva, qb, kb, vb, yc0, qd, kd, vd, so, gates) = outs
        ya = _attn_a(qa, kva, attn_sink[l], lctx)
        yb = _attn_b(qb, kb, vb, bias_b, l, lctx)
        yc = _conv(yc0, conv_w[l], conv_b[l], conv_ln_g[l], conv_ln_b[l], lctx)
        hf, hb = _mlstm(qd, kd, vd, gates, lctx)
        wr_hi, wr_lo = _split2(w_router[l])
        wr = jnp.pad(jnp.concatenate([wr_hi, wr_lo], axis=1), ((0, 0), (0, LANES - 2 * N_EXPERTS)))
        x1, hp, aff = _merge(xs, mod_all[l], ya, yb, yc, hf, hb, so, w_gate_bf, b_gate[l], w_branch_bf, w_out_bf, l,
                             ln1_g[l], ln1_b[l], wr, lctx, alpha)
        idx, vals = _route(aff, lctx)
        idx_s = idx.reshape(nb * N_EXPERTS, 1, cap_t)
        xg = _gather(idx_s, hp, cap_t)
        y = _ffn(xg, w_exp_gate, w_exp_up, w_exp_down, vals, l)
        ml = _scatter(idx_s, y, tt)
        prev = (x1, ml, mod_all[l], ln2_g[l], ln2_b[l], alpha)
    return _post(x1, ml, mod3, ln2_g[depth - 1], ln2_b[depth - 1], lctx, alpha)
```
